```python
import jax, jax.numpy as jnp
from jax import lax
import numpy as np

D_MODEL = 1024
BATCH = 8
SEQ = 2048
DEPTH = 4
DEC_BATCH = 32
DEC_SEQ = 64
PAST_LEN = 1024

CHUNK = 64
HEAD_DIM = 64
N_HEADS = 8
KV_HEADS = 2
GROUP = N_HEADS // KV_HEADS
WINDOW = 128
WIN_CHUNKS = WINDOW // CHUNK
ATTN_WIDTH = N_HEADS * HEAD_DIM
KV_WIDTH = KV_HEADS * HEAD_DIM
CONV_DIM = 256
CONV_W = 3
MEM_HEADS = 4
MEM_WIDTH = MEM_HEADS * HEAD_DIM
N_MEM = 256
MIX_WIDTH = ATTN_WIDTH + CONV_DIM + MEM_WIDTH
SPLIT_IDX = [ATTN_WIDTH, ATTN_WIDTH + KV_WIDTH, ATTN_WIDTH + 2 * KV_WIDTH,
             ATTN_WIDTH + 2 * KV_WIDTH + CONV_DIM, ATTN_WIDTH + 2 * KV_WIDTH + 2 * CONV_DIM,
             ATTN_WIDTH + 2 * KV_WIDTH + 3 * CONV_DIM]
IN_WIDTH = ATTN_WIDTH + 2 * KV_WIDTH + 3 * CONV_DIM + MEM_WIDTH
D_FF = ((8 * D_MODEL // 3 + 255) // 256) * 256
EPS = 1e-6
ATTN_SCALE = HEAD_DIM ** -0.5
NEG = -1e30

kernel_name = "hymba_swa_sink_shortconv_memxattn_stream"


def rms_norm(x, g):
    xf = x.astype(jnp.float32)
    xf = xf * lax.rsqrt(jnp.mean(xf * xf, axis=-1, keepdims=True) + EPS)
    return xf.astype(x.dtype) * g


def sink_attention(q, k, v, sink, valid):
    s = jnp.einsum('bnqkgd,bnskd->bnkgqs', q.astype(jnp.float32), k.astype(jnp.float32)) * ATTN_SCALE
    s = jnp.where(valid[None, :, None, None, None, :], s, NEG)
    sl = sink.astype(jnp.float32).reshape(KV_HEADS, GROUP)[None, None, :, :, None, None]
    m = jnp.maximum(jnp.max(s, axis=-1, keepdims=True), sl)
    p = jnp.exp(s - m)
    p = p / (jnp.sum(p, axis=-1, keepdims=True) + jnp.exp(sl - m))
    return jnp.einsum('bnkgqs,bnskd->bnqkgd', p.astype(v.dtype), v)


def window_attention_prompt(q, k, v, sink):
    B, L = q.shape[0], q.shape[1]
    nc = L // CHUNK
    qb = q.reshape(B, nc, CHUNK, KV_HEADS, GROUP, HEAD_DIM)
    pad = ((0, 0), (WIN_CHUNKS, 0), (0, 0), (0, 0), (0, 0))
    kp = jnp.pad(k.reshape(B, nc, CHUNK, KV_HEADS, HEAD_DIM), pad)
    vp = jnp.pad(v.reshape(B, nc, CHUNK, KV_HEADS, HEAD_DIM), pad)
    kb = jnp.concatenate([kp[:, i:i + nc] for i in range(WIN_CHUNKS + 1)], axis=2)
    vb = jnp.concatenate([vp[:, i:i + nc] for i in range(WIN_CHUNKS + 1)], axis=2)
    key_block = jnp.arange((WIN_CHUNKS + 1) * CHUNK) // CHUNK
    valid = (jnp.arange(nc)[:, None] + key_block[None, :]) >= WIN_CHUNKS
    o = sink_attention(qb, kb, vb, sink, valid)
    return o.reshape(B, L, ATTN_WIDTH)


def window_attention_sample(q, k, v, sink, cache_k, cache_v):
    B, L = q.shape[0], q.shape[1]
    kk = jnp.concatenate([cache_k, k], axis=1)
    vv = jnp.concatenate([cache_v, v], axis=1)
    valid = jnp.ones((1, kk.shape[1]), dtype=bool)
    o = sink_attention(q.reshape(B, 1, L, KV_HEADS, GROUP, HEAD_DIM), kk[:, None], vv[:, None], sink, valid)
    return o.reshape(B, L, ATTN_WIDTH), kk[:, -WINDOW:], vv[:, -WINDOW:]


def causal_conv(u, state, w):
    L = u.shape[1]
    up = jnp.concatenate([state, u], axis=1)
    y = up[:, 0:L] * w[0]
    for i in range(1, CONV_W):
        y = y + up[:, i:i + L] * w[i]
    return y, up[:, -(CONV_W - 1):]


def memory_kv(mem, mem_norm_g, w_mem_kv, mk_norm_g):
    B = mem.shape[0]
    mk, mv = jnp.split(rms_norm(mem, mem_norm_g) @ w_mem_kv, 2, axis=-1)
    mk = rms_norm(mk.reshape(B, N_MEM, MEM_HEADS, HEAD_DIM), mk_norm_g)
    return mk, mv.reshape(B, N_MEM, MEM_HEADS, HEAD_DIM)


def memory_attention(mq, mk, mv):
    s = jnp.einsum('blhd,bmhd->bhlm', mq.astype(jnp.float32), mk.astype(jnp.float32)) * ATTN_SCALE
    p = jax.nn.softmax(s, axis=-1)
    o = jnp.einsum('bhlm,bmhd->blhd', p.astype(mv.dtype), mv)
    return o.reshape(mq.shape[0], mq.shape[1], MEM_WIDTH)


def layer(x, mem_k, mem_v, cache_k, cache_v, conv_state, attn_norm_g, w_in, q_norm_g, k_norm_g,
          sinks, conv_w, mq_norm_g, out_norm_g, w_out, ffn_norm_g, w_gate_up, w_down):
    B, L, _ = x.shape
    u = rms_norm(x, attn_norm_g) @ w_in
    q, k, v, cb, cc, cx, mq = jnp.split(u, SPLIT_IDX, axis=-1)
    q = rms_norm(q.reshape(B, L, N_HEADS, HEAD_DIM), q_norm_g)
    k = rms_norm(k.reshape(B, L, KV_HEADS, HEAD_DIM), k_norm_g)
    v = v.reshape(B, L, KV_HEADS, HEAD_DIM)
    if cache_k is None:
        a = window_attention_prompt(q, k, v, sinks)
        new_k, new_v = k[:, -WINDOW:], v[:, -WINDOW:]
        conv_state = jnp.zeros((B, CONV_W - 1, CONV_DIM), dtype=x.dtype)
    else:
        a, new_k, new_v = window_attention_sample(q, k, v, sinks, cache_k, cache_v)
    cy, new_conv = causal_conv(cc * cx, conv_state, conv_w)
    cy = cb * cy
    mq = rms_norm(mq.reshape(B, L, MEM_HEADS, HEAD_DIM), mq_norm_g)
    mo = memory_attention(mq, mem_k, mem_v)
    mix = jnp.concatenate([
        rms_norm(a, out_norm_g[:ATTN_WIDTH]),
        rms_norm(cy, out_norm_g[ATTN_WIDTH:ATTN_WIDTH + CONV_DIM]),
        rms_norm(mo, out_norm_g[ATTN_WIDTH + CONV_DIM:]),
    ], axis=-1)
    h = x + mix @ w_out
    gate, up = jnp.split(rms_norm(h, ffn_norm_g) @ w_gate_up, 2, axis=-1)
    y = h + (jax.nn.silu(gate) * up) @ w_down
    return y, new_k, new_v, new_conv


def setup_inputs(seed: int = 0) -> dict:
    key = jax.random.key(seed)
    ks = jax.random.split(key, 24)
    f32 = jnp.float32
    nrm = lambda k, shape, scale: jax.random.normal(k, shape, f32) * scale
    gain = lambda k, shape: 1.0 + 0.05 * jax.random.normal(k, shape, f32)
    win_rows = min(WINDOW, PAST_LEN)
    return {
        "x_prompt": nrm(ks[0], (BATCH, SEQ, D_MODEL), 1.0),
        "x_sample": nrm(ks[1], (DEC_BATCH, DEC_SEQ, D_MODEL), 1.0),
        "mem_prompt": nrm(ks[2], (BATCH, N_MEM, D_MODEL), 1.0),
        "cache_win_k": nrm(ks[3], (DEPTH, DEC_BATCH, win_rows, KV_HEADS, HEAD_DIM), 1.0),
        "cache_win_v": nrm(ks[4], (DEPTH, DEC_BATCH, win_rows, KV_HEADS, HEAD_DIM), 1.0),
        "cache_conv": nrm(ks[5], (DEPTH, DEC_BATCH, CONV_W - 1, CONV_DIM), 1.0),
        "cache_mem_k": nrm(ks[6], (DEPTH, DEC_BATCH, N_MEM, MEM_HEADS, HEAD_DIM), 1.0),
        "cache_mem_v": nrm(ks[7], (DEPTH, DEC_BATCH, N_MEM, MEM_HEADS, HEAD_DIM), 1.0),
        "attn_norm_g": gain(ks[8], (DEPTH, D_MODEL)),
        "w_in": nrm(ks[9], (DEPTH, D_MODEL, IN_WIDTH), D_MODEL ** -0.5),
        "q_norm_g": gain(ks[10], (DEPTH, HEAD_DIM)),
        "k_norm_g": gain(ks[11], (DEPTH, HEAD_DIM)),
        "sinks": nrm(ks[12], (DEPTH, N_HEADS), 0.5),
        "conv_w": nrm(ks[13], (DEPTH, CONV_W, CONV_DIM), CONV_W ** -0.5),
        "mem_norm_g": gain(ks[14], (DEPTH, D_MODEL)),
        "w_mem_kv": nrm(ks[15], (DEPTH, D_MODEL, 2 * MEM_WIDTH), D_MODEL ** -0.5),
        "mq_norm_g": gain(ks[16], (DEPTH, HEAD_DIM)),
        "mk_norm_g": gain(ks[17], (DEPTH, HEAD_DIM)),
        "out_norm_g": gain(ks[18], (DEPTH, MIX_WIDTH)),
        "w_out": nrm(ks[19], (DEPTH, MIX_WIDTH, D_MODEL), MIX_WIDTH ** -0.5),
        "ffn_norm_g": gain(ks[20], (DEPTH, D_MODEL)),
        "w_gate_up": nrm(ks[21], (DEPTH, D_MODEL, 2 * D_FF), D_MODEL ** -0.5),
        "w_down": nrm(ks[22], (DEPTH, D_FF, D_MODEL), D_FF ** -0.5),
    }


def reference(x_prompt, x_sample, mem_prompt, cache_win_k, cache_win_v, cache_conv, cache_mem_k,
              cache_mem_v, attn_norm_g, w_in, q_norm_g, k_norm_g, sinks, conv_w, mem_norm_g,
              w_mem_kv, mq_norm_g, mk_norm_g, out_norm_g, w_out, ffn_norm_g, w_gate_up, w_down):
    yp, ys = x_prompt, x_sample
    wk_p, wv_p, cv_p, mk_p_all, mv_p_all = [], [], [], [], []
    wk_s, wv_s, cv_s = [], [], []
    for l in range(DEPTH):
        lw = (attn_norm_g[l], w_in[l], q_norm_g[l], k_norm_g[l], sinks[l], conv_w[l],
              mq_norm_g[l], out_norm_g[l], w_out[l], ffn_norm_g[l], w_gate_up[l], w_down[l])
        mk_p, mv_p = memory_kv(mem_prompt, mem_norm_g[l], w_mem_kv[l], mk_norm_g[l])
        yp, k_p, v_p, c_p = layer(yp, mk_p, mv_p, None, None, None, *lw)
        wk_p.append(k_p); wv_p.append(v_p); cv_p.append(c_p)
        mk_p_all.append(mk_p); mv_p_all.append(mv_p)
        ys, k_s, v_s, c_s = layer(ys, cache_mem_k[l], cache_mem_v[l], cache_win_k[l], cache_win_v[l],
                                  cache_conv[l], *lw)
        wk_s.append(k_s); wv_s.append(v_s); cv_s.append(c_s)
    return (yp, ys,
            jnp.stack(wk_p), jnp.stack(wv_p), jnp.stack(cv_p),
            jnp.stack(mk_p_all), jnp.stack(mv_p_all),
            jnp.stack(wk_s), jnp.stack(wv_s), jnp.stack(cv_s))
```

```python
import functools

import jax
import jax.numpy as jnp
from jax import lax
from jax.experimental import pallas as pl
from jax.experimental.pallas import tpu as pltpu

D_MODEL = 1024
DEPTH = 4
CHUNK = 64
HEAD_DIM = 64
N_HEADS = 8
KV_HEADS = 2
GROUP = N_HEADS // KV_HEADS
WINDOW = 128
ATTN_WIDTH = N_HEADS * HEAD_DIM
KV_WIDTH = KV_HEADS * HEAD_DIM
CONV_DIM = 256
CONV_W = 3
MEM_HEADS = 4
MEM_WIDTH = MEM_HEADS * HEAD_DIM
N_MEM = 256
MIX_WIDTH = ATTN_WIDTH + CONV_DIM + MEM_WIDTH
IN_WIDTH = ATTN_WIDTH + 2 * KV_WIDTH + 3 * CONV_DIM + MEM_WIDTH
D_FF = 2816
EPS = 1e-6
ATTN_SCALE = HEAD_DIM ** -0.5
NEG = -1e30

_Q0, _K0, _V0 = 0, ATTN_WIDTH, ATTN_WIDTH + KV_WIDTH
_CB0 = ATTN_WIDTH + 2 * KV_WIDTH
_CC0 = _CB0 + CONV_DIM
_CX0 = _CC0 + CONV_DIM
_MQ0 = _CX0 + CONV_DIM

PAIR = 2 * CHUNK
TM_PROMPT = 512
G_SAMPLE = 8
TM_MEM = 512
FF_CHUNKS = (1024, 1024, 768)
VMEM_LIMIT_BYTES = 56 * 1024 * 1024

BF = jnp.bfloat16
F32 = jnp.float32


def _dot(a, b):
    return jnp.dot(a, b, preferred_element_type=F32)


def _dot_nt(a, b):
    return lax.dot_general(a, b, (((1,), (1,)), ((), ())), preferred_element_type=F32)


def _rms(x, g):
    ms = jnp.mean(x * x, axis=-1, keepdims=True)
    return x * lax.rsqrt(ms + EPS) * g


def _head_rms(z, g, bd):
    w = z.shape[-1]
    ms = _dot((z * z).astype(BF), bd[:w, :w])
    return z * lax.rsqrt(ms + EPS) * g


def _lane_iota(shape):
    return lax.broadcasted_iota(jnp.int32, shape, len(shape) - 1)


def _split_heads_lo_hi(z):
    lo = (_lane_iota(z.shape) % (2 * HEAD_DIM)) < HEAD_DIM
    return jnp.where(lo, z, 0.0).astype(BF), jnp.where(lo, 0.0, z).astype(BF)


def _dup_halves(z):
    sw = pltpu.roll(z, HEAD_DIM, axis=1)
    lo = _lane_iota(z.shape) < HEAD_DIM
    return jnp.where(lo, z, sw), jnp.where(lo, sw, z)


def _pick_head_lanes(o_all, rows):
    grp = _lane_iota((rows, GROUP * HEAD_DIM)) // HEAD_DIM
    out = o_all[3 * rows:4 * rows]
    for h in (2, 1, 0):
        out = jnp.where(grp == h, o_all[h * rows:(h + 1) * rows], out)
    return out


def _softmax_rows(s, sink):
    m = jnp.max(s, axis=-1, keepdims=True)
    if sink is not None:
        m = jnp.maximum(m, sink)
    p = jnp.exp(s - m)
    den = jnp.sum(p, axis=-1, keepdims=True)
    if sink is not None:
        den = den + jnp.exp(sink - m)
    return p * (1.0 / den)


def _project_in(x, g_attn, w_in_ref, bd, gq, gk, gmq):
    xn = _rms(x, g_attn).astype(BF)
    u = _dot(xn, w_in_ref[...])
    half = ATTN_WIDTH // 2
    qn = jnp.concatenate(
        [_head_rms(u[:, _Q0:_Q0 + half], gq[:, :half], bd),
         _head_rms(u[:, _Q0 + half:_K0], gq[:, half:], bd)], axis=-1) * ATTN_SCALE
    kn = _head_rms(u[:, _K0:_V0], gk, bd)
    v = u[:, _V0:_CB0]
    cb = u[:, _CB0:_CC0]
    ccx = u[:, _CC0:_CX0] * u[:, _CX0:_MQ0]
    mqn = _head_rms(u[:, _MQ0:], gmq, bd) * ATTN_SCALE
    return qn, kn, v, cb, ccx, mqn


def _mem_attention(mq_blocks, mk_b, mv_b, rows):
    s = _dot_nt(mq_blocks, mk_b)
    p = _softmax_rows(s, None).astype(BF)
    return _pick_head_lanes(_dot(p, mv_b), rows)


def _mix_out_ffn(x, a, cy, mo, g_out, w_out_ref, g_ffn, w_gu_ref, w_down_ref):
    mix = jnp.concatenate(
        [_rms(a, g_out[:, :ATTN_WIDTH]),
         _rms(cy, g_out[:, ATTN_WIDTH:ATTN_WIDTH + CONV_DIM]),
         _rms(mo, g_out[:, ATTN_WIDTH + CONV_DIM:])], axis=-1).astype(BF)
    h = x + _dot(mix, w_out_ref[...])
    hn = _rms(h, g_ffn).astype(BF)
    y = h
    c0 = 0
    for width in FF_CHUNKS:
        gate = _dot(hn, w_gu_ref[:, c0:c0 + width])
        up = _dot(hn, w_gu_ref[:, D_FF + c0:D_FF + c0 + width])
        act = (gate * jax.nn.sigmoid(gate) * up).astype(BF)
        y = y + _dot(act, w_down_ref[c0:c0 + width, :])
        c0 += width
    return y


def _conv_from_scratch(conv_scr, ccx, cb, conv_w, rows):
    sh2 = conv_scr[6:6 + rows, :]
    sh1 = conv_scr[7:7 + rows, :]
    return cb * (sh2 * conv_w[0:1, :] + sh1 * conv_w[1:2, :] + ccx * conv_w[2:3, :])


def _prompt_layer_kernel(sinks_ref, x_ref, mk_ref, mv_ref, g_attn_ref, w_in_ref, bd_ref, gq_ref,
                         gk_ref, gmq_ref, conv_w_ref, g_out_ref, w_out_ref, g_ffn_ref, w_gu_ref,
                         w_down_ref, y_ref, newk_ref, newv_ref, newc_ref,
                         kcar_scr, vcar_scr, ccar_scr, conv_scr, a_scr):
    tm = TM_PROMPT
    t = pl.program_id(1)
    last_t = pl.num_programs(1) - 1

    started = t > 0
    k_prev = [jnp.where(started, kcar_scr[j], 0.0).astype(BF) for j in range(KV_HEADS)]
    v_prev = [jnp.where(started, vcar_scr[j], 0.0).astype(BF) for j in range(KV_HEADS)]
    conv_scr[0:8, :] = jnp.where(started, ccar_scr[...], 0.0)

    x = x_ref[0]
    qn, kn, v, cb, ccx, mqn = _project_in(x, g_attn_ref[...], w_in_ref, bd_ref[...], gq_ref[...],
                                          gk_ref[...], gmq_ref[...])

    k_dup = _dup_halves(kn)
    v_dup = _dup_halves(v)
    for j in range(KV_HEADS):
        kcar_scr[j] = k_dup[j][tm - WINDOW:tm, :]
        vcar_scr[j] = v_dup[j][tm - WINDOW:tm, :]
    k_rep = [k_dup[j].astype(BF) for j in range(KV_HEADS)]
    v_rep = [v_dup[j].astype(BF) for j in range(KV_HEADS)]

    q_lo, q_hi = _split_heads_lo_hi(qn)

    row_chunk = lax.broadcasted_iota(jnp.int32, (PAIR, 2 * PAIR), 0) // CHUNK
    col = _lane_iota((PAIR, 2 * PAIR))
    first_key = jnp.where(t > 0, 0, WINDOW)
    visible = (col >= row_chunk * CHUNK) & (col < (row_chunk + 3) * CHUNK)
    bias_rest = jnp.where(visible, 0.0, NEG)
    bias_first = jnp.where(col >= first_key, bias_rest, NEG)

    for p in range(tm // PAIR):
        r0 = p * PAIR
        bias = bias_first if p == 0 else bias_rest
        for j in range(KV_HEADS):
            c0 = j * GROUP * HEAD_DIM
            qf = jnp.concatenate(
                [q_lo[r0:r0 + PAIR, c0:c0 + 128], q_hi[r0:r0 + PAIR, c0:c0 + 128],
                 q_lo[r0:r0 + PAIR, c0 + 128:c0 + 256], q_hi[r0:r0 + PAIR, c0 + 128:c0 + 256]],
                axis=0)
            if p == 0:
                k_win = jnp.concatenate([k_prev[j], k_rep[j][0:PAIR]], axis=0)
                v_win = jnp.concatenate([v_prev[j], v_rep[j][0:PAIR]], axis=0)
            else:
                k_win = k_rep[j][r0 - PAIR:r0 + PAIR]
                v_win = v_rep[j][r0 - PAIR:r0 + PAIR]
            s = _dot_nt(qf, k_win)
            probs = []
            for h in range(GROUP):
                sh = s[h * PAIR:(h + 1) * PAIR] + bias
                probs.append(_softmax_rows(sh, sinks_ref[j * GROUP + h]))
            pm = jnp.concatenate(probs, axis=0).astype(BF)
            o_all = _dot(pm, jnp.concatenate([v_win, v_win], axis=-1))
            a_scr[r0:r0 + PAIR, c0:c0 + GROUP * HEAD_DIM] = _pick_head_lanes(o_all, PAIR)

    conv_scr[8:8 + tm, :] = ccx
    cy = _conv_from_scratch(conv_scr, ccx, cb, conv_w_ref[...], tm)
    ccar_scr[...] = ccx[tm - 8:tm, :]

    grp = _lane_iota(mqn.shape) // HEAD_DIM
    mq_blocks = jnp.concatenate(
        [jnp.where(grp == h, mqn, 0.0).astype(BF) for h in range(MEM_HEADS)], axis=0)
    mo = _mem_attention(mq_blocks, mk_ref[0].astype(BF), mv_ref[0].astype(BF), tm)

    y_ref[0] = _mix_out_ffn(x, a_scr[...], cy, mo, g_out_ref[...], w_out_ref, g_ffn_ref[...],
                            w_gu_ref, w_down_ref)

    @pl.when(t == last_t)
    def _():
        newk_ref[0] = kn[tm - WINDOW:tm, :]
        newv_ref[0] = v[tm - WINDOW:tm, :]
        newc_ref[0] = ccx[tm - (CONV_W - 1):tm, :]


def _const_spec(shape):
    zeros = (0,) * len(shape)
    return pl.BlockSpec(shape, lambda *_: zeros, pipeline_mode=pl.Buffered(1))


def _weight_specs():
    return dict(
        g_attn=_const_spec((1, D_MODEL)),
        w_in=_const_spec((D_MODEL, IN_WIDTH)),
        bd=_const_spec((256, 256)),
        gq=_const_spec((1, ATTN_WIDTH)),
        gk=_const_spec((1, KV_WIDTH)),
        gmq=_const_spec((1, MEM_WIDTH)),
        conv_w=_const_spec((CONV_W, CONV_DIM)),
        g_out=_const_spec((1, MIX_WIDTH)),
        w_out=_const_spec((MIX_WIDTH, D_MODEL)),
        g_ffn=_const_spec((1, D_MODEL)),
        w_gu=_const_spec((D_MODEL, 2 * D_FF)),
        w_down=_const_spec((D_FF, D_MODEL)),
    )


_WEIGHT_ORDER = ("g_attn", "w_in", "bd", "gq", "gk", "gmq", "conv_w", "g_out", "w_out", "g_ffn",
                 "w_gu", "w_down")


def _prompt_layer(x, mem_k, mem_v, sinks, lw):
    batch, seq, _ = x.shape
    tm = TM_PROMPT
    wspecs = _weight_specs()
    in_specs = [
        pl.BlockSpec(memory_space=pltpu.SMEM),
        pl.BlockSpec((1, tm, D_MODEL), lambda b, t: (b, t, 0)),
        pl.BlockSpec((1, N_MEM, MEM_WIDTH), lambda b, t: (b, 0, 0)),
        pl.BlockSpec((1, N_MEM, MEM_WIDTH), lambda b, t: (b, 0, 0)),
    ] + [wspecs[n] for n in _WEIGHT_ORDER]
    out_shape = (
        jax.ShapeDtypeStruct((batch, seq, D_MODEL), F32),
        jax.ShapeDtypeStruct((batch, WINDOW, KV_WIDTH), F32),
        jax.ShapeDtypeStruct((batch, WINDOW, KV_WIDTH), F32),
        jax.ShapeDtypeStruct((batch, CONV_W - 1, CONV_DIM), F32),
    )
    out_specs = (
        pl.BlockSpec((1, tm, D_MODEL), lambda b, t: (b, t, 0)),
        pl.BlockSpec((1, WINDOW, KV_WIDTH), lambda b, t: (b, 0, 0)),
        pl.BlockSpec((1, WINDOW, KV_WIDTH), lambda b, t: (b, 0, 0)),
        pl.BlockSpec((1, CONV_W - 1, CONV_DIM), lambda b, t: (b, 0, 0)),
    )
    scratch = [
        pltpu.VMEM((KV_HEADS, WINDOW, 2 * HEAD_DIM), F32),
        pltpu.VMEM((KV_HEADS, WINDOW, 2 * HEAD_DIM), F32),
        pltpu.VMEM((8, CONV_DIM), F32),
        pltpu.VMEM((8 + tm, CONV_DIM), F32),
        pltpu.VMEM((tm, ATTN_WIDTH), F32),
    ]
    return pl.pallas_call(
        _prompt_layer_kernel,
        grid=(batch, seq // tm),
        in_specs=in_specs,
        out_specs=out_specs,
        out_shape=out_shape,
        scratch_shapes=scratch,
        compiler_params=pltpu.CompilerParams(
            dimension_semantics=("arbitrary", "arbitrary"),
            vmem_limit_bytes=VMEM_LIMIT_BYTES),
        name="prompt_layer",
    )(sinks, x, mem_k, mem_v, *[lw[n] for n in _WEIGHT_ORDER])


def _sample_layer_kernel(sinks_ref, x_ref, ck_ref, cv_ref, cc_ref, mk_ref, mv_ref, g_attn_ref,
                         w_in_ref, bd_ref, gq_ref, gk_ref, gmq_ref, conv_w_ref, g_out_ref,
                         w_out_ref, g_ffn_ref, w_gu_ref, w_down_ref, y_ref, newk_ref, newv_ref,
                         newc_ref, conv_scr, a_scr, cy_scr, mo_scr):
    rows = CHUNK
    x = x_ref[...]
    qn, kn, v, cb, ccx, mqn = _project_in(x, g_attn_ref[...], w_in_ref, bd_ref[...], gq_ref[...],
                                          gk_ref[...], gmq_ref[...])
    q_lo, q_hi = _split_heads_lo_hi(qn)
    grp = _lane_iota(mqn.shape) // HEAD_DIM
    mq_heads = [jnp.where(grp == h, mqn, 0.0).astype(BF) for h in range(MEM_HEADS)]
    conv_w = conv_w_ref[...]

    for b in range(G_SAMPLE):
        r0 = b * rows
        k_all = jnp.concatenate([ck_ref[b], kn[r0:r0 + rows]], axis=0)
        v_all = jnp.concatenate([cv_ref[b], v[r0:r0 + rows]], axis=0)
        newk_ref[b] = k_all[rows:, :]
        newv_ref[b] = v_all[rows:, :]
        k_dup = _dup_halves(k_all)
        v_dup = _dup_halves(v_all)
        for j in range(KV_HEADS):
            c0 = j * GROUP * HEAD_DIM
            qf = jnp.concatenate(
                [q_lo[r0:r0 + rows, c0:c0 + 128], q_hi[r0:r0 + rows, c0:c0 + 128],
                 q_lo[r0:r0 + rows, c0 + 128:c0 + 256], q_hi[r0:r0 + rows, c0 + 128:c0 + 256]],
                axis=0)
            s = _dot_nt(qf, k_dup[j].astype(BF))
            probs = [_softmax_rows(s[h * rows:(h + 1) * rows], sinks_ref[j * GROUP + h])
                     for h in range(GROUP)]
            pm = jnp.concatenate(probs, axis=0).astype(BF)
            vb = v_dup[j].astype(BF)
            o_all = _dot(pm, jnp.concatenate([vb, vb], axis=-1))
            a_scr[r0:r0 + rows, c0:c0 + GROUP * HEAD_DIM] = _pick_head_lanes(o_all, rows)

        ccx_b = ccx[r0:r0 + rows]
        conv_scr[6:8, :] = cc_ref[b]
        conv_scr[8:8 + rows, :] = ccx_b
        cy_scr[r0:r0 + rows, :] = _conv_from_scratch(conv_scr, ccx_b, cb[r0:r0 + rows], conv_w,
                                                     rows)
        newc_ref[b] = ccx_b[rows - (CONV_W - 1):rows, :]

        mq_blocks = jnp.concatenate([mq_heads[h][r0:r0 + rows] for h in range(MEM_HEADS)], axis=0)
        mo_scr[r0:r0 + rows, :] = _mem_attention(mq_blocks, mk_ref[b].astype(BF),
                                                 mv_ref[b].astype(BF), rows)

    y_ref[...] = _mix_out_ffn(x, a_scr[...], cy_scr[...], mo_scr[...], g_out_ref[...], w_out_ref,
                              g_ffn_ref[...], w_gu_ref, w_down_ref)


def _sample_layer(x, cache_k, cache_v, cache_conv, mem_k, mem_v, sinks, lw):
    n_b = cache_k.shape[0]
    g = G_SAMPLE
    tm = g * CHUNK
    wspecs = _weight_specs()
    in_specs = [
        pl.BlockSpec(memory_space=pltpu.SMEM),
        pl.BlockSpec((tm, D_MODEL), lambda i: (i, 0)),
        pl.BlockSpec((g, WINDOW, KV_WIDTH), lambda i: (i, 0, 0)),
        pl.BlockSpec((g, WINDOW, KV_WIDTH), lambda i: (i, 0, 0)),
        pl.BlockSpec((g, CONV_W - 1, CONV_DIM), lambda i: (i, 0, 0)),
        pl.BlockSpec((g, N_MEM, MEM_WIDTH), lambda i: (i, 0, 0)),
        pl.BlockSpec((g, N_MEM, MEM_WIDTH), lambda i: (i, 0, 0)),
    ] + [wspecs[n] for n in _WEIGHT_ORDER]
    out_shape = (
        jax.ShapeDtypeStruct((n_b * CHUNK, D_MODEL), F32),
        jax.ShapeDtypeStruct((n_b, WINDOW, KV_WIDTH), F32),
        jax.ShapeDtypeStruct((n_b, WINDOW, KV_WIDTH), F32),
        jax.ShapeDtypeStruct((n_b, CONV_W - 1, CONV_DIM), F32),
    )
    out_specs = (
        pl.BlockSpec((tm, D_MODEL), lambda i: (i, 0)),
        pl.BlockSpec((g, WINDOW, KV_WIDTH), lambda i: (i, 0, 0)),
        pl.BlockSpec((g, WINDOW, KV_WIDTH), lambda i: (i, 0, 0)),
        pl.BlockSpec((g, CONV_W - 1, CONV_DIM), lambda i: (i, 0, 0)),
    )
    scratch = [
        pltpu.VMEM((8 + CHUNK, CONV_DIM), F32),
        pltpu.VMEM((tm, ATTN_WIDTH), F32),
        pltpu.VMEM((tm, CONV_DIM), F32),
        pltpu.VMEM((tm, MEM_WIDTH), F32),
    ]
    return pl.pallas_call(
        _sample_layer_kernel,
        grid=(n_b // g,),
        in_specs=in_specs,
        out_specs=out_specs,
        out_shape=out_shape,
        scratch_shapes=scratch,
        compiler_params=pltpu.CompilerParams(
            dimension_semantics=("arbitrary",),
            vmem_limit_bytes=VMEM_LIMIT_BYTES),
        name="sample_layer",
    )(sinks, x, cache_k, cache_v, cache_conv, mem_k, mem_v, *[lw[n] for n in _WEIGHT_ORDER])


def _mem_kv_kernel(mem_ref, g_mem_ref, w_ref, bd_ref, gmk_ref, mk_ref, mv_ref):
    xn = _rms(mem_ref[...], g_mem_ref[0]).astype(BF)
    kv = _dot(xn, w_ref[0])
    mk_ref[0] = _head_rms(kv[:, :MEM_WIDTH], gmk_ref[0], bd_ref[...])
    mv_ref[0] = kv[:, MEM_WIDTH:]


def _mem_kv(mem2d, g_mem, w_mem_kv, bd, gmk):
    rows = mem2d.shape[0]
    tm = TM_MEM
    out = jax.ShapeDtypeStruct((DEPTH, rows, MEM_WIDTH), F32)
    return pl.pallas_call(
        _mem_kv_kernel,
        grid=(DEPTH, rows // tm),
        in_specs=[
            pl.BlockSpec((tm, D_MODEL), lambda l, i: (i, 0)),
            pl.BlockSpec((1, 1, D_MODEL), lambda l, i: (l, 0, 0)),
            pl.BlockSpec((1, D_MODEL, 2 * MEM_WIDTH), lambda l, i: (l, 0, 0)),
            pl.BlockSpec((256, 256), lambda l, i: (0, 0)),
            pl.BlockSpec((1, 1, MEM_WIDTH), lambda l, i: (l, 0, 0)),
        ],
        out_specs=(pl.BlockSpec((1, tm, MEM_WIDTH), lambda l, i: (l, i, 0)),
                   pl.BlockSpec((1, tm, MEM_WIDTH), lambda l, i: (l, i, 0))),
        out_shape=(out, out),
        compiler_params=pltpu.CompilerParams(
            dimension_semantics=("arbitrary", "arbitrary"),
            vmem_limit_bytes=VMEM_LIMIT_BYTES),
        name="prompt_mem_kv",
    )(mem2d, g_mem, w_mem_kv, bd, gmk)


def _tile_heads(g, n):
    return jnp.tile(g, (1, n)).reshape(g.shape[0], 1, n * g.shape[1])


def kernel(x_prompt, x_sample, mem_prompt, cache_win_k, cache_win_v, cache_conv, cache_mem_k,
           cache_mem_v, attn_norm_g, w_in, q_norm_g, k_norm_g, sinks, conv_w, mem_norm_g,
           w_mem_kv, mq_norm_g, mk_norm_g, out_norm_g, w_out, ffn_norm_g, w_gate_up, w_down):
    batch, seq, _ = x_prompt.shape
    dec_batch, dec_seq, _ = x_sample.shape
    assert dec_seq == CHUNK and seq % TM_PROMPT == 0 and dec_batch % G_SAMPLE == 0

    head = jnp.arange(256) // HEAD_DIM
    bd = jnp.where(head[:, None] == head[None, :], 1.0 / HEAD_DIM, 0.0).astype(BF)

    w_in_b = w_in.astype(BF)
    w_out_b = w_out.astype(BF)
    w_gu_b = w_gate_up.astype(BF)
    w_down_b = w_down.astype(BF)
    gq = _tile_heads(q_norm_g, N_HEADS)
    gk = _tile_heads(k_norm_g, KV_HEADS)
    gmq = _tile_heads(mq_norm_g, MEM_HEADS)
    gmk = _tile_heads(mk_norm_g, MEM_HEADS)

    mk_all, mv_all = _mem_kv(mem_prompt.reshape(batch * N_MEM, D_MODEL),
                             mem_norm_g.reshape(DEPTH, 1, D_MODEL), w_mem_kv.astype(BF), bd, gmk)
    mk_all = mk_all.reshape(DEPTH, batch, N_MEM, MEM_WIDTH)
    mv_all = mv_all.reshape(DEPTH, batch, N_MEM, MEM_WIDTH)

    yp = x_prompt
    ys = x_sample.reshape(dec_batch * dec_seq, D_MODEL)
    wk_p, wv_p, cv_p, wk_s, wv_s, cv_s = [], [], [], [], [], []
    for l in range(DEPTH):
        lw = dict(
            g_attn=attn_norm_g[l].reshape(1, D_MODEL), w_in=w_in_b[l], bd=bd, gq=gq[l], gk=gk[l],
            gmq=gmq[l], conv_w=conv_w[l], g_out=out_norm_g[l].reshape(1, MIX_WIDTH),
            w_out=w_out_b[l], g_ffn=ffn_norm_g[l].reshape(1, D_MODEL), w_gu=w_gu_b[l],
            w_down=w_down_b[l])
        yp, k_p, v_p, c_p = _prompt_layer(yp, mk_all[l], mv_all[l], sinks[l], lw)
        wk_p.append(k_p); wv_p.append(v_p); cv_p.append(c_p)
        ys, k_s, v_s, c_s = _sample_layer(
            ys,
            cache_win_k[l].reshape(dec_batch, WINDOW, KV_WIDTH),
            cache_win_v[l].reshape(dec_batch, WINDOW, KV_WIDTH),
            cache_conv[l],
            cache_mem_k[l].reshape(dec_batch, N_MEM, MEM_WIDTH),
            cache_mem_v[l].reshape(dec_batch, N_MEM, MEM_WIDTH),
            sinks[l], lw)
        wk_s.append(k_s); wv_s.append(v_s); cv_s.append(c_s)

    kv_p = (DEPTH, batch, WINDOW, KV_HEADS, HEAD_DIM)
    kv_s = (DEPTH, dec_batch, WINDOW, KV_HEADS, HEAD_DIM)
    mem_shape = (DEPTH, batch, N_MEM, MEM_HEADS, HEAD_DIM)
    return (yp, ys.reshape(dec_batch, dec_seq, D_MODEL),
            jnp.stack(wk_p).reshape(kv_p), jnp.stack(wv_p).reshape(kv_p), jnp.stack(cv_p),
            mk_all.reshape(mem_shape), mv_all.reshape(mem_shape),
            jnp.stack(wk_s).reshape(kv_s), jnp.stack(wv_s).reshape(kv_s), jnp.stack(cv_s))
```

```python
import jax
import jax.numpy as jnp
from jax import lax
from jax.experimental import pallas as pl
from jax.experimental.pallas import tpu as pltpu

D_MODEL = 1024
DEPTH = 4
CHUNK = 64
HEAD_DIM = 64
N_HEADS = 8
KV_HEADS = 2
GROUP = N_HEADS // KV_HEADS
WINDOW = 128
ATTN_WIDTH = N_HEADS * HEAD_DIM
KV_WIDTH = KV_HEADS * HEAD_DIM
CONV_DIM = 256
CONV_W = 3
MEM_HEADS = 4
MEM_WIDTH = MEM_HEADS * HEAD_DIM
N_MEM = 256
MIX_WIDTH = ATTN_WIDTH + CONV_DIM + MEM_WIDTH
IN_WIDTH = ATTN_WIDTH + 2 * KV_WIDTH + 3 * CONV_DIM + MEM_WIDTH
D_FF = 2816
EPS = 1e-6
ATTN_SCALE = HEAD_DIM ** -0.5
NEG = -1e30

_Q0, _K0, _V0 = 0, ATTN_WIDTH, ATTN_WIDTH + KV_WIDTH
_CB0 = ATTN_WIDTH + 2 * KV_WIDTH
_CC0 = _CB0 + CONV_DIM
_CX0 = _CC0 + CONV_DIM
_MQ0 = _CX0 + CONV_DIM

PAIR = 2 * CHUNK
TM_PROMPT = 512
G_SAMPLE = 8
FF_CHUNKS = (1024, 1024, 768)
VMEM_LIMIT_BYTES = 56 * 1024 * 1024

BF = jnp.bfloat16
F32 = jnp.float32


def _dot(a, b):
    return jnp.dot(a, b, preferred_element_type=F32)


def _dot_nt(a, b):
    return lax.dot_general(a, b, (((1,), (1,)), ((), ())), preferred_element_type=F32)


def _rms(x, g):
    ms = jnp.mean(x * x, axis=-1, keepdims=True)
    return x * lax.rsqrt(ms + EPS) * g


def _head_rms(z, g, bd):
    w = z.shape[-1]
    ms = _dot((z * z).astype(BF), bd[:w, :w])
    return z * lax.rsqrt(ms + EPS) * g


def _lane_iota(shape):
    return lax.broadcasted_iota(jnp.int32, shape, len(shape) - 1)


def _split_heads_lo_hi(z):
    lo = (_lane_iota(z.shape) % (2 * HEAD_DIM)) < HEAD_DIM
    return jnp.where(lo, z, 0.0).astype(BF), jnp.where(lo, 0.0, z).astype(BF)


def _stack_group_queries(q_lo, q_hi, r0, rows, j):
    c0 = j * GROUP * HEAD_DIM
    return jnp.concatenate(
        [q_lo[r0:r0 + rows, c0:c0 + 128], q_hi[r0:r0 + rows, c0:c0 + 128],
         q_lo[r0:r0 + rows, c0 + 128:c0 + 256], q_hi[r0:r0 + rows, c0 + 128:c0 + 256]], axis=0)


def _dup_halves(z):
    sw = pltpu.roll(z, HEAD_DIM, axis=1)
    lo = _lane_iota(z.shape) < HEAD_DIM
    return jnp.where(lo, z, sw), jnp.where(lo, sw, z)


def _pick_head_lanes(o_all, rows):
    grp = _lane_iota((rows, GROUP * HEAD_DIM)) // HEAD_DIM
    out = o_all[3 * rows:4 * rows]
    for h in (2, 1, 0):
        out = jnp.where(grp == h, o_all[h * rows:(h + 1) * rows], out)
    return out


def _softmax_rows(s, sink):
    m = jnp.max(s, axis=-1, keepdims=True)
    if sink is not None:
        m = jnp.maximum(m, sink)
    p = jnp.exp(s - m)
    den = jnp.sum(p, axis=-1, keepdims=True)
    if sink is not None:
        den = den + jnp.exp(sink - m)
    return p * (1.0 / den)


def _project_in(x, g_attn, w_in, bd, gq, gk, gmq):
    xn = _rms(x, g_attn).astype(BF)
    u = _dot(xn, w_in)
    half = ATTN_WIDTH // 2
    qn = jnp.concatenate(
        [_head_rms(u[:, _Q0:_Q0 + half], gq[:, :half], bd),
         _head_rms(u[:, _Q0 + half:_K0], gq[:, half:], bd)], axis=-1) * ATTN_SCALE
    kn = _head_rms(u[:, _K0:_V0], gk, bd)
    v = u[:, _V0:_CB0]
    cb = u[:, _CB0:_CC0]
    ccx = u[:, _CC0:_CX0] * u[:, _CX0:_MQ0]
    mqn = _head_rms(u[:, _MQ0:], gmq, bd) * ATTN_SCALE
    return qn, kn, v, cb, ccx, mqn


def _mem_attention(mq_blocks, mk_t, mv_t, rows):
    s = _dot(mq_blocks, mk_t)
    p = _softmax_rows(s, None).astype(BF)
    return _pick_head_lanes(_dot_nt(p, mv_t), rows)


def _mix_out_ffn(x, a, cy, mo, g_out, w_out_ref, g_ffn, w_gu_ref, w_down_ref):
    mix = jnp.concatenate(
        [_rms(a, g_out[:, :ATTN_WIDTH]),
         _rms(cy, g_out[:, ATTN_WIDTH:ATTN_WIDTH + CONV_DIM]),
         _rms(mo, g_out[:, ATTN_WIDTH + CONV_DIM:])], axis=-1).astype(BF)
    h = x + _dot(mix, w_out_ref[0])
    hn = _rms(h, g_ffn).astype(BF)
    y = h
    c0 = 0
    for width in FF_CHUNKS:
        gate = _dot(hn, w_gu_ref[0, :, c0:c0 + width])
        up = _dot(hn, w_gu_ref[0, :, D_FF + c0:D_FF + c0 + width])
        act = (gate * jax.nn.sigmoid(gate) * up).astype(BF)
        y = y + _dot(act, w_down_ref[0, c0:c0 + width, :])
        c0 += width
    return y


def _conv_from_scratch(conv_scr, ccx, cb, conv_w, rows):
    sh2 = conv_scr[6:6 + rows, :]
    sh1 = conv_scr[7:7 + rows, :]
    return cb * (sh2 * conv_w[0:1, :] + sh1 * conv_w[1:2, :] + ccx * conv_w[2:3, :])


def _prompt_layer_kernel(layer_ref, sinks_ref, x_ref, mk_ref, mv_ref, g_attn_ref, w_in_ref, bd_ref,
                         gq_ref, gk_ref, gmq_ref, conv_w_ref, g_out_ref, w_out_ref, g_ffn_ref,
                         w_gu_ref, w_down_ref, y_ref, newk_ref, newv_ref, newc_ref,
                         kcar_scr, vcar_scr, ccar_scr, conv_scr, a_scr):
    tm = TM_PROMPT
    layer = layer_ref[0]
    t = pl.program_id(1)
    last_t = pl.num_programs(1) - 1

    started = t > 0
    k_prev = [jnp.where(started, kcar_scr[j], 0.0).astype(BF) for j in range(KV_HEADS)]
    v_prev = [jnp.where(started, vcar_scr[j], 0.0).astype(BF) for j in range(KV_HEADS)]
    conv_scr[0:8, :] = jnp.where(started, ccar_scr[...], 0.0)

    x = x_ref[0]
    qn, kn, v, cb, ccx, mqn = _project_in(x, g_attn_ref[0], w_in_ref[0], bd_ref[...], gq_ref[0],
                                          gk_ref[0], gmq_ref[0])

    k_dup = _dup_halves(kn)
    v_dup = _dup_halves(v)
    for j in range(KV_HEADS):
        kcar_scr[j] = k_dup[j][tm - WINDOW:tm, :]
        vcar_scr[j] = v_dup[j][tm - WINDOW:tm, :]
    k_rep = [k_dup[j].astype(BF) for j in range(KV_HEADS)]
    v_rep = [v_dup[j].astype(BF) for j in range(KV_HEADS)]

    q_lo, q_hi = _split_heads_lo_hi(qn)

    row_chunk = lax.broadcasted_iota(jnp.int32, (PAIR, 2 * PAIR), 0) // CHUNK
    col = _lane_iota((PAIR, 2 * PAIR))
    first_key = jnp.where(started, 0, WINDOW)
    visible = (col >= row_chunk * CHUNK) & (col < (row_chunk + 3) * CHUNK)
    bias_rest = jnp.where(visible, 0.0, NEG)
    bias_first = jnp.where(col >= first_key, bias_rest, NEG)

    for p in range(tm // PAIR):
        r0 = p * PAIR
        bias = bias_first if p == 0 else bias_rest
        for j in range(KV_HEADS):
            c0 = j * GROUP * HEAD_DIM
            qf = _stack_group_queries(q_lo, q_hi, r0, PAIR, j)
            if p == 0:
                k_win = jnp.concatenate([k_prev[j], k_rep[j][0:PAIR]], axis=0)
                v_win = jnp.concatenate([v_prev[j], v_rep[j][0:PAIR]], axis=0)
            else:
                k_win = k_rep[j][r0 - PAIR:r0 + PAIR]
                v_win = v_rep[j][r0 - PAIR:r0 + PAIR]
            s = _dot_nt(qf, k_win)
            probs = []
            for h in range(GROUP):
                sh = s[h * PAIR:(h + 1) * PAIR] + bias
                probs.append(_softmax_rows(sh, sinks_ref[layer, j * GROUP + h]))
            pm = jnp.concatenate(probs, axis=0).astype(BF)
            o_all = _dot(pm, jnp.concatenate([v_win, v_win], axis=-1))
            a_scr[r0:r0 + PAIR, c0:c0 + GROUP * HEAD_DIM] = _pick_head_lanes(o_all, PAIR)

    conv_scr[8:8 + tm, :] = ccx
    cy = _conv_from_scratch(conv_scr, ccx, cb, conv_w_ref[0], tm)
    ccar_scr[...] = ccx[tm - 8:tm, :]

    grp = _lane_iota(mqn.shape) // HEAD_DIM
    mq_blocks = jnp.concatenate(
        [jnp.where(grp == h, mqn, 0.0).astype(BF) for h in range(MEM_HEADS)], axis=0)
    mo = _mem_attention(mq_blocks, mk_ref[0, 0].astype(BF), mv_ref[0, 0].astype(BF), tm)

    y_ref[0] = _mix_out_ffn(x, a_scr[...], cy, mo, g_out_ref[0], w_out_ref, g_ffn_ref[0],
                            w_gu_ref, w_down_ref)

    @pl.when(t == last_t)
    def _():
        newk_ref[0] = kn[tm - WINDOW:tm, :].T
        newv_ref[0] = v[tm - WINDOW:tm, :].T
        newc_ref[0] = ccx[tm - (CONV_W - 1):tm, :]


def _layer_spec(shape):
    zeros = (0,) * len(shape)
    return pl.BlockSpec((1,) + shape, lambda *args: (args[-1][0],) + zeros,
                        pipeline_mode=pl.Buffered(1))


def _weight_specs():
    return [
        _layer_spec((1, D_MODEL)),
        _layer_spec((D_MODEL, IN_WIDTH)),
        pl.BlockSpec((256, 256), lambda *args: (0, 0), pipeline_mode=pl.Buffered(1)),
        _layer_spec((1, ATTN_WIDTH)),
        _layer_spec((1, KV_WIDTH)),
        _layer_spec((1, MEM_WIDTH)),
        _layer_spec((CONV_W, CONV_DIM)),
        _layer_spec((1, MIX_WIDTH)),
        _layer_spec((MIX_WIDTH, D_MODEL)),
        _layer_spec((1, D_MODEL)),
        _layer_spec((D_MODEL, 2 * D_FF)),
        _layer_spec((D_FF, D_MODEL)),
    ]


_WEIGHT_ORDER = ("g_attn", "w_in", "bd", "gq", "gk", "gmq", "conv_w", "g_out", "w_out", "g_ffn",
                 "w_gu", "w_down")

_SMEM_SPEC = pl.BlockSpec(memory_space=pltpu.SMEM)


def _prompt_layer(layer, x, mem_k_t, mem_v_t, sinks, weights):
    batch, seq, _ = x.shape
    tm = TM_PROMPT
    in_specs = [
        _SMEM_SPEC,
        pl.BlockSpec((1, tm, D_MODEL), lambda b, t, l: (b, t, 0)),
        pl.BlockSpec((1, 1, MEM_WIDTH, N_MEM), lambda b, t, l: (l[0], b, 0, 0)),
        pl.BlockSpec((1, 1, MEM_WIDTH, N_MEM), lambda b, t, l: (l[0], b, 0, 0)),
    ] + _weight_specs()
    out_shape = (
        jax.ShapeDtypeStruct((batch, seq, D_MODEL), F32),
        jax.ShapeDtypeStruct((batch, KV_WIDTH, WINDOW), F32),
        jax.ShapeDtypeStruct((batch, KV_WIDTH, WINDOW), F32),
        jax.ShapeDtypeStruct((batch, CONV_W - 1, CONV_DIM), F32),
    )
    out_specs = (
        pl.BlockSpec((1, tm, D_MODEL), lambda b, t, l: (b, t, 0)),
        pl.BlockSpec((1, KV_WIDTH, WINDOW), lambda b, t, l: (b, 0, 0)),
        pl.BlockSpec((1, KV_WIDTH, WINDOW), lambda b, t, l: (b, 0, 0)),
        pl.BlockSpec((1, CONV_W - 1, CONV_DIM), lambda b, t, l: (b, 0, 0)),
    )
    scratch = [
        pltpu.VMEM((KV_HEADS, WINDOW, 2 * HEAD_DIM), F32),
        pltpu.VMEM((KV_HEADS, WINDOW, 2 * HEAD_DIM), F32),
        pltpu.VMEM((8, CONV_DIM), F32),
        pltpu.VMEM((8 + tm, CONV_DIM), F32),
        pltpu.VMEM((tm, ATTN_WIDTH), F32),
    ]
    return pl.pallas_call(
        _prompt_layer_kernel,
        grid_spec=pltpu.PrefetchScalarGridSpec(
            num_scalar_prefetch=1, grid=(batch, seq // tm), in_specs=in_specs,
            out_specs=out_specs, scratch_shapes=scratch),
        out_shape=out_shape,
        compiler_params=pltpu.CompilerParams(
            dimension_semantics=("arbitrary", "arbitrary"),
            vmem_limit_bytes=VMEM_LIMIT_BYTES),
        name="prompt_layer",
    )(layer, sinks, x, mem_k_t, mem_v_t, *[weights[n] for n in _WEIGHT_ORDER])


def _sample_layer_kernel(layer_ref, sinks_ref, x_ref, ck_ref, cv_ref, cc_ref, mk_ref, mv_ref,
                         g_attn_ref, w_in_ref, bd_ref, gq_ref, gk_ref, gmq_ref, conv_w_ref,
                         g_out_ref, w_out_ref, g_ffn_ref, w_gu_ref, w_down_ref, y_ref, newk_ref,
                         newv_ref, newc_ref, conv_scr, a_scr, cy_scr, mo_scr):
    rows = CHUNK
    layer = layer_ref[0]
    x = x_ref[...]
    qn, kn, v, cb, ccx, mqn = _project_in(x, g_attn_ref[0], w_in_ref[0], bd_ref[...], gq_ref[0],
                                          gk_ref[0], gmq_ref[0])
    q_lo, q_hi = _split_heads_lo_hi(qn)
    grp = _lane_iota(mqn.shape) // HEAD_DIM
    mq_heads = [jnp.where(grp == h, mqn, 0.0).astype(BF) for h in range(MEM_HEADS)]
    conv_w = conv_w_ref[0]
    old_lane = _lane_iota((KV_WIDTH, WINDOW)) < WINDOW - rows

    def shifted_window(cache_t, new_rows):
        new_t = jnp.concatenate([new_rows, new_rows], axis=0).T
        return jnp.where(old_lane, pltpu.roll(cache_t, WINDOW - rows, axis=1), new_t)

    for b in range(G_SAMPLE):
        r0 = b * rows
        ck_t = ck_ref[0, b]
        cv_t = cv_ref[0, b]
        kn_b = kn[r0:r0 + rows]
        v_b = v[r0:r0 + rows]
        newk_ref[b] = shifted_window(ck_t, kn_b)
        newv_ref[b] = shifted_window(cv_t, v_b)
        k_dup = _dup_halves(kn_b)
        v_dup = _dup_halves(v_b)
        ck_bf = ck_t.astype(BF)
        cv_bf = cv_t.astype(BF)
        for j in range(KV_HEADS):
            c0 = j * GROUP * HEAD_DIM
            qf = _stack_group_queries(q_lo, q_hi, r0, rows, j)
            kj_t = ck_bf[j * HEAD_DIM:(j + 1) * HEAD_DIM, :]
            vj_t = cv_bf[j * HEAD_DIM:(j + 1) * HEAD_DIM, :]
            s = jnp.concatenate(
                [_dot(qf, jnp.concatenate([kj_t, kj_t], axis=0)),
                 _dot_nt(qf, k_dup[j].astype(BF))], axis=-1)
            probs = [_softmax_rows(s[h * rows:(h + 1) * rows], sinks_ref[layer, j * GROUP + h])
                     for h in range(GROUP)]
            pm = jnp.concatenate(probs, axis=0).astype(BF)
            vb = v_dup[j].astype(BF)
            o_all = (_dot_nt(pm[:, :WINDOW], jnp.concatenate([vj_t] * GROUP, axis=0))
                     + _dot(pm[:, WINDOW:], jnp.concatenate([vb, vb], axis=-1)))
            a_scr[r0:r0 + rows, c0:c0 + GROUP * HEAD_DIM] = _pick_head_lanes(o_all, rows)

        ccx_b = ccx[r0:r0 + rows]
        conv_scr[6:8, :] = cc_ref[0, b]
        conv_scr[8:8 + rows, :] = ccx_b
        cy_scr[r0:r0 + rows, :] = _conv_from_scratch(conv_scr, ccx_b, cb[r0:r0 + rows], conv_w,
                                                     rows)
        newc_ref[b] = ccx_b[rows - (CONV_W - 1):rows, :]

        mq_blocks = jnp.concatenate([mq_heads[h][r0:r0 + rows] for h in range(MEM_HEADS)], axis=0)
        mo_scr[r0:r0 + rows, :] = _mem_attention(mq_blocks, mk_ref[0, b].astype(BF),
                                                 mv_ref[0, b].astype(BF), rows)

    y_ref[...] = _mix_out_ffn(x, a_scr[...], cy_scr[...], mo_scr[...], g_out_ref[0], w_out_ref,
                              g_ffn_ref[0], w_gu_ref, w_down_ref)


def _sample_layer(layer, x, cache_k_t, cache_v_t, cache_conv, mem_k_t, mem_v_t, sinks, weights):
    n_b = cache_k_t.shape[1]
    g = G_SAMPLE
    tm = g * CHUNK
    in_specs = [
        _SMEM_SPEC,
        pl.BlockSpec((tm, D_MODEL), lambda i, l: (i, 0)),
        pl.BlockSpec((1, g, KV_WIDTH, WINDOW), lambda i, l: (l[0], i, 0, 0)),
        pl.BlockSpec((1, g, KV_WIDTH, WINDOW), lambda i, l: (l[0], i, 0, 0)),
        pl.BlockSpec((1, g, CONV_W - 1, CONV_DIM), lambda i, l: (l[0], i, 0, 0)),
        pl.BlockSpec((1, g, MEM_WIDTH, N_MEM), lambda i, l: (l[0], i, 0, 0)),
        pl.BlockSpec((1, g, MEM_WIDTH, N_MEM), lambda i, l: (l[0], i, 0, 0)),
    ] + _weight_specs()
    out_shape = (
        jax.ShapeDtypeStruct((n_b * CHUNK, D_MODEL), F32),
        jax.ShapeDtypeStruct((n_b, KV_WIDTH, WINDOW), F32),
        jax.ShapeDtypeStruct((n_b, KV_WIDTH, WINDOW), F32),
        jax.ShapeDtypeStruct((n_b, CONV_W - 1, CONV_DIM), F32),
    )
    out_specs = (
        pl.BlockSpec((tm, D_MODEL), lambda i, l: (i, 0)),
        pl.BlockSpec((g, KV_WIDTH, WINDOW), lambda i, l: (i, 0, 0)),
        pl.BlockSpec((g, KV_WIDTH, WINDOW), lambda i, l: (i, 0, 0)),
        pl.BlockSpec((g, CONV_W - 1, CONV_DIM), lambda i, l: (i, 0, 0)),
    )
    scratch = [
        pltpu.VMEM((8 + CHUNK, CONV_DIM), F32),
        pltpu.VMEM((tm, ATTN_WIDTH), F32),
        pltpu.VMEM((tm, CONV_DIM), F32),
        pltpu.VMEM((tm, MEM_WIDTH), F32),
    ]
    return pl.pallas_call(
        _sample_layer_kernel,
        grid_spec=pltpu.PrefetchScalarGridSpec(
            num_scalar_prefetch=1, grid=(n_b // g,), in_specs=in_specs, out_specs=out_specs,
            scratch_shapes=scratch),
        out_shape=out_shape,
        compiler_params=pltpu.CompilerParams(
            dimension_semantics=("arbitrary",),
            vmem_limit_bytes=VMEM_LIMIT_BYTES),
        name="sample_layer",
    )(layer, sinks, x, cache_k_t, cache_v_t, cache_conv, mem_k_t, mem_v_t,
      *[weights[n] for n in _WEIGHT_ORDER])


def _mem_kv_kernel(mem_ref, g_mem_ref, w_ref, bd_ref, gmk_ref, mk_ref, mv_ref):
    xn = _rms(mem_ref[0], g_mem_ref[0]).astype(BF)
    kv = _dot(xn, w_ref[0])
    mk_ref[0, 0] = _head_rms(kv[:, :MEM_WIDTH], gmk_ref[0], bd_ref[...]).T
    mv_ref[0, 0] = kv[:, MEM_WIDTH:].T


def _mem_kv(mem, g_mem, w_mem_kv, bd, gmk):
    batch = mem.shape[0]
    out = jax.ShapeDtypeStruct((DEPTH, batch, MEM_WIDTH, N_MEM), F32)
    return pl.pallas_call(
        _mem_kv_kernel,
        grid=(DEPTH, batch),
        in_specs=[
            pl.BlockSpec((1, N_MEM, D_MODEL), lambda l, i: (i, 0, 0)),
            pl.BlockSpec((1, 1, D_MODEL), lambda l, i: (l, 0, 0)),
            pl.BlockSpec((1, D_MODEL, 2 * MEM_WIDTH), lambda l, i: (l, 0, 0)),
            pl.BlockSpec((256, 256), lambda l, i: (0, 0)),
            pl.BlockSpec((1, 1, MEM_WIDTH), lambda l, i: (l, 0, 0)),
        ],
        out_specs=(pl.BlockSpec((1, 1, MEM_WIDTH, N_MEM), lambda l, i: (l, i, 0, 0)),
                   pl.BlockSpec((1, 1, MEM_WIDTH, N_MEM), lambda l, i: (l, i, 0, 0))),
        out_shape=(out, out),
        compiler_params=pltpu.CompilerParams(
            dimension_semantics=("arbitrary", "arbitrary"),
            vmem_limit_bytes=VMEM_LIMIT_BYTES),
        name="prompt_mem_kv",
    )(mem, g_mem, w_mem_kv, bd, gmk)


def _tile_heads(g, n):
    return jnp.tile(g, (1, n)).reshape(g.shape[0], 1, n * g.shape[1])


def _feature_major(c):
    lead = c.shape[:-3]
    n_tok, heads, dim = c.shape[-3:]
    perm = tuple(range(len(lead))) + (len(lead) + 1, len(lead) + 2, len(lead))
    return jnp.transpose(c, perm).reshape(lead + (heads * dim, n_tok))


def _token_major(c_t, heads):
    lead = c_t.shape[:-2]
    n_tok = c_t.shape[-1]
    c = c_t.reshape(lead + (heads, HEAD_DIM, n_tok))
    perm = tuple(range(len(lead))) + (len(lead) + 2, len(lead), len(lead) + 1)
    return jnp.transpose(c, perm)


def kernel(x_prompt, x_sample, mem_prompt, cache_win_k, cache_win_v, cache_conv, cache_mem_k,
           cache_mem_v, attn_norm_g, w_in, q_norm_g, k_norm_g, sinks, conv_w, mem_norm_g,
           w_mem_kv, mq_norm_g, mk_norm_g, out_norm_g, w_out, ffn_norm_g, w_gate_up, w_down):
    batch, seq, _ = x_prompt.shape
    dec_batch, dec_seq, _ = x_sample.shape
    assert dec_seq == CHUNK and seq % TM_PROMPT == 0 and dec_batch % G_SAMPLE == 0

    head = jnp.arange(256) // HEAD_DIM
    bd = jnp.where(head[:, None] == head[None, :], 1.0 / HEAD_DIM, 0.0).astype(BF)

    weights = dict(
        g_attn=attn_norm_g.reshape(DEPTH, 1, D_MODEL), w_in=w_in.astype(BF), bd=bd,
        gq=_tile_heads(q_norm_g, N_HEADS), gk=_tile_heads(k_norm_g, KV_HEADS),
        gmq=_tile_heads(mq_norm_g, MEM_HEADS), conv_w=conv_w,
        g_out=out_norm_g.reshape(DEPTH, 1, MIX_WIDTH), w_out=w_out.astype(BF),
        g_ffn=ffn_norm_g.reshape(DEPTH, 1, D_MODEL), w_gu=w_gate_up.astype(BF),
        w_down=w_down.astype(BF))

    mk_t, mv_t = _mem_kv(mem_prompt, mem_norm_g.reshape(DEPTH, 1, D_MODEL), w_mem_kv.astype(BF),
                         bd, _tile_heads(mk_norm_g, MEM_HEADS))

    ck_t = _feature_major(cache_win_k)
    cv_t = _feature_major(cache_win_v)
    cmk_t = _feature_major(cache_mem_k)
    cmv_t = _feature_major(cache_mem_v)

    yp = x_prompt
    ys = x_sample.reshape(dec_batch * dec_seq, D_MODEL)
    wk_p, wv_p, cv_p, wk_s, wv_s, cv_s = [], [], [], [], [], []
    for l in range(DEPTH):
        layer = jnp.full((1,), l, jnp.int32)
        yp, k_p, v_p, c_p = _prompt_layer(layer, yp, mk_t, mv_t, sinks, weights)
        wk_p.append(k_p); wv_p.append(v_p); cv_p.append(c_p)
        ys, k_s, v_s, c_s = _sample_layer(layer, ys, ck_t, cv_t, cache_conv, cmk_t, cmv_t, sinks,
                                          weights)
        wk_s.append(k_s); wv_s.append(v_s); cv_s.append(c_s)

    return (yp, ys.reshape(dec_batch, dec_seq, D_MODEL),
            _token_major(jnp.stack(wk_p), KV_HEADS), _token_major(jnp.stack(wv_p), KV_HEADS),
            jnp.stack(cv_p),
            _token_major(mk_t, MEM_HEADS), _token_major(mv_t, MEM_HEADS),
            _token_major(jnp.stack(wk_s), KV_HEADS), _token_major(jnp.stack(wv_s), KV_HEADS),
            jnp.stack(cv_s))
```

```python
import functools

import jax
import jax.numpy as jnp
from jax import lax
from jax.experimental import pallas as pl
from jax.experimental.pallas import tpu as pltpu

D_MODEL = 1024
DEPTH = 4
CHUNK = 64
HEAD_DIM = 64
N_HEADS = 8
KV_HEADS = 2
GROUP = N_HEADS // KV_HEADS
WINDOW = 128
ATTN_WIDTH = N_HEADS * HEAD_DIM
KV_WIDTH = KV_HEADS * HEAD_DIM
CONV_DIM = 256
CONV_W = 3
MEM_HEADS = 4
MEM_WIDTH = MEM_HEADS * HEAD_DIM
N_MEM = 256
MIX_WIDTH = ATTN_WIDTH + CONV_DIM + MEM_WIDTH
IN_WIDTH = ATTN_WIDTH + 2 * KV_WIDTH + 3 * CONV_DIM + MEM_WIDTH
D_FF = 2816
EPS = 1e-6
ATTN_SCALE = HEAD_DIM ** -0.5
NEG = -1e30

_Q0, _K0, _V0 = 0, ATTN_WIDTH, ATTN_WIDTH + KV_WIDTH
_CB0 = ATTN_WIDTH + 2 * KV_WIDTH
_CC0 = _CB0 + CONV_DIM
_CX0 = _CC0 + CONV_DIM
_MQ0 = _CX0 + CONV_DIM

PAIR = 2 * CHUNK
TM_PROMPT = 512
G_SAMPLE = 8
FF_CHUNKS = (1024, 1024, 768)
FF_BLOCK = 256
VMEM_LIMIT_BYTES = 56 * 1024 * 1024

BF = jnp.bfloat16
F32 = jnp.float32


def _dot(a, b):
    return jnp.dot(a, b, preferred_element_type=F32)


def _dot_nt(a, b):
    return lax.dot_general(a, b, (((1,), (1,)), ((), ())), preferred_element_type=F32)


def _rms(x, g):
    ms = jnp.mean(x * x, axis=-1, keepdims=True)
    return x * lax.rsqrt(ms + EPS) * g


def _head_rms(z, g, bd):
    w = z.shape[-1]
    ms = _dot((z * z).astype(BF), bd[:w, :w])
    return z * lax.rsqrt(ms + EPS) * g


def _lane_iota(shape):
    return lax.broadcasted_iota(jnp.int32, shape, len(shape) - 1)


def _split_heads_lo_hi(z):
    lo = (_lane_iota(z.shape) % (2 * HEAD_DIM)) < HEAD_DIM
    return jnp.where(lo, z, 0.0).astype(BF), jnp.where(lo, 0.0, z).astype(BF)


def _stack_group_queries(q_lo, q_hi, r0, rows, j):
    c0 = j * GROUP * HEAD_DIM
    return jnp.concatenate(
        [q_lo[r0:r0 + rows, c0:c0 + 128], q_hi[r0:r0 + rows, c0:c0 + 128],
         q_lo[r0:r0 + rows, c0 + 128:c0 + 256], q_hi[r0:r0 + rows, c0 + 128:c0 + 256]], axis=0)


def _dup_halves(z):
    sw = pltpu.roll(z, HEAD_DIM, axis=1)
    lo = _lane_iota(z.shape) < HEAD_DIM
    return jnp.where(lo, z, sw), jnp.where(lo, sw, z)


def _pick_head_lanes(o_all, rows):
    grp = _lane_iota((rows, GROUP * HEAD_DIM)) // HEAD_DIM
    out = o_all[3 * rows:4 * rows]
    for h in (2, 1, 0):
        out = jnp.where(grp == h, o_all[h * rows:(h + 1) * rows], out)
    return out


def _softmax_rows(s, sink):
    m = jnp.max(s, axis=-1, keepdims=True)
    if sink is not None:
        m = jnp.maximum(m, sink)
    p = jnp.exp(s - m)
    den = jnp.sum(p, axis=-1, keepdims=True)
    if sink is not None:
        den = den + jnp.exp(sink - m)
    return p * (1.0 / den)


def _project_in(x, g_attn, w_in, bd, gq, gk, gmq):
    xn = _rms(x, g_attn).astype(BF)
    u = _dot(xn, w_in)
    half = ATTN_WIDTH // 2
    qn = jnp.concatenate(
        [_head_rms(u[:, _Q0:_Q0 + half], gq[:, :half], bd),
         _head_rms(u[:, _Q0 + half:_K0], gq[:, half:], bd)], axis=-1) * ATTN_SCALE
    kn = _head_rms(u[:, _K0:_V0], gk, bd)
    v = u[:, _V0:_CB0]
    cb = u[:, _CB0:_CC0]
    ccx = u[:, _CC0:_CX0] * u[:, _CX0:_MQ0]
    mqn = _head_rms(u[:, _MQ0:], gmq, bd) * ATTN_SCALE
    return qn, kn, v, cb, ccx, mqn


def _mem_attention(mq_blocks, mk_t, mv_t, rows):
    s = _dot(mq_blocks, mk_t)
    p = _softmax_rows(s, None).astype(BF)
    return _pick_head_lanes(_dot_nt(p, mv_t), rows)


def _mix_norm(a, cy, mo, g_out):
    return jnp.concatenate(
        [_rms(a, g_out[:, :ATTN_WIDTH]),
         _rms(cy, g_out[:, ATTN_WIDTH:ATTN_WIDTH + CONV_DIM]),
         _rms(mo, g_out[:, ATTN_WIDTH + CONV_DIM:])], axis=-1).astype(BF)


def _ffn_act(hn, w_gu_ref, c0, width):
    gate = _dot(hn, w_gu_ref[0, :, c0:c0 + width])
    up = _dot(hn, w_gu_ref[0, :, D_FF + c0:D_FF + c0 + width])
    return (gate * jax.nn.sigmoid(gate) * up).astype(BF)


def _ffn_chunk(hn, w_gu_ref, w_down_ref, c0, width):
    return _dot(_ffn_act(hn, w_gu_ref, c0, width), w_down_ref[0, c0:c0 + width, :])


def _out_ffn(x, mix, w_out_ref, g_ffn, w_gu_ref, w_down_ref):
    h = x + _dot(mix, w_out_ref[0])
    hn = _rms(h, g_ffn).astype(BF)
    y = h
    c0 = 0
    for width in FF_CHUNKS:
        y = y + _ffn_chunk(hn, w_gu_ref, w_down_ref, c0, width)
        c0 += width
    return y


def _conv_from_scratch(conv_scr, ccx, cb, conv_w, rows):
    sh2 = conv_scr[6:6 + rows, :]
    sh1 = conv_scr[7:7 + rows, :]
    return cb * (sh2 * conv_w[0:1, :] + sh1 * conv_w[1:2, :] + ccx * conv_w[2:3, :])


def _prompt_layer_kernel(layer_ref, sinks_ref, x_ref, mk_ref, mv_ref, g_attn_ref, w_in_ref, bd_ref,
                         gq_ref, gk_ref, gmq_ref, conv_w_ref, g_out_ref, w_out_ref, g_ffn_ref,
                         w_gu_ref, w_down_ref, y_ref, newk_ref, newv_ref, newc_ref,
                         kcar_scr, vcar_scr, ccar_scr, conv_scr, a_scr, mo_scr, mix_scr,
                         xres_scr, *,
                         tiles_per_seq):
    tm = TM_PROMPT
    layer = layer_ref[0]
    s = pl.program_id(0)
    n_tiles = pl.num_programs(0) - 1
    t = jnp.minimum(s, n_tiles - 1) % tiles_per_seq

    @pl.when(s == 0)
    def _():
        mix_scr[...] = jnp.zeros(mix_scr.shape, BF)
        xres_scr[...] = jnp.zeros(xres_scr.shape, F32)

    h_prev = xres_scr[...] + _dot(mix_scr[...], w_out_ref[0])
    hn_prev = _rms(h_prev, g_ffn_ref[0]).astype(BF)

    def back_units():
        blk = FF_BLOCK
        n_blk = D_FF // blk
        acts = []
        y = h_prev

        def down(c):
            lo, hi = 2 * c, min(2 * c + 2, n_blk)
            act = acts[lo] if hi - lo == 1 else jnp.concatenate(acts[lo:hi], axis=-1)
            return _dot(act, w_down_ref[0, lo * blk:hi * blk, :])

        for b in range(n_blk):
            gate = _dot(hn_prev, w_gu_ref[0, :, b * blk:(b + 1) * blk])
            yield
            up = _dot(hn_prev, w_gu_ref[0, :, D_FF + b * blk:D_FF + (b + 1) * blk])
            acts.append((gate * jax.nn.sigmoid(gate) * up).astype(BF))
            yield
            if b >= 2 and b % 2 == 0:
                y = y + down(b // 2 - 1)
                yield
        for c in range((n_blk - 1) // 2, (n_blk + 1) // 2):
            y = y + down(c)
            yield
        y_ref[0] = y

    back = back_units()

    started = t > 0
    k_prev = [jnp.where(started, kcar_scr[j], 0.0).astype(BF) for j in range(KV_HEADS)]
    v_prev = [jnp.where(started, vcar_scr[j], 0.0).astype(BF) for j in range(KV_HEADS)]
    conv_scr[0:8, :] = jnp.where(started, ccar_scr[...], 0.0)

    x = x_ref[0]
    qn, kn, v, cb, ccx, mqn = _project_in(x, g_attn_ref[0], w_in_ref[0], bd_ref[...], gq_ref[0],
                                          gk_ref[0], gmq_ref[0])

    k_dup = _dup_halves(kn)
    v_dup = _dup_halves(v)
    for j in range(KV_HEADS):
        kcar_scr[j] = k_dup[j][tm - WINDOW:tm, :]
        vcar_scr[j] = v_dup[j][tm - WINDOW:tm, :]
    k_rep = [k_dup[j].astype(BF) for j in range(KV_HEADS)]
    v_rep = [v_dup[j].astype(BF) for j in range(KV_HEADS)]

    q_lo, q_hi = _split_heads_lo_hi(qn)

    row_chunk = lax.broadcasted_iota(jnp.int32, (PAIR, 2 * PAIR), 0) // CHUNK
    col = _lane_iota((PAIR, 2 * PAIR))
    first_key = jnp.where(started, 0, WINDOW)
    visible = (col >= row_chunk * CHUNK) & (col < (row_chunk + 3) * CHUNK)
    bias_rest = jnp.where(visible, 0.0, NEG)
    bias_first = jnp.where(col >= first_key, bias_rest, NEG)

    grp = _lane_iota(mqn.shape) // HEAD_DIM
    mk_t = mk_ref[0, 0].astype(BF)
    mv_t = mv_ref[0, 0].astype(BF)
    g_out = g_out_ref[0]
    a_end, c_end = ATTN_WIDTH, ATTN_WIDTH + CONV_DIM

    def front_units():
        for p in range(tm // PAIR):
            r0 = p * PAIR
            bias = bias_first if p == 0 else bias_rest
            for j in range(KV_HEADS):
                c0 = j * GROUP * HEAD_DIM
                qf = _stack_group_queries(q_lo, q_hi, r0, PAIR, j)
                if p == 0:
                    k_win = jnp.concatenate([k_prev[j], k_rep[j][0:PAIR]], axis=0)
                    v_win = jnp.concatenate([v_prev[j], v_rep[j][0:PAIR]], axis=0)
                else:
                    k_win = k_rep[j][r0 - PAIR:r0 + PAIR]
                    v_win = v_rep[j][r0 - PAIR:r0 + PAIR]
                s = _dot_nt(qf, k_win)
                yield
                probs = []
                for h in range(GROUP):
                    sh = s[h * PAIR:(h + 1) * PAIR] + bias
                    probs.append(_softmax_rows(sh, sinks_ref[layer, j * GROUP + h]).astype(BF))
                    if h % 2 == 1:
                        yield
                o_all = _dot(jnp.concatenate(probs, axis=0),
                             jnp.concatenate([v_win, v_win], axis=-1))
                a_scr[r0:r0 + PAIR, c0:c0 + GROUP * HEAD_DIM] = _pick_head_lanes(o_all, PAIR)
                yield
        mix_scr[:, :a_end] = _rms(a_scr[...], g_out[:, :a_end]).astype(BF)
        yield
        for h in range(MEM_HEADS):
            mq_h = jnp.where(grp == h, mqn, 0.0).astype(BF)
            s = _dot(mq_h, mk_t)
            yield
            pm = _softmax_rows(s, None).astype(BF)
            yield
            o_h = _dot_nt(pm, mv_t)
            mo_scr[:, h * HEAD_DIM:(h + 1) * HEAD_DIM] = o_h[:, h * HEAD_DIM:(h + 1) * HEAD_DIM]
            yield

    xres_scr[...] = x

    conv_scr[8:8 + tm, :] = ccx
    cy = _conv_from_scratch(conv_scr, ccx, cb, conv_w_ref[0], tm)
    ccar_scr[...] = ccx[tm - 8:tm, :]
    mix_scr[:, a_end:c_end] = _rms(cy, g_out[:, a_end:c_end]).astype(BF)

    n_back = 2 * (D_FF // FF_BLOCK) + (D_FF // FF_BLOCK + 1) // 2
    n_front = (tm // PAIR) * KV_HEADS * 4 + 1 + MEM_HEADS * 3
    front = front_units()
    done_front = 0
    for i in range(n_back):
        next(back)
        want = ((i + 1) * n_front) // n_back
        for _ in range(want - done_front):
            next(front)
        done_front = want
    assert next(back, "end") == "end" and next(front, "end") == "end"
    mix_scr[:, c_end:] = _rms(mo_scr[...], g_out[:, c_end:]).astype(BF)

    @pl.when(t == tiles_per_seq - 1)
    def _():
        newk_ref[0] = kn[tm - WINDOW:tm, :].T
        newv_ref[0] = v[tm - WINDOW:tm, :].T
        newc_ref[0] = ccx[tm - (CONV_W - 1):tm, :]


def _layer_spec(shape):
    zeros = (0,) * len(shape)
    return pl.BlockSpec((1,) + shape, lambda *args: (args[-1][0],) + zeros,
                        pipeline_mode=pl.Buffered(1))


def _weight_specs():
    return [
        _layer_spec((1, D_MODEL)),
        _layer_spec((D_MODEL, IN_WIDTH)),
        pl.BlockSpec((256, 256), lambda *args: (0, 0), pipeline_mode=pl.Buffered(1)),
        _layer_spec((1, ATTN_WIDTH)),
        _layer_spec((1, KV_WIDTH)),
        _layer_spec((1, MEM_WIDTH)),
        _layer_spec((CONV_W, CONV_DIM)),
        _layer_spec((1, MIX_WIDTH)),
        _layer_spec((MIX_WIDTH, D_MODEL)),
        _layer_spec((1, D_MODEL)),
        _layer_spec((D_MODEL, 2 * D_FF)),
        _layer_spec((D_FF, D_MODEL)),
    ]


_WEIGHT_ORDER = ("g_attn", "w_in", "bd", "gq", "gk", "gmq", "conv_w", "g_out", "w_out", "g_ffn",
                 "w_gu", "w_down")

_SMEM_SPEC = pl.BlockSpec(memory_space=pltpu.SMEM)


def _prompt_layer(layer, x, mem_k_t, mem_v_t, sinks, weights):
    batch, seq, _ = x.shape
    tm = TM_PROMPT
    tps = seq // tm
    n_tiles = batch * tps

    def front(s):
        return jnp.minimum(s, n_tiles - 1)

    def back(s):
        return jnp.maximum(s - 1, 0)

    in_specs = [
        _SMEM_SPEC,
        pl.BlockSpec((1, tm, D_MODEL), lambda s, l: (front(s) // tps, front(s) % tps, 0)),
        pl.BlockSpec((1, 1, MEM_WIDTH, N_MEM), lambda s, l: (l[0], front(s) // tps, 0, 0)),
        pl.BlockSpec((1, 1, MEM_WIDTH, N_MEM), lambda s, l: (l[0], front(s) // tps, 0, 0)),
    ] + _weight_specs()
    out_shape = (
        jax.ShapeDtypeStruct((batch, seq, D_MODEL), F32),
        jax.ShapeDtypeStruct((batch, KV_WIDTH, WINDOW), F32),
        jax.ShapeDtypeStruct((batch, KV_WIDTH, WINDOW), F32),
        jax.ShapeDtypeStruct((batch, CONV_W - 1, CONV_DIM), F32),
    )
    out_specs = (
        pl.BlockSpec((1, tm, D_MODEL), lambda s, l: (back(s) // tps, back(s) % tps, 0)),
        pl.BlockSpec((1, KV_WIDTH, WINDOW), lambda s, l: (front(s) // tps, 0, 0)),
        pl.BlockSpec((1, KV_WIDTH, WINDOW), lambda s, l: (front(s) // tps, 0, 0)),
        pl.BlockSpec((1, CONV_W - 1, CONV_DIM), lambda s, l: (front(s) // tps, 0, 0)),
    )
    scratch = [
        pltpu.VMEM((KV_HEADS, WINDOW, 2 * HEAD_DIM), F32),
        pltpu.VMEM((KV_HEADS, WINDOW, 2 * HEAD_DIM), F32),
        pltpu.VMEM((8, CONV_DIM), F32),
        pltpu.VMEM((8 + tm, CONV_DIM), F32),
        pltpu.VMEM((tm, ATTN_WIDTH), F32),
        pltpu.VMEM((tm, MEM_WIDTH), F32),
        pltpu.VMEM((tm, MIX_WIDTH), BF),
        pltpu.VMEM((tm, D_MODEL), F32),
    ]
    return pl.pallas_call(
        functools.partial(_prompt_layer_kernel, tiles_per_seq=tps),
        grid_spec=pltpu.PrefetchScalarGridSpec(
            num_scalar_prefetch=1, grid=(n_tiles + 1,), in_specs=in_specs,
            out_specs=out_specs, scratch_shapes=scratch),
        out_shape=out_shape,
        compiler_params=pltpu.CompilerParams(
            dimension_semantics=("arbitrary",),
            vmem_limit_bytes=VMEM_LIMIT_BYTES),
        name="prompt_layer",
    )(layer, sinks, x, mem_k_t, mem_v_t, *[weights[n] for n in _WEIGHT_ORDER])


def _sample_layer_kernel(layer_ref, sinks_ref, x_ref, ck_ref, cv_ref, cc_ref, mk_ref, mv_ref,
                         g_attn_ref, w_in_ref, bd_ref, gq_ref, gk_ref, gmq_ref, conv_w_ref,
                         g_out_ref, w_out_ref, g_ffn_ref, w_gu_ref, w_down_ref, y_ref, newk_ref,
                         newv_ref, newc_ref, conv_scr, a_scr, cy_scr, mo_scr):
    rows = CHUNK
    layer = layer_ref[0]
    x = x_ref[...]
    qn, kn, v, cb, ccx, mqn = _project_in(x, g_attn_ref[0], w_in_ref[0], bd_ref[...], gq_ref[0],
                                          gk_ref[0], gmq_ref[0])
    q_lo, q_hi = _split_heads_lo_hi(qn)
    grp = _lane_iota(mqn.shape) // HEAD_DIM
    mq_heads = [jnp.where(grp == h, mqn, 0.0).astype(BF) for h in range(MEM_HEADS)]
    conv_w = conv_w_ref[0]
    old_lane = _lane_iota((KV_WIDTH, WINDOW)) < WINDOW - rows

    def shifted_window(cache_t, new_rows):
        new_t = jnp.concatenate([new_rows, new_rows], axis=0).T
        return jnp.where(old_lane, pltpu.roll(cache_t, WINDOW - rows, axis=1), new_t)

    for b in range(G_SAMPLE):
        r0 = b * rows
        ck_t = ck_ref[0, b]
        cv_t = cv_ref[0, b]
        kn_b = kn[r0:r0 + rows]
        v_b = v[r0:r0 + rows]
        newk_ref[b] = shifted_window(ck_t, kn_b)
        newv_ref[b] = shifted_window(cv_t, v_b)
        k_dup = _dup_halves(kn_b)
        v_dup = _dup_halves(v_b)
        ck_bf = ck_t.astype(BF)
        cv_bf = cv_t.astype(BF)
        for j in range(KV_HEADS):
            c0 = j * GROUP * HEAD_DIM
            qf = _stack_group_queries(q_lo, q_hi, r0, rows, j)
            kj_t = ck_bf[j * HEAD_DIM:(j + 1) * HEAD_DIM, :]
            vj_t = cv_bf[j * HEAD_DIM:(j + 1) * HEAD_DIM, :]
            s = jnp.concatenate(
                [_dot(qf, jnp.concatenate([kj_t, kj_t], axis=0)),
                 _dot_nt(qf, k_dup[j].astype(BF))], axis=-1)
            probs = [_softmax_rows(s[h * rows:(h + 1) * rows], sinks_ref[layer, j * GROUP + h])
                     for h in range(GROUP)]
            pm = jnp.concatenate(probs, axis=0).astype(BF)
            vb = v_dup[j].astype(BF)
            o_all = (_dot_nt(pm[:, :WINDOW], jnp.concatenate([vj_t] * GROUP, axis=0))
                     + _dot(pm[:, WINDOW:], jnp.concatenate([vb, vb], axis=-1)))
            a_scr[r0:r0 + rows, c0:c0 + GROUP * HEAD_DIM] = _pick_head_lanes(o_all, rows)

        ccx_b = ccx[r0:r0 + rows]
        conv_scr[6:8, :] = cc_ref[0, b]
        conv_scr[8:8 + rows, :] = ccx_b
        cy_scr[r0:r0 + rows, :] = _conv_from_scratch(conv_scr, ccx_b, cb[r0:r0 + rows], conv_w,
                                                     rows)
        newc_ref[b] = ccx_b[rows - (CONV_W - 1):rows, :]

        mq_blocks = jnp.concatenate([mq_heads[h][r0:r0 + rows] for h in range(MEM_HEADS)], axis=0)
        mo_scr[r0:r0 + rows, :] = _mem_attention(mq_blocks, mk_ref[0, b].astype(BF),
                                                 mv_ref[0, b].astype(BF), rows)

    mix = _mix_norm(a_scr[...], cy_scr[...], mo_scr[...], g_out_ref[0])
    y_ref[...] = _out_ffn(x, mix, w_out_ref, g_ffn_ref[0], w_gu_ref, w_down_ref)


def _sample_layer(layer, x, cache_k_t, cache_v_t, cache_conv, mem_k_t, mem_v_t, sinks, weights):
    n_b = cache_k_t.shape[1]
    g = G_SAMPLE
    tm = g * CHUNK
    in_specs = [
        _SMEM_SPEC,
        pl.BlockSpec((tm, D_MODEL), lambda i, l: (i, 0)),
        pl.BlockSpec((1, g, KV_WIDTH, WINDOW), lambda i, l: (l[0], i, 0, 0)),
        pl.BlockSpec((1, g, KV_WIDTH, WINDOW), lambda i, l: (l[0], i, 0, 0)),
        pl.BlockSpec((1, g, CONV_W - 1, CONV_DIM), lambda i, l: (l[0], i, 0, 0)),
        pl.BlockSpec((1, g, MEM_WIDTH, N_MEM), lambda i, l: (l[0], i, 0, 0)),
        pl.BlockSpec((1, g, MEM_WIDTH, N_MEM), lambda i, l: (l[0], i, 0, 0)),
    ] + _weight_specs()
    out_shape = (
        jax.ShapeDtypeStruct((n_b * CHUNK, D_MODEL), F32),
        jax.ShapeDtypeStruct((n_b, KV_WIDTH, WINDOW), F32),
        jax.ShapeDtypeStruct((n_b, KV_WIDTH, WINDOW), F32),
        jax.ShapeDtypeStruct((n_b, CONV_W - 1, CONV_DIM), F32),
    )
    out_specs = (
        pl.BlockSpec((tm, D_MODEL), lambda i, l: (i, 0)),
        pl.BlockSpec((g, KV_WIDTH, WINDOW), lambda i, l: (i, 0, 0)),
        pl.BlockSpec((g, KV_WIDTH, WINDOW), lambda i, l: (i, 0, 0)),
        pl.BlockSpec((g, CONV_W - 1, CONV_DIM), lambda i, l: (i, 0, 0)),
    )
    scratch = [
        pltpu.VMEM((8 + CHUNK, CONV_DIM), F32),
        pltpu.VMEM((tm, ATTN_WIDTH), F32),
        pltpu.VMEM((tm, CONV_DIM), F32),
        pltpu.VMEM((tm, MEM_WIDTH), F32),
    ]
    return pl.pallas_call(
        _sample_layer_kernel,
        grid_spec=pltpu.PrefetchScalarGridSpec(
            num_scalar_prefetch=1, grid=(n_b // g,), in_specs=in_specs, out_specs=out_specs,
            scratch_shapes=scratch),
        out_shape=out_shape,
        compiler_params=pltpu.CompilerParams(
            dimension_semantics=("arbitrary",),
            vmem_limit_bytes=VMEM_LIMIT_BYTES),
        name="sample_layer",
    )(layer, sinks, x, cache_k_t, cache_v_t, cache_conv, mem_k_t, mem_v_t,
      *[weights[n] for n in _WEIGHT_ORDER])


def _mem_kv_kernel(mem_ref, g_mem_ref, w_ref, bd_ref, gmk_ref, mk_ref, mv_ref):
    xn = _rms(mem_ref[0], g_mem_ref[0]).astype(BF)
    kv = _dot(xn, w_ref[0])
    mk_ref[0, 0] = _head_rms(kv[:, :MEM_WIDTH], gmk_ref[0], bd_ref[...]).T
    mv_ref[0, 0] = kv[:, MEM_WIDTH:].T


def _mem_kv(mem, g_mem, w_mem_kv, bd, gmk):
    batch = mem.shape[0]
    out = jax.ShapeDtypeStruct((DEPTH, batch, MEM_WIDTH, N_MEM), F32)
    return pl.pallas_call(
        _mem_kv_kernel,
        grid=(DEPTH, batch),
        in_specs=[
            pl.BlockSpec((1, N_MEM, D_MODEL), lambda l, i: (i, 0, 0)),
            pl.BlockSpec((1, 1, D_MODEL), lambda l, i: (l, 0, 0)),
            pl.BlockSpec((1, D_MODEL, 2 * MEM_WIDTH), lambda l, i: (l, 0, 0)),
            pl.BlockSpec((256, 256), lambda l, i: (0, 0)),
            pl.BlockSpec((1, 1, MEM_WIDTH), lambda l, i: (l, 0, 0)),
        ],
        out_specs=(pl.BlockSpec((1, 1, MEM_WIDTH, N_MEM), lambda l, i: (l, i, 0, 0)),
                   pl.BlockSpec((1, 1, MEM_WIDTH, N_MEM), lambda l, i: (l, i, 0, 0))),
        out_shape=(out, out),
        compiler_params=pltpu.CompilerParams(
            dimension_semantics=("arbitrary", "arbitrary"),
            vmem_limit_bytes=VMEM_LIMIT_BYTES),
        name="prompt_mem_kv",
    )(mem, g_mem, w_mem_kv, bd, gmk)


def _tile_heads(g, n):
    return jnp.tile(g, (1, n)).reshape(g.shape[0], 1, n * g.shape[1])


def _feature_major(c):
    lead = c.shape[:-3]
    n_tok, heads, dim = c.shape[-3:]
    perm = tuple(range(len(lead))) + (len(lead) + 1, len(lead) + 2, len(lead))
    return jnp.transpose(c, perm).reshape(lead + (heads * dim, n_tok))


def _token_major(c_t, heads):
    lead = c_t.shape[:-2]
    n_tok = c_t.shape[-1]
    c = c_t.reshape(lead + (heads, HEAD_DIM, n_tok))
    perm = tuple(range(len(lead))) + (len(lead) + 2, len(lead), len(lead) + 1)
    return jnp.transpose(c, perm)


def kernel(x_prompt, x_sample, mem_prompt, cache_win_k, cache_win_v, cache_conv, cache_mem_k,
           cache_mem_v, attn_norm_g, w_in, q_norm_g, k_norm_g, sinks, conv_w, mem_norm_g,
           w_mem_kv, mq_norm_g, mk_norm_g, out_norm_g, w_out, ffn_norm_g, w_gate_up, w_down):
    batch, seq, _ = x_prompt.shape
    dec_batch, dec_seq, _ = x_sample.shape
    assert dec_seq == CHUNK and seq % TM_PROMPT == 0 and dec_batch % G_SAMPLE == 0

    head = jnp.arange(256) // HEAD_DIM
    bd = jnp.where(head[:, None] == head[None, :], 1.0 / HEAD_DIM, 0.0).astype(BF)

    weights = dict(
        g_attn=attn_norm_g.reshape(DEPTH, 1, D_MODEL), w_in=w_in.astype(BF), bd=bd,
        gq=_tile_heads(q_norm_g, N_HEADS), gk=_tile_heads(k_norm_g, KV_HEADS),
        gmq=_tile_heads(mq_norm_g, MEM_HEADS), conv_w=conv_w,
        g_out=out_norm_g.reshape(DEPTH, 1, MIX_WIDTH), w_out=w_out.astype(BF),
        g_ffn=ffn_norm_g.reshape(DEPTH, 1, D_MODEL), w_gu=w_gate_up.astype(BF),
        w_down=w_down.astype(BF))

    mk_t, mv_t = _mem_kv(mem_prompt, mem_norm_g.reshape(DEPTH, 1, D_MODEL), w_mem_kv.astype(BF),
                         bd, _tile_heads(mk_norm_g, MEM_HEADS))

    ck_t = _feature_major(cache_win_k)
    cv_t = _feature_major(cache_win_v)
    cmk_t = _feature_major(cache_mem_k)
    cmv_t = _feature_major(cache_mem_v)

    yp = x_prompt
    ys = x_sample.reshape(dec_batch * dec_seq, D_MODEL)
    wk_p, wv_p, cv_p, wk_s, wv_s, cv_s = [], [], [], [], [], []
    for l in range(DEPTH):
        layer = jnp.full((1,), l, jnp.int32)
        yp, k_p, v_p, c_p = _prompt_layer(layer, yp, mk_t, mv_t, sinks, weights)
        wk_p.append(k_p); wv_p.append(v_p); cv_p.append(c_p)
        ys, k_s, v_s, c_s = _sample_layer(layer, ys, ck_t, cv_t, cache_conv, cmk_t, cmv_t, sinks,
                                          weights)
        wk_s.append(k_s); wv_s.append(v_s); cv_s.append(c_s)

    return (yp, ys.reshape(dec_batch, dec_seq, D_MODEL),
            _token_major(jnp.stack(wk_p), KV_HEADS), _token_major(jnp.stack(wv_p), KV_HEADS),
            jnp.stack(cv_p),
            _token_major(mk_t, MEM_HEADS), _token_major(mv_t, MEM_HEADS),
            _token_major(jnp.stack(wk_s), KV_HEADS), _token_major(jnp.stack(wv_s), KV_HEADS),
            jnp.stack(cv_s))
```

```python
import functools

import jax
import jax.numpy as jnp
from jax import lax
from jax.experimental import pallas as pl
from jax.experimental.pallas import tpu as pltpu

D_MODEL = 1024
DEPTH = 4
CHUNK = 64
HEAD_DIM = 64
N_HEADS = 8
KV_HEADS = 2
GROUP = N_HEADS // KV_HEADS
WINDOW = 128
ATTN_WIDTH = N_HEADS * HEAD_DIM
KV_WIDTH = KV_HEADS * HEAD_DIM
CONV_DIM = 256
CONV_W = 3
MEM_HEADS = 4
MEM_WIDTH = MEM_HEADS * HEAD_DIM
N_MEM = 256
MIX_WIDTH = ATTN_WIDTH + CONV_DIM + MEM_WIDTH
IN_WIDTH = ATTN_WIDTH + 2 * KV_WIDTH + 3 * CONV_DIM + MEM_WIDTH
D_FF = 2816
EPS = 1e-6
ATTN_SCALE = HEAD_DIM ** -0.5
NEG = -1e30

_Q0, _K0, _V0 = 0, ATTN_WIDTH, ATTN_WIDTH + KV_WIDTH
_CB0 = ATTN_WIDTH + 2 * KV_WIDTH
_CC0 = _CB0 + CONV_DIM
_CX0 = _CC0 + CONV_DIM
_MQ0 = _CX0 + CONV_DIM

PAIR = 2 * CHUNK
TM_PROMPT = 512
G_SAMPLE = 8
FF_CHUNKS = (1024, 1024, 768)
FF_BLOCK = 256
VMEM_LIMIT_BYTES = 56 * 1024 * 1024

BF = jnp.bfloat16
F32 = jnp.float32


def _dot(a, b):
    return jnp.dot(a, b, preferred_element_type=F32)


def _dot_nt(a, b):
    return lax.dot_general(a, b, (((1,), (1,)), ((), ())), preferred_element_type=F32)


def _rms(x, g):
    ms = jnp.mean(x * x, axis=-1, keepdims=True)
    return x * lax.rsqrt(ms + EPS) * g


def _head_rms(z, g, bd):
    w = z.shape[-1]
    ms = _dot((z * z).astype(BF), bd[:w, :w])
    return z * lax.rsqrt(ms + EPS) * g


def _lane_iota(shape):
    return lax.broadcasted_iota(jnp.int32, shape, len(shape) - 1)


def _split_heads_lo_hi(z):
    lo = (_lane_iota(z.shape) % (2 * HEAD_DIM)) < HEAD_DIM
    return jnp.where(lo, z, 0.0).astype(BF), jnp.where(lo, 0.0, z).astype(BF)


def _stack_group_queries(q_lo, q_hi, r0, rows, j):
    c0 = j * GROUP * HEAD_DIM
    return jnp.concatenate(
        [q_lo[r0:r0 + rows, c0:c0 + 128], q_hi[r0:r0 + rows, c0:c0 + 128],
         q_lo[r0:r0 + rows, c0 + 128:c0 + 256], q_hi[r0:r0 + rows, c0 + 128:c0 + 256]], axis=0)


def _dup_halves(z):
    sw = pltpu.roll(z, HEAD_DIM, axis=1)
    lo = _lane_iota(z.shape) < HEAD_DIM
    return jnp.where(lo, z, sw), jnp.where(lo, sw, z)


def _pick_head_lanes(o_all, rows):
    grp = _lane_iota((rows, GROUP * HEAD_DIM)) // HEAD_DIM
    out = o_all[3 * rows:4 * rows]
    for h in (2, 1, 0):
        out = jnp.where(grp == h, o_all[h * rows:(h + 1) * rows], out)
    return out


def _softmax_rows(s, sink):
    m = jnp.max(s, axis=-1, keepdims=True)
    if sink is not None:
        m = jnp.maximum(m, sink)
    p = jnp.exp(s - m)
    den = jnp.sum(p, axis=-1, keepdims=True)
    if sink is not None:
        den = den + jnp.exp(sink - m)
    return p * (1.0 / den)


def _project_in(x, g_attn, w_in, bd, gq, gk, gmq):
    xn = _rms(x, g_attn).astype(BF)
    u = _dot(xn, w_in)
    half = ATTN_WIDTH // 2
    qn = jnp.concatenate(
        [_head_rms(u[:, _Q0:_Q0 + half], gq[:, :half], bd),
         _head_rms(u[:, _Q0 + half:_K0], gq[:, half:], bd)], axis=-1) * ATTN_SCALE
    kn = _head_rms(u[:, _K0:_V0], gk, bd)
    v = u[:, _V0:_CB0]
    cb = u[:, _CB0:_CC0]
    ccx = u[:, _CC0:_CX0] * u[:, _CX0:_MQ0]
    mqn = _head_rms(u[:, _MQ0:], gmq, bd) * ATTN_SCALE
    return qn, kn, v, cb, ccx, mqn


def _mem_attention(mq_blocks, mk_t, mv_t, rows):
    s = _dot(mq_blocks, mk_t)
    p = _softmax_rows(s, None).astype(BF)
    return _pick_head_lanes(_dot_nt(p, mv_t), rows)


def _mix_norm(a, cy, mo, g_out):
    return jnp.concatenate(
        [_rms(a, g_out[:, :ATTN_WIDTH]),
         _rms(cy, g_out[:, ATTN_WIDTH:ATTN_WIDTH + CONV_DIM]),
         _rms(mo, g_out[:, ATTN_WIDTH + CONV_DIM:])], axis=-1).astype(BF)


def _ffn_act(hn, w_gu_ref, c0, width):
    gate = _dot(hn, w_gu_ref[0, :, c0:c0 + width])
    up = _dot(hn, w_gu_ref[0, :, D_FF + c0:D_FF + c0 + width])
    return (gate * jax.nn.sigmoid(gate) * up).astype(BF)


def _ffn_chunk(hn, w_gu_ref, w_down_ref, c0, width):
    return _dot(_ffn_act(hn, w_gu_ref, c0, width), w_down_ref[0, c0:c0 + width, :])


def _out_ffn(x, mix, w_out_ref, g_ffn, w_gu_ref, w_down_ref):
    h = x + _dot(mix, w_out_ref[0])
    hn = _rms(h, g_ffn).astype(BF)
    y = h
    c0 = 0
    for width in FF_CHUNKS:
        y = y + _ffn_chunk(hn, w_gu_ref, w_down_ref, c0, width)
        c0 += width
    return y


def _conv_from_scratch(conv_scr, ccx, cb, conv_w, rows):
    sh2 = conv_scr[6:6 + rows, :]
    sh1 = conv_scr[7:7 + rows, :]
    return cb * (sh2 * conv_w[0:1, :] + sh1 * conv_w[1:2, :] + ccx * conv_w[2:3, :])


def _prompt_layer_kernel(layer_ref, sinks_ref, x_ref, mk_ref, mv_ref, g_attn_ref, w_in_ref, bd_ref,
                         gq_ref, gk_ref, gmq_ref, conv_w_ref, g_out_ref, w_out_ref, g_ffn_ref,
                         w_gu_ref, w_down_ref, nw_in_ref, nw_out_ref, nw_gu_ref, nw_down_ref,
                         y_ref, newk_ref, newv_ref, newc_ref,
                         nw_in_bf_ref, nw_out_bf_ref, nw_gu_bf_ref, nw_down_bf_ref,
                         kcar_scr, vcar_scr, ccar_scr, conv_scr, a_scr, mo_scr, mix_scr,
                         xres_scr, *,
                         tiles_per_seq):
    tm = TM_PROMPT
    layer = layer_ref[0]
    s = pl.program_id(0)
    n_tiles = pl.num_programs(0) - 1
    t = jnp.minimum(s, n_tiles - 1) % tiles_per_seq

    @pl.when(s == 0)
    def _():
        mix_scr[...] = jnp.zeros(mix_scr.shape, BF)
        xres_scr[...] = jnp.zeros(xres_scr.shape, F32)

    h_prev = xres_scr[...] + _dot(mix_scr[...], w_out_ref[0])
    hn_prev = _rms(h_prev, g_ffn_ref[0]).astype(BF)

    def back_units():
        blk = FF_BLOCK
        n_blk = D_FF // blk
        acts = []
        y = h_prev

        def down(c):
            lo, hi = 2 * c, min(2 * c + 2, n_blk)
            act = acts[lo] if hi - lo == 1 else jnp.concatenate(acts[lo:hi], axis=-1)
            return _dot(act, w_down_ref[0, lo * blk:hi * blk, :])

        for b in range(n_blk):
            gate = _dot(hn_prev, w_gu_ref[0, :, b * blk:(b + 1) * blk])
            yield
            up = _dot(hn_prev, w_gu_ref[0, :, D_FF + b * blk:D_FF + (b + 1) * blk])
            acts.append((gate * jax.nn.sigmoid(gate) * up).astype(BF))
            yield
            if b >= 2 and b % 2 == 0:
                y = y + down(b // 2 - 1)
                yield
        for c in range((n_blk - 1) // 2, (n_blk + 1) // 2):
            y = y + down(c)
            yield
        y_ref[0] = y

    back = back_units()

    started = t > 0
    k_prev = [jnp.where(started, kcar_scr[j], 0.0).astype(BF) for j in range(KV_HEADS)]
    v_prev = [jnp.where(started, vcar_scr[j], 0.0).astype(BF) for j in range(KV_HEADS)]
    conv_scr[0:8, :] = jnp.where(started, ccar_scr[...], 0.0)

    x = x_ref[0]
    qn, kn, v, cb, ccx, mqn = _project_in(x, g_attn_ref[0], w_in_ref[0], bd_ref[...], gq_ref[0],
                                          gk_ref[0], gmq_ref[0])

    k_dup = _dup_halves(kn)
    v_dup = _dup_halves(v)
    for j in range(KV_HEADS):
        kcar_scr[j] = k_dup[j][tm - WINDOW:tm, :]
        vcar_scr[j] = v_dup[j][tm - WINDOW:tm, :]
    k_rep = [k_dup[j].astype(BF) for j in range(KV_HEADS)]
    v_rep = [v_dup[j].astype(BF) for j in range(KV_HEADS)]

    q_lo, q_hi = _split_heads_lo_hi(qn)

    row_chunk = lax.broadcasted_iota(jnp.int32, (PAIR, 2 * PAIR), 0) // CHUNK
    col = _lane_iota((PAIR, 2 * PAIR))
    first_key = jnp.where(started, 0, WINDOW)
    visible = (col >= row_chunk * CHUNK) & (col < (row_chunk + 3) * CHUNK)
    bias_rest = jnp.where(visible, 0.0, NEG)
    bias_first = jnp.where(col >= first_key, bias_rest, NEG)

    grp = _lane_iota(mqn.shape) // HEAD_DIM
    mk_t = mk_ref[0, 0].astype(BF)
    mv_t = mv_ref[0, 0].astype(BF)
    g_out = g_out_ref[0]
    a_end, c_end = ATTN_WIDTH, ATTN_WIDTH + CONV_DIM

    def front_units():
        for p in range(tm // PAIR):
            r0 = p * PAIR
            bias = bias_first if p == 0 else bias_rest
            for j in range(KV_HEADS):
                c0 = j * GROUP * HEAD_DIM
                qf = _stack_group_queries(q_lo, q_hi, r0, PAIR, j)
                if p == 0:
                    k_win = jnp.concatenate([k_prev[j], k_rep[j][0:PAIR]], axis=0)
                    v_win = jnp.concatenate([v_prev[j], v_rep[j][0:PAIR]], axis=0)
                else:
                    k_win = k_rep[j][r0 - PAIR:r0 + PAIR]
                    v_win = v_rep[j][r0 - PAIR:r0 + PAIR]
                s = _dot_nt(qf, k_win)
                yield
                probs = []
                for h in range(GROUP):
                    sh = s[h * PAIR:(h + 1) * PAIR] + bias
                    probs.append(_softmax_rows(sh, sinks_ref[layer, j * GROUP + h]).astype(BF))
                    if h % 2 == 1:
                        yield
                o_all = _dot(jnp.concatenate(probs, axis=0),
                             jnp.concatenate([v_win, v_win], axis=-1))
                a_scr[r0:r0 + PAIR, c0:c0 + GROUP * HEAD_DIM] = _pick_head_lanes(o_all, PAIR)
                yield
        mix_scr[:, :a_end] = _rms(a_scr[...], g_out[:, :a_end]).astype(BF)
        yield
        for h in range(MEM_HEADS):
            mq_h = jnp.where(grp == h, mqn, 0.0).astype(BF)
            s = _dot(mq_h, mk_t)
            yield
            pm = _softmax_rows(s, None).astype(BF)
            yield
            o_h = _dot_nt(pm, mv_t)
            mo_scr[:, h * HEAD_DIM:(h + 1) * HEAD_DIM] = o_h[:, h * HEAD_DIM:(h + 1) * HEAD_DIM]
            yield

    xres_scr[...] = x

    conv_scr[8:8 + tm, :] = ccx
    cy = _conv_from_scratch(conv_scr, ccx, cb, conv_w_ref[0], tm)
    ccar_scr[...] = ccx[tm - 8:tm, :]
    mix_scr[:, a_end:c_end] = _rms(cy, g_out[:, a_end:c_end]).astype(BF)

    n_back = 2 * (D_FF // FF_BLOCK) + (D_FF // FF_BLOCK + 1) // 2
    n_front = (tm // PAIR) * KV_HEADS * 4 + 1 + MEM_HEADS * 3
    front = front_units()
    done_front = 0
    for i in range(n_back):
        next(back)
        want = ((i + 1) * n_front) // n_back
        for _ in range(want - done_front):
            next(front)
        done_front = want
    assert next(back, "end") == "end" and next(front, "end") == "end"
    mix_scr[:, c_end:] = _rms(mo_scr[...], g_out[:, c_end:]).astype(BF)

    @pl.when(t == tiles_per_seq - 1)
    def _():
        newk_ref[0] = kn[tm - WINDOW:tm, :].T
        newv_ref[0] = v[tm - WINDOW:tm, :].T
        newc_ref[0] = ccx[tm - (CONV_W - 1):tm, :]

    nw_in_bf_ref[...] = nw_in_ref[...].astype(BF)
    nw_out_bf_ref[...] = nw_out_ref[...].astype(BF)
    nw_gu_bf_ref[...] = nw_gu_ref[...].astype(BF)
    nw_down_bf_ref[...] = nw_down_ref[...].astype(BF)


def _layer_spec(shape):
    zeros = (0,) * len(shape)
    return pl.BlockSpec((1,) + shape, lambda *args: (args[-1][0],) + zeros,
                        pipeline_mode=pl.Buffered(1))


def _matrix_spec(shape):
    return pl.BlockSpec((1,) + shape, lambda *args: (0, 0, 0), pipeline_mode=pl.Buffered(1))


_MATRIX_SHAPES = dict(w_in=(D_MODEL, IN_WIDTH), w_out=(MIX_WIDTH, D_MODEL),
                      w_gu=(D_MODEL, 2 * D_FF), w_down=(D_FF, D_MODEL))
_MATRIX_ORDER = ("w_in", "w_out", "w_gu", "w_down")
_CAST_BLOCKS = dict(w_in=32, w_out=32, w_gu=32, w_down=16)


def _weight_specs():
    return [
        _layer_spec((1, D_MODEL)),
        _matrix_spec(_MATRIX_SHAPES["w_in"]),
        pl.BlockSpec((256, 256), lambda *args: (0, 0), pipeline_mode=pl.Buffered(1)),
        _layer_spec((1, ATTN_WIDTH)),
        _layer_spec((1, KV_WIDTH)),
        _layer_spec((1, MEM_WIDTH)),
        _layer_spec((CONV_W, CONV_DIM)),
        _layer_spec((1, MIX_WIDTH)),
        _matrix_spec(_MATRIX_SHAPES["w_out"]),
        _layer_spec((1, D_MODEL)),
        _matrix_spec(_MATRIX_SHAPES["w_gu"]),
        _matrix_spec(_MATRIX_SHAPES["w_down"]),
    ]


_WEIGHT_ORDER = ("g_attn", "w_in", "bd", "gq", "gk", "gmq", "conv_w", "g_out", "w_out", "g_ffn",
                 "w_gu", "w_down")

_SMEM_SPEC = pl.BlockSpec(memory_space=pltpu.SMEM)


def _prompt_layer(layer, x, mem_k_t, mem_v_t, sinks, weights, f32_matrices):
    batch, seq, _ = x.shape
    tm = TM_PROMPT
    tps = seq // tm
    n_tiles = batch * tps

    def front(s):
        return jnp.minimum(s, n_tiles - 1)

    def back(s):
        return jnp.maximum(s - 1, 0)

    def cast_specs(name):
        rows, cols = _MATRIX_SHAPES[name]
        n_blk = _CAST_BLOCKS[name]
        assert n_blk <= n_tiles and rows % (16 * n_blk) == 0
        blk = (1, rows // n_blk, cols)
        src = pl.BlockSpec(blk, lambda s, l: (jnp.minimum(l[0] + 1, DEPTH - 1),
                                              jnp.minimum(s, n_blk - 1), 0))
        dst = pl.BlockSpec(blk, lambda s, l: (0, jnp.minimum(s, n_blk - 1), 0))
        return src, dst, jax.ShapeDtypeStruct((1, rows, cols), BF)

    casts = [cast_specs(n) for n in _MATRIX_ORDER]
    in_specs = [
        _SMEM_SPEC,
        pl.BlockSpec((1, tm, D_MODEL), lambda s, l: (front(s) // tps, front(s) % tps, 0)),
        pl.BlockSpec((1, 1, MEM_WIDTH, N_MEM), lambda s, l: (l[0], front(s) // tps, 0, 0)),
        pl.BlockSpec((1, 1, MEM_WIDTH, N_MEM), lambda s, l: (l[0], front(s) // tps, 0, 0)),
    ] + _weight_specs() + [c[0] for c in casts]
    out_shape = (
        jax.ShapeDtypeStruct((batch, seq, D_MODEL), F32),
        jax.ShapeDtypeStruct((batch, KV_WIDTH, WINDOW), F32),
        jax.ShapeDtypeStruct((batch, KV_WIDTH, WINDOW), F32),
        jax.ShapeDtypeStruct((batch, CONV_W - 1, CONV_DIM), F32),
    ) + tuple(c[2] for c in casts)
    out_specs = (
        pl.BlockSpec((1, tm, D_MODEL), lambda s, l: (back(s) // tps, back(s) % tps, 0)),
        pl.BlockSpec((1, KV_WIDTH, WINDOW), lambda s, l: (front(s) // tps, 0, 0)),
        pl.BlockSpec((1, KV_WIDTH, WINDOW), lambda s, l: (front(s) // tps, 0, 0)),
        pl.BlockSpec((1, CONV_W - 1, CONV_DIM), lambda s, l: (front(s) // tps, 0, 0)),
    ) + tuple(c[1] for c in casts)
    scratch = [
        pltpu.VMEM((KV_HEADS, WINDOW, 2 * HEAD_DIM), F32),
        pltpu.VMEM((KV_HEADS, WINDOW, 2 * HEAD_DIM), F32),
        pltpu.VMEM((8, CONV_DIM), F32),
        pltpu.VMEM((8 + tm, CONV_DIM), F32),
        pltpu.VMEM((tm, ATTN_WIDTH), F32),
        pltpu.VMEM((tm, MEM_WIDTH), F32),
        pltpu.VMEM((tm, MIX_WIDTH), BF),
        pltpu.VMEM((tm, D_MODEL), F32),
    ]
    return pl.pallas_call(
        functools.partial(_prompt_layer_kernel, tiles_per_seq=tps),
        grid_spec=pltpu.PrefetchScalarGridSpec(
            num_scalar_prefetch=1, grid=(n_tiles + 1,), in_specs=in_specs,
            out_specs=out_specs, scratch_shapes=scratch),
        out_shape=out_shape,
        compiler_params=pltpu.CompilerParams(
            dimension_semantics=("arbitrary",),
            vmem_limit_bytes=VMEM_LIMIT_BYTES),
        name="prompt_layer",
    )(layer, sinks, x, mem_k_t, mem_v_t, *[weights[n] for n in _WEIGHT_ORDER],
      *[f32_matrices[n] for n in _MATRIX_ORDER])


def _sample_layer_kernel(layer_ref, sinks_ref, x_ref, ck_ref, cv_ref, cc_ref, mk_ref, mv_ref,
                         g_attn_ref, w_in_ref, bd_ref, gq_ref, gk_ref, gmq_ref, conv_w_ref,
                         g_out_ref, w_out_ref, g_ffn_ref, w_gu_ref, w_down_ref, y_ref, newk_ref,
                         newv_ref, newc_ref, conv_scr, a_scr, cy_scr, mo_scr):
    rows = CHUNK
    layer = layer_ref[0]
    x = x_ref[...]
    qn, kn, v, cb, ccx, mqn = _project_in(x, g_attn_ref[0], w_in_ref[0], bd_ref[...], gq_ref[0],
                                          gk_ref[0], gmq_ref[0])
    q_lo, q_hi = _split_heads_lo_hi(qn)
    grp = _lane_iota(mqn.shape) // HEAD_DIM
    mq_heads = [jnp.where(grp == h, mqn, 0.0).astype(BF) for h in range(MEM_HEADS)]
    conv_w = conv_w_ref[0]
    old_lane = _lane_iota((KV_WIDTH, WINDOW)) < WINDOW - rows

    def shifted_window(cache_t, new_rows):
        new_t = jnp.concatenate([new_rows, new_rows], axis=0).T
        return jnp.where(old_lane, pltpu.roll(cache_t, WINDOW - rows, axis=1), new_t)

    for b in range(G_SAMPLE):
        r0 = b * rows
        ck_t = ck_ref[0, b]
        cv_t = cv_ref[0, b]
        kn_b = kn[r0:r0 + rows]
        v_b = v[r0:r0 + rows]
        newk_ref[b] = shifted_window(ck_t, kn_b)
        newv_ref[b] = shifted_window(cv_t, v_b)
        k_dup = _dup_halves(kn_b)
        v_dup = _dup_halves(v_b)
        ck_bf = ck_t.astype(BF)
        cv_bf = cv_t.astype(BF)
        for j in range(KV_HEADS):
            c0 = j * GROUP * HEAD_DIM
            qf = _stack_group_queries(q_lo, q_hi, r0, rows, j)
            kj_t = ck_bf[j * HEAD_DIM:(j + 1) * HEAD_DIM, :]
            vj_t = cv_bf[j * HEAD_DIM:(j + 1) * HEAD_DIM, :]
            s = jnp.concatenate(
                [_dot(qf, jnp.concatenate([kj_t, kj_t], axis=0)),
                 _dot_nt(qf, k_dup[j].astype(BF))], axis=-1)
            probs = [_softmax_rows(s[h * rows:(h + 1) * rows], sinks_ref[layer, j * GROUP + h])
                     for h in range(GROUP)]
            pm = jnp.concatenate(probs, axis=0).astype(BF)
            vb = v_dup[j].astype(BF)
            o_all = (_dot_nt(pm[:, :WINDOW], jnp.concatenate([vj_t] * GROUP, axis=0))
                     + _dot(pm[:, WINDOW:], jnp.concatenate([vb, vb], axis=-1)))
            a_scr[r0:r0 + rows, c0:c0 + GROUP * HEAD_DIM] = _pick_head_lanes(o_all, rows)

        ccx_b = ccx[r0:r0 + rows]
        conv_scr[6:8, :] = cc_ref[0, b]
        conv_scr[8:8 + rows, :] = ccx_b
        cy_scr[r0:r0 + rows, :] = _conv_from_scratch(conv_scr, ccx_b, cb[r0:r0 + rows], conv_w,
                                                     rows)
        newc_ref[b] = ccx_b[rows - (CONV_W - 1):rows, :]

        mq_blocks = jnp.concatenate([mq_heads[h][r0:r0 + rows] for h in range(MEM_HEADS)], axis=0)
        mo_scr[r0:r0 + rows, :] = _mem_attention(mq_blocks, mk_ref[0, b].astype(BF),
                                                 mv_ref[0, b].astype(BF), rows)

    mix = _mix_norm(a_scr[...], cy_scr[...], mo_scr[...], g_out_ref[0])
    y_ref[...] = _out_ffn(x, mix, w_out_ref, g_ffn_ref[0], w_gu_ref, w_down_ref)


def _sample_layer(layer, x, cache_k_t, cache_v_t, cache_conv, mem_k_t, mem_v_t, sinks, weights):
    n_b = cache_k_t.shape[1]
    g = G_SAMPLE
    tm = g * CHUNK
    in_specs = [
        _SMEM_SPEC,
        pl.BlockSpec((tm, D_MODEL), lambda i, l: (i, 0)),
        pl.BlockSpec((1, g, KV_WIDTH, WINDOW), lambda i, l: (l[0], i, 0, 0)),
        pl.BlockSpec((1, g, KV_WIDTH, WINDOW), lambda i, l: (l[0], i, 0, 0)),
        pl.BlockSpec((1, g, CONV_W - 1, CONV_DIM), lambda i, l: (l[0], i, 0, 0)),
        pl.BlockSpec((1, g, MEM_WIDTH, N_MEM), lambda i, l: (l[0], i, 0, 0)),
        pl.BlockSpec((1, g, MEM_WIDTH, N_MEM), lambda i, l: (l[0], i, 0, 0)),
    ] + _weight_specs()
    out_shape = (
        jax.ShapeDtypeStruct((n_b * CHUNK, D_MODEL), F32),
        jax.ShapeDtypeStruct((n_b, KV_WIDTH, WINDOW), F32),
        jax.ShapeDtypeStruct((n_b, KV_WIDTH, WINDOW), F32),
        jax.ShapeDtypeStruct((n_b, CONV_W - 1, CONV_DIM), F32),
    )
    out_specs = (
        pl.BlockSpec((tm, D_MODEL), lambda i, l: (i, 0)),
        pl.BlockSpec((g, KV_WIDTH, WINDOW), lambda i, l: (i, 0, 0)),
        pl.BlockSpec((g, KV_WIDTH, WINDOW), lambda i, l: (i, 0, 0)),
        pl.BlockSpec((g, CONV_W - 1, CONV_DIM), lambda i, l: (i, 0, 0)),
    )
    scratch = [
        pltpu.VMEM((8 + CHUNK, CONV_DIM), F32),
        pltpu.VMEM((tm, ATTN_WIDTH), F32),
        pltpu.VMEM((tm, CONV_DIM), F32),
        pltpu.VMEM((tm, MEM_WIDTH), F32),
    ]
    return pl.pallas_call(
        _sample_layer_kernel,
        grid_spec=pltpu.PrefetchScalarGridSpec(
            num_scalar_prefetch=1, grid=(n_b // g,), in_specs=in_specs, out_specs=out_specs,
            scratch_shapes=scratch),
        out_shape=out_shape,
        compiler_params=pltpu.CompilerParams(
            dimension_semantics=("arbitrary",),
            vmem_limit_bytes=VMEM_LIMIT_BYTES),
        name="sample_layer",
    )(layer, sinks, x, cache_k_t, cache_v_t, cache_conv, mem_k_t, mem_v_t,
      *[weights[n] for n in _WEIGHT_ORDER])


def _mem_kv_kernel(mem_ref, g_mem_ref, w_ref, bd_ref, gmk_ref, mk_ref, mv_ref):
    w = w_ref[0].astype(BF)
    for b in range(mem_ref.shape[0]):
        xn = _rms(mem_ref[b], g_mem_ref[0]).astype(BF)
        kv = _dot(xn, w)
        mk_ref[0, b] = _head_rms(kv[:, :MEM_WIDTH], gmk_ref[0], bd_ref[...]).T
        mv_ref[0, b] = kv[:, MEM_WIDTH:].T


def _mem_kv(mem, g_mem, w_mem_kv, bd, gmk):
    batch = mem.shape[0]
    out = jax.ShapeDtypeStruct((DEPTH, batch, MEM_WIDTH, N_MEM), F32)
    return pl.pallas_call(
        _mem_kv_kernel,
        grid=(DEPTH,),
        in_specs=[
            pl.BlockSpec((batch, N_MEM, D_MODEL), lambda l: (0, 0, 0),
                         pipeline_mode=pl.Buffered(1)),
            pl.BlockSpec((1, 1, D_MODEL), lambda l: (l, 0, 0)),
            pl.BlockSpec((1, D_MODEL, 2 * MEM_WIDTH), lambda l: (l, 0, 0)),
            pl.BlockSpec((256, 256), lambda l: (0, 0)),
            pl.BlockSpec((1, 1, MEM_WIDTH), lambda l: (l, 0, 0)),
        ],
        out_specs=(pl.BlockSpec((1, batch, MEM_WIDTH, N_MEM), lambda l: (l, 0, 0, 0)),
                   pl.BlockSpec((1, batch, MEM_WIDTH, N_MEM), lambda l: (l, 0, 0, 0))),
        out_shape=(out, out),
        compiler_params=pltpu.CompilerParams(
            dimension_semantics=("arbitrary",),
            vmem_limit_bytes=VMEM_LIMIT_BYTES),
        name="prompt_mem_kv",
    )(mem, g_mem, w_mem_kv, bd, gmk)


def _tile_heads(g, n):
    return jnp.tile(g, (1, n)).reshape(g.shape[0], 1, n * g.shape[1])


def _feature_major(c):
    lead = c.shape[:-3]
    n_tok, heads, dim = c.shape[-3:]
    perm = tuple(range(len(lead))) + (len(lead) + 1, len(lead) + 2, len(lead))
    return jnp.transpose(c, perm).reshape(lead + (heads * dim, n_tok))


def _token_major(c_t, heads):
    lead = c_t.shape[:-2]
    n_tok = c_t.shape[-1]
    c = c_t.reshape(lead + (heads, HEAD_DIM, n_tok))
    perm = tuple(range(len(lead))) + (len(lead) + 2, len(lead), len(lead) + 1)
    return jnp.transpose(c, perm)


def kernel(x_prompt, x_sample, mem_prompt, cache_win_k, cache_win_v, cache_conv, cache_mem_k,
           cache_mem_v, attn_norm_g, w_in, q_norm_g, k_norm_g, sinks, conv_w, mem_norm_g,
           w_mem_kv, mq_norm_g, mk_norm_g, out_norm_g, w_out, ffn_norm_g, w_gate_up, w_down):
    batch, seq, _ = x_prompt.shape
    dec_batch, dec_seq, _ = x_sample.shape
    assert dec_seq == CHUNK and seq % TM_PROMPT == 0 and dec_batch % G_SAMPLE == 0

    head = jnp.arange(256) // HEAD_DIM
    bd = jnp.where(head[:, None] == head[None, :], 1.0 / HEAD_DIM, 0.0).astype(BF)

    f32_matrices = dict(w_in=w_in, w_out=w_out, w_gu=w_gate_up, w_down=w_down)
    matrices = {n: m[0:1].astype(BF) for n, m in f32_matrices.items()}
    small = dict(
        g_attn=attn_norm_g.reshape(DEPTH, 1, D_MODEL), bd=bd,
        gq=_tile_heads(q_norm_g, N_HEADS), gk=_tile_heads(k_norm_g, KV_HEADS),
        gmq=_tile_heads(mq_norm_g, MEM_HEADS), conv_w=conv_w,
        g_out=out_norm_g.reshape(DEPTH, 1, MIX_WIDTH),
        g_ffn=ffn_norm_g.reshape(DEPTH, 1, D_MODEL))

    mk_t, mv_t = _mem_kv(mem_prompt, mem_norm_g.reshape(DEPTH, 1, D_MODEL), w_mem_kv, bd,
                         _tile_heads(mk_norm_g, MEM_HEADS))

    ck_t = _feature_major(cache_win_k)
    cv_t = _feature_major(cache_win_v)
    cmk_t = _feature_major(cache_mem_k)
    cmv_t = _feature_major(cache_mem_v)

    yp = x_prompt
    ys = x_sample.reshape(dec_batch * dec_seq, D_MODEL)
    wk_p, wv_p, cv_p, wk_s, wv_s, cv_s = [], [], [], [], [], []
    for l in range(DEPTH):
        layer = jnp.full((1,), l, jnp.int32)
        weights = dict(small, **matrices)
        yp, k_p, v_p, c_p, *next_matrices = _prompt_layer(layer, yp, mk_t, mv_t, sinks, weights,
                                                          f32_matrices)
        wk_p.append(k_p); wv_p.append(v_p); cv_p.append(c_p)
        ys, k_s, v_s, c_s = _sample_layer(layer, ys, ck_t, cv_t, cache_conv, cmk_t, cmv_t, sinks,
                                          weights)
        wk_s.append(k_s); wv_s.append(v_s); cv_s.append(c_s)
        matrices = dict(zip(_MATRIX_ORDER, next_matrices))

    return (yp, ys.reshape(dec_batch, dec_seq, D_MODEL),
            _token_major(jnp.stack(wk_p), KV_HEADS), _token_major(jnp.stack(wv_p), KV_HEADS),
            jnp.stack(cv_p),
            _token_major(mk_t, MEM_HEADS), _token_major(mv_t, MEM_HEADS),
            _token_major(jnp.stack(wk_s), KV_HEADS), _token_major(jnp.stack(wv_s), KV_HEADS),
            jnp.stack(cv_s))
```

```python
import functools

import jax
import jax.numpy as jnp
from jax import lax
from jax.experimental import pallas as pl
from jax.experimental.pallas import tpu as pltpu

D_MODEL = 1024
DEPTH = 4
CHUNK = 64
HEAD_DIM = 64
N_HEADS = 8
KV_HEADS = 2
GROUP = N_HEADS // KV_HEADS
WINDOW = 128
ATTN_WIDTH = N_HEADS * HEAD_DIM
KV_WIDTH = KV_HEADS * HEAD_DIM
CONV_DIM = 256
CONV_W = 3
MEM_HEADS = 4
MEM_WIDTH = MEM_HEADS * HEAD_DIM
N_MEM = 256
MIX_WIDTH = ATTN_WIDTH + CONV_DIM + MEM_WIDTH
IN_WIDTH = ATTN_WIDTH + 2 * KV_WIDTH + 3 * CONV_DIM + MEM_WIDTH
D_FF = 2816
EPS = 1e-6
ATTN_SCALE = HEAD_DIM ** -0.5
NEG = -1e30

_Q0, _K0, _V0 = 0, ATTN_WIDTH, ATTN_WIDTH + KV_WIDTH
_CB0 = ATTN_WIDTH + 2 * KV_WIDTH
_CC0 = _CB0 + CONV_DIM
_CX0 = _CC0 + CONV_DIM
_MQ0 = _CX0 + CONV_DIM

PAIR = 2 * CHUNK
TM_PROMPT = 512
G_SAMPLE = 8
FF_CHUNKS = (1024, 1024, 768)
FF_BLOCK = 256
VMEM_LIMIT_BYTES = 56 * 1024 * 1024

BF = jnp.bfloat16
F32 = jnp.float32


def _dot(a, b):
    return jnp.dot(a, b, preferred_element_type=F32)


def _dot_nt(a, b):
    return lax.dot_general(a, b, (((1,), (1,)), ((), ())), preferred_element_type=F32)


def _rms(x, g):
    ms = jnp.mean(x * x, axis=-1, keepdims=True)
    return x * lax.rsqrt(ms + EPS) * g


def _head_rms(z, g, bd):
    w = z.shape[-1]
    ms = _dot((z * z).astype(BF), bd[:w, :w])
    return z * lax.rsqrt(ms + EPS) * g


def _lane_iota(shape):
    return lax.broadcasted_iota(jnp.int32, shape, len(shape) - 1)


def _split_heads_lo_hi(z):
    lo = (_lane_iota(z.shape) % (2 * HEAD_DIM)) < HEAD_DIM
    return jnp.where(lo, z, 0.0).astype(BF), jnp.where(lo, 0.0, z).astype(BF)


def _stack_group_queries(q_lo, q_hi, r0, rows, j):
    c0 = j * GROUP * HEAD_DIM
    return jnp.concatenate(
        [q_lo[r0:r0 + rows, c0:c0 + 128], q_hi[r0:r0 + rows, c0:c0 + 128],
         q_lo[r0:r0 + rows, c0 + 128:c0 + 256], q_hi[r0:r0 + rows, c0 + 128:c0 + 256]], axis=0)


def _dup_halves(z):
    sw = pltpu.roll(z, HEAD_DIM, axis=1)
    lo = _lane_iota(z.shape) < HEAD_DIM
    return jnp.where(lo, z, sw), jnp.where(lo, sw, z)


def _pick_head_lanes(o_all, rows):
    grp = _lane_iota((rows, GROUP * HEAD_DIM)) // HEAD_DIM
    out = o_all[3 * rows:4 * rows]
    for h in (2, 1, 0):
        out = jnp.where(grp == h, o_all[h * rows:(h + 1) * rows], out)
    return out


def _softmax_rows(s, sink):
    m = jnp.max(s, axis=-1, keepdims=True)
    if sink is not None:
        m = jnp.maximum(m, sink)
    p = jnp.exp(s - m)
    den = jnp.sum(p, axis=-1, keepdims=True)
    if sink is not None:
        den = den + jnp.exp(sink - m)
    return p * (1.0 / den)


def _project_in(x, g_attn, w_in, bd, gq, gk, gmq):
    xn = _rms(x, g_attn).astype(BF)
    u = _dot(xn, w_in)
    half = ATTN_WIDTH // 2
    qn = jnp.concatenate(
        [_head_rms(u[:, _Q0:_Q0 + half], gq[:, :half], bd),
         _head_rms(u[:, _Q0 + half:_K0], gq[:, half:], bd)], axis=-1) * ATTN_SCALE
    kn = _head_rms(u[:, _K0:_V0], gk, bd)
    v = u[:, _V0:_CB0]
    cb = u[:, _CB0:_CC0]
    ccx = u[:, _CC0:_CX0] * u[:, _CX0:_MQ0]
    mqn = _head_rms(u[:, _MQ0:], gmq, bd) * ATTN_SCALE
    return qn, kn, v, cb, ccx, mqn


def _mem_attention(mq_blocks, mk_t, mv_t, rows):
    s = _dot(mq_blocks, mk_t)
    p = _softmax_rows(s, None).astype(BF)
    return _pick_head_lanes(_dot_nt(p, mv_t), rows)


def _mix_norm(a, cy, mo, g_out):
    return jnp.concatenate(
        [_rms(a, g_out[:, :ATTN_WIDTH]),
         _rms(cy, g_out[:, ATTN_WIDTH:ATTN_WIDTH + CONV_DIM]),
         _rms(mo, g_out[:, ATTN_WIDTH + CONV_DIM:])], axis=-1).astype(BF)


def _ffn_act(hn, w_gu_ref, c0, width):
    gate = _dot(hn, w_gu_ref[0, :, c0:c0 + width])
    up = _dot(hn, w_gu_ref[0, :, D_FF + c0:D_FF + c0 + width])
    return (gate * jax.nn.sigmoid(gate) * up).astype(BF)


def _ffn_chunk(hn, w_gu_ref, w_down_ref, c0, width):
    return _dot(_ffn_act(hn, w_gu_ref, c0, width), w_down_ref[0, c0:c0 + width, :])


def _out_ffn(x, mix, w_out_ref, g_ffn, w_gu_ref, w_down_ref):
    h = x + _dot(mix, w_out_ref[0])
    hn = _rms(h, g_ffn).astype(BF)
    y = h
    c0 = 0
    for width in FF_CHUNKS:
        y = y + _ffn_chunk(hn, w_gu_ref, w_down_ref, c0, width)
        c0 += width
    return y


def _conv_from_scratch(conv_scr, ccx, cb, conv_w, rows):
    sh2 = conv_scr[6:6 + rows, :]
    sh1 = conv_scr[7:7 + rows, :]
    return cb * (sh2 * conv_w[0:1, :] + sh1 * conv_w[1:2, :] + ccx * conv_w[2:3, :])


def _prompt_layer_kernel(layer_ref, sinks_ref, x_ref, mk_ref, mv_ref, g_attn_ref, w_in_ref, bd_ref,
                         gq_ref, gk_ref, gmq_ref, conv_w_ref, g_out_ref, w_out_ref, g_ffn_ref,
                         w_gu_ref, w_down_ref, nw_in_ref, nw_out_ref, nw_gu_ref, nw_down_ref,
                         y_ref, newk_ref, newv_ref, newc_ref,
                         nw_in_bf_ref, nw_out_bf_ref, nw_gu_bf_ref, nw_down_bf_ref,
                         kcar_scr, vcar_scr, ccar_scr, conv_scr, a_scr, mo_scr, mix_scr,
                         xres_scr, *,
                         tiles_per_seq):
    tm = TM_PROMPT
    layer = layer_ref[0]
    s = pl.program_id(0)
    n_tiles = pl.num_programs(0) - 1
    t = jnp.minimum(s, n_tiles - 1) % tiles_per_seq

    @pl.when(s == 0)
    def _():
        mix_scr[...] = jnp.zeros(mix_scr.shape, BF)
        xres_scr[...] = jnp.zeros(xres_scr.shape, F32)

    h_prev = xres_scr[...] + _dot(mix_scr[...], w_out_ref[0])
    hn_prev = _rms(h_prev, g_ffn_ref[0]).astype(BF)

    def back_units():
        blk = FF_BLOCK
        n_blk = D_FF // blk
        acts = []
        y = h_prev

        def down(c):
            lo, hi = 2 * c, min(2 * c + 2, n_blk)
            act = acts[lo] if hi - lo == 1 else jnp.concatenate(acts[lo:hi], axis=-1)
            return _dot(act, w_down_ref[0, lo * blk:hi * blk, :])

        for b in range(n_blk):
            gate = _dot(hn_prev, w_gu_ref[0, :, b * blk:(b + 1) * blk])
            yield
            up = _dot(hn_prev, w_gu_ref[0, :, D_FF + b * blk:D_FF + (b + 1) * blk])
            acts.append((gate * jax.nn.sigmoid(gate) * up).astype(BF))
            yield
            if b >= 2 and b % 2 == 0:
                y = y + down(b // 2 - 1)
                yield
        for c in range((n_blk - 1) // 2, (n_blk + 1) // 2):
            y = y + down(c)
            yield
        y_ref[0] = y

    back = back_units()

    started = t > 0
    k_prev = [jnp.where(started, kcar_scr[j], 0.0).astype(BF) for j in range(KV_HEADS)]
    v_prev = [jnp.where(started, vcar_scr[j], 0.0).astype(BF) for j in range(KV_HEADS)]
    conv_scr[0:8, :] = jnp.where(started, ccar_scr[...], 0.0)

    x = x_ref[0]
    qn, kn, v, cb, ccx, mqn = _project_in(x, g_attn_ref[0], w_in_ref[0], bd_ref[...], gq_ref[0],
                                          gk_ref[0], gmq_ref[0])

    k_dup = _dup_halves(kn)
    v_dup = _dup_halves(v)
    for j in range(KV_HEADS):
        kcar_scr[j] = k_dup[j][tm - WINDOW:tm, :]
        vcar_scr[j] = v_dup[j][tm - WINDOW:tm, :]
    k_rep = [k_dup[j].astype(BF) for j in range(KV_HEADS)]
    v_rep = [v_dup[j].astype(BF) for j in range(KV_HEADS)]

    q_lo, q_hi = _split_heads_lo_hi(qn)

    row_chunk = lax.broadcasted_iota(jnp.int32, (PAIR, 2 * PAIR), 0) // CHUNK
    col = _lane_iota((PAIR, 2 * PAIR))
    first_key = jnp.where(started, 0, WINDOW)
    visible = (col >= row_chunk * CHUNK) & (col < (row_chunk + 3) * CHUNK)
    bias_rest = jnp.where(visible, 0.0, NEG)
    bias_first = jnp.where(col >= first_key, bias_rest, NEG)

    grp = _lane_iota(mqn.shape) // HEAD_DIM
    mk_t = mk_ref[0, 0].astype(BF)
    mv_t = mv_ref[0, 0].astype(BF)
    g_out = g_out_ref[0]
    a_end, c_end = ATTN_WIDTH, ATTN_WIDTH + CONV_DIM

    def front_units():
        for p in range(tm // PAIR):
            r0 = p * PAIR
            bias = bias_first if p == 0 else bias_rest
            for j in range(KV_HEADS):
                c0 = j * GROUP * HEAD_DIM
                qf = _stack_group_queries(q_lo, q_hi, r0, PAIR, j)
                if p == 0:
                    k_win = jnp.concatenate([k_prev[j], k_rep[j][0:PAIR]], axis=0)
                    v_win = jnp.concatenate([v_prev[j], v_rep[j][0:PAIR]], axis=0)
                else:
                    k_win = k_rep[j][r0 - PAIR:r0 + PAIR]
                    v_win = v_rep[j][r0 - PAIR:r0 + PAIR]
                s = _dot_nt(qf, k_win)
                yield
                probs = []
                for h in range(GROUP):
                    sh = s[h * PAIR:(h + 1) * PAIR] + bias
                    probs.append(_softmax_rows(sh, sinks_ref[layer, j * GROUP + h]).astype(BF))
                    if h % 2 == 1:
                        yield
                o_all = _dot(jnp.concatenate(probs, axis=0),
                             jnp.concatenate([v_win, v_win], axis=-1))
                a_scr[r0:r0 + PAIR, c0:c0 + GROUP * HEAD_DIM] = _pick_head_lanes(o_all, PAIR)
                yield
        mix_scr[:, :a_end] = _rms(a_scr[...], g_out[:, :a_end]).astype(BF)
        yield
        for h in range(MEM_HEADS):
            mq_h = jnp.where(grp == h, mqn, 0.0).astype(BF)
            s = _dot(mq_h, mk_t)
            yield
            pm = _softmax_rows(s, None).astype(BF)
            yield
            o_h = _dot_nt(pm, mv_t)
            mo_scr[:, h * HEAD_DIM:(h + 1) * HEAD_DIM] = o_h[:, h * HEAD_DIM:(h + 1) * HEAD_DIM]
            yield

    xres_scr[...] = x

    conv_scr[8:8 + tm, :] = ccx
    cy = _conv_from_scratch(conv_scr, ccx, cb, conv_w_ref[0], tm)
    ccar_scr[...] = ccx[tm - 8:tm, :]
    mix_scr[:, a_end:c_end] = _rms(cy, g_out[:, a_end:c_end]).astype(BF)

    n_back = 2 * (D_FF // FF_BLOCK) + (D_FF // FF_BLOCK + 1) // 2
    n_front = (tm // PAIR) * KV_HEADS * 4 + 1 + MEM_HEADS * 3
    front = front_units()
    done_front = 0
    for i in range(n_back):
        next(back)
        want = ((i + 1) * n_front) // n_back
        for _ in range(want - done_front):
            next(front)
        done_front = want
    assert next(back, "end") == "end" and next(front, "end") == "end"
    mix_scr[:, c_end:] = _rms(mo_scr[...], g_out[:, c_end:]).astype(BF)

    @pl.when(t == tiles_per_seq - 1)
    def _():
        newk_ref[0] = kn[tm - WINDOW:tm, :].T
        newv_ref[0] = v[tm - WINDOW:tm, :].T
        newc_ref[0] = ccx[tm - (CONV_W - 1):tm, :]

    nw_in_bf_ref[...] = nw_in_ref[...].astype(BF)
    nw_out_bf_ref[...] = nw_out_ref[...].astype(BF)
    nw_gu_bf_ref[...] = nw_gu_ref[...].astype(BF)
    nw_down_bf_ref[...] = nw_down_ref[...].astype(BF)


def _layer_spec(shape):
    zeros = (0,) * len(shape)
    return pl.BlockSpec((1,) + shape, lambda *args: (args[-1][0],) + zeros,
                        pipeline_mode=pl.Buffered(1))


def _matrix_spec(shape):
    return pl.BlockSpec((1,) + shape, lambda *args: (0, 0, 0), pipeline_mode=pl.Buffered(1))


_MATRIX_SHAPES = dict(w_in=(D_MODEL, IN_WIDTH), w_out=(MIX_WIDTH, D_MODEL),
                      w_gu=(D_MODEL, 2 * D_FF), w_down=(D_FF, D_MODEL))
_MATRIX_ORDER = ("w_in", "w_out", "w_gu", "w_down")
_CAST_BLOCKS = dict(w_in=32, w_out=32, w_gu=32, w_down=16)


def _cast_specs(name, n_blk, layer_of, step_of):
    rows, cols = _MATRIX_SHAPES[name]
    assert rows % (16 * n_blk) == 0
    blk = (1, rows // n_blk, cols)
    src = pl.BlockSpec(blk, lambda *a: (layer_of(*a), jnp.minimum(step_of(*a), n_blk - 1), 0))
    dst = pl.BlockSpec(blk, lambda *a: (0, jnp.minimum(step_of(*a), n_blk - 1), 0))
    return src, dst, jax.ShapeDtypeStruct((1, rows, cols), BF)


def _weight_specs():
    return [
        _layer_spec((1, D_MODEL)),
        _matrix_spec(_MATRIX_SHAPES["w_in"]),
        pl.BlockSpec((256, 256), lambda *args: (0, 0), pipeline_mode=pl.Buffered(1)),
        _layer_spec((1, ATTN_WIDTH)),
        _layer_spec((1, KV_WIDTH)),
        _layer_spec((1, MEM_WIDTH)),
        _layer_spec((CONV_W, CONV_DIM)),
        _layer_spec((1, MIX_WIDTH)),
        _matrix_spec(_MATRIX_SHAPES["w_out"]),
        _layer_spec((1, D_MODEL)),
        _matrix_spec(_MATRIX_SHAPES["w_gu"]),
        _matrix_spec(_MATRIX_SHAPES["w_down"]),
    ]


_WEIGHT_ORDER = ("g_attn", "w_in", "bd", "gq", "gk", "gmq", "conv_w", "g_out", "w_out", "g_ffn",
                 "w_gu", "w_down")

_SMEM_SPEC = pl.BlockSpec(memory_space=pltpu.SMEM)


def _prompt_layer(layer, x, mem_k_t, mem_v_t, sinks, weights, f32_matrices):
    batch, seq, _ = x.shape
    tm = TM_PROMPT
    tps = seq // tm
    n_tiles = batch * tps

    def front(s):
        return jnp.minimum(s, n_tiles - 1)

    def back(s):
        return jnp.maximum(s - 1, 0)

    assert max(_CAST_BLOCKS.values()) <= n_tiles
    casts = [_cast_specs(n, _CAST_BLOCKS[n], lambda s, l: jnp.minimum(l[0] + 1, DEPTH - 1),
                         lambda s, l: s) for n in _MATRIX_ORDER]
    in_specs = [
        _SMEM_SPEC,
        pl.BlockSpec((1, tm, D_MODEL), lambda s, l: (front(s) // tps, front(s) % tps, 0)),
        pl.BlockSpec((1, 1, MEM_WIDTH, N_MEM), lambda s, l: (l[0], front(s) // tps, 0, 0)),
        pl.BlockSpec((1, 1, MEM_WIDTH, N_MEM), lambda s, l: (l[0], front(s) // tps, 0, 0)),
    ] + _weight_specs() + [c[0] for c in casts]
    out_shape = (
        jax.ShapeDtypeStruct((batch, seq, D_MODEL), F32),
        jax.ShapeDtypeStruct((batch, KV_WIDTH, WINDOW), F32),
        jax.ShapeDtypeStruct((batch, KV_WIDTH, WINDOW), F32),
        jax.ShapeDtypeStruct((batch, CONV_W - 1, CONV_DIM), F32),
    ) + tuple(c[2] for c in casts)
    out_specs = (
        pl.BlockSpec((1, tm, D_MODEL), lambda s, l: (back(s) // tps, back(s) % tps, 0)),
        pl.BlockSpec((1, KV_WIDTH, WINDOW), lambda s, l: (front(s) // tps, 0, 0)),
        pl.BlockSpec((1, KV_WIDTH, WINDOW), lambda s, l: (front(s) // tps, 0, 0)),
        pl.BlockSpec((1, CONV_W - 1, CONV_DIM), lambda s, l: (front(s) // tps, 0, 0)),
    ) + tuple(c[1] for c in casts)
    scratch = [
        pltpu.VMEM((KV_HEADS, WINDOW, 2 * HEAD_DIM), F32),
        pltpu.VMEM((KV_HEADS, WINDOW, 2 * HEAD_DIM), F32),
        pltpu.VMEM((8, CONV_DIM), F32),
        pltpu.VMEM((8 + tm, CONV_DIM), F32),
        pltpu.VMEM((tm, ATTN_WIDTH), F32),
        pltpu.VMEM((tm, MEM_WIDTH), F32),
        pltpu.VMEM((tm, MIX_WIDTH), BF),
        pltpu.VMEM((tm, D_MODEL), F32),
    ]
    return pl.pallas_call(
        functools.partial(_prompt_layer_kernel, tiles_per_seq=tps),
        grid_spec=pltpu.PrefetchScalarGridSpec(
            num_scalar_prefetch=1, grid=(n_tiles + 1,), in_specs=in_specs,
            out_specs=out_specs, scratch_shapes=scratch),
        out_shape=out_shape,
        compiler_params=pltpu.CompilerParams(
            dimension_semantics=("arbitrary",),
            vmem_limit_bytes=VMEM_LIMIT_BYTES),
        name="prompt_layer",
    )(layer, sinks, x, mem_k_t, mem_v_t, *[weights[n] for n in _WEIGHT_ORDER],
      *[f32_matrices[n] for n in _MATRIX_ORDER])


def _sample_layer_kernel(layer_ref, sinks_ref, x_ref, ck_ref, cv_ref, cc_ref, mk_ref, mv_ref,
                         g_attn_ref, w_in_ref, bd_ref, gq_ref, gk_ref, gmq_ref, conv_w_ref,
                         g_out_ref, w_out_ref, g_ffn_ref, w_gu_ref, w_down_ref, y_ref, newk_ref,
                         newv_ref, newc_ref, conv_scr, a_scr, cy_scr, mo_scr):
    rows = CHUNK
    layer = layer_ref[0]
    x = x_ref[...]
    qn, kn, v, cb, ccx, mqn = _project_in(x, g_attn_ref[0], w_in_ref[0], bd_ref[...], gq_ref[0],
                                          gk_ref[0], gmq_ref[0])
    q_lo, q_hi = _split_heads_lo_hi(qn)
    grp = _lane_iota(mqn.shape) // HEAD_DIM
    mq_heads = [jnp.where(grp == h, mqn, 0.0).astype(BF) for h in range(MEM_HEADS)]
    conv_w = conv_w_ref[0]
    old_lane = _lane_iota((KV_WIDTH, WINDOW)) < WINDOW - rows

    def shifted_window(cache_t, new_rows):
        new_t = jnp.concatenate([new_rows, new_rows], axis=0).T
        return jnp.where(old_lane, pltpu.roll(cache_t, WINDOW - rows, axis=1), new_t)

    for b in range(G_SAMPLE):
        r0 = b * rows
        ck_t = ck_ref[0, b]
        cv_t = cv_ref[0, b]
        kn_b = kn[r0:r0 + rows]
        v_b = v[r0:r0 + rows]
        newk_ref[b] = shifted_window(ck_t, kn_b)
        newv_ref[b] = shifted_window(cv_t, v_b)
        k_dup = _dup_halves(kn_b)
        v_dup = _dup_halves(v_b)
        ck_bf = ck_t.astype(BF)
        cv_bf = cv_t.astype(BF)
        for j in range(KV_HEADS):
            c0 = j * GROUP * HEAD_DIM
            qf = _stack_group_queries(q_lo, q_hi, r0, rows, j)
            kj_t = ck_bf[j * HEAD_DIM:(j + 1) * HEAD_DIM, :]
            vj_t = cv_bf[j * HEAD_DIM:(j + 1) * HEAD_DIM, :]
            s = jnp.concatenate(
                [_dot(qf, jnp.concatenate([kj_t, kj_t], axis=0)),
                 _dot_nt(qf, k_dup[j].astype(BF))], axis=-1)
            probs = [_softmax_rows(s[h * rows:(h + 1) * rows], sinks_ref[layer, j * GROUP + h])
                     for h in range(GROUP)]
            pm = jnp.concatenate(probs, axis=0).astype(BF)
            vb = v_dup[j].astype(BF)
            o_all = (_dot_nt(pm[:, :WINDOW], jnp.concatenate([vj_t] * GROUP, axis=0))
                     + _dot(pm[:, WINDOW:], jnp.concatenate([vb, vb], axis=-1)))
            a_scr[r0:r0 + rows, c0:c0 + GROUP * HEAD_DIM] = _pick_head_lanes(o_all, rows)

        ccx_b = ccx[r0:r0 + rows]
        conv_scr[6:8, :] = cc_ref[0, b]
        conv_scr[8:8 + rows, :] = ccx_b
        cy_scr[r0:r0 + rows, :] = _conv_from_scratch(conv_scr, ccx_b, cb[r0:r0 + rows], conv_w,
                                                     rows)
        newc_ref[b] = ccx_b[rows - (CONV_W - 1):rows, :]

        mq_blocks = jnp.concatenate([mq_heads[h][r0:r0 + rows] for h in range(MEM_HEADS)], axis=0)
        mo_scr[r0:r0 + rows, :] = _mem_attention(mq_blocks, mk_ref[0, b].astype(BF),
                                                 mv_ref[0, b].astype(BF), rows)

    mix = _mix_norm(a_scr[...], cy_scr[...], mo_scr[...], g_out_ref[0])
    y_ref[...] = _out_ffn(x, mix, w_out_ref, g_ffn_ref[0], w_gu_ref, w_down_ref)


def _sample_layer(layer, x, cache_k_t, cache_v_t, cache_conv, mem_k_t, mem_v_t, sinks, weights):
    n_b = cache_k_t.shape[1]
    g = G_SAMPLE
    tm = g * CHUNK
    in_specs = [
        _SMEM_SPEC,
        pl.BlockSpec((tm, D_MODEL), lambda i, l: (i, 0)),
        pl.BlockSpec((1, g, KV_WIDTH, WINDOW), lambda i, l: (l[0], i, 0, 0)),
        pl.BlockSpec((1, g, KV_WIDTH, WINDOW), lambda i, l: (l[0], i, 0, 0)),
        pl.BlockSpec((1, g, CONV_W - 1, CONV_DIM), lambda i, l: (l[0], i, 0, 0)),
        pl.BlockSpec((1, g, MEM_WIDTH, N_MEM), lambda i, l: (l[0], i, 0, 0)),
        pl.BlockSpec((1, g, MEM_WIDTH, N_MEM), lambda i, l: (l[0], i, 0, 0)),
    ] + _weight_specs()
    out_shape = (
        jax.ShapeDtypeStruct((n_b * CHUNK, D_MODEL), F32),
        jax.ShapeDtypeStruct((n_b, KV_WIDTH, WINDOW), F32),
        jax.ShapeDtypeStruct((n_b, KV_WIDTH, WINDOW), F32),
        jax.ShapeDtypeStruct((n_b, CONV_W - 1, CONV_DIM), F32),
    )
    out_specs = (
        pl.BlockSpec((tm, D_MODEL), lambda i, l: (i, 0)),
        pl.BlockSpec((g, KV_WIDTH, WINDOW), lambda i, l: (i, 0, 0)),
        pl.BlockSpec((g, KV_WIDTH, WINDOW), lambda i, l: (i, 0, 0)),
        pl.BlockSpec((g, CONV_W - 1, CONV_DIM), lambda i, l: (i, 0, 0)),
    )
    scratch = [
        pltpu.VMEM((8 + CHUNK, CONV_DIM), F32),
        pltpu.VMEM((tm, ATTN_WIDTH), F32),
        pltpu.VMEM((tm, CONV_DIM), F32),
        pltpu.VMEM((tm, MEM_WIDTH), F32),
    ]
    return pl.pallas_call(
        _sample_layer_kernel,
        grid_spec=pltpu.PrefetchScalarGridSpec(
            num_scalar_prefetch=1, grid=(n_b // g,), in_specs=in_specs, out_specs=out_specs,
            scratch_shapes=scratch),
        out_shape=out_shape,
        compiler_params=pltpu.CompilerParams(
            dimension_semantics=("arbitrary",),
            vmem_limit_bytes=VMEM_LIMIT_BYTES),
        name="sample_layer",
    )(layer, sinks, x, cache_k_t, cache_v_t, cache_conv, mem_k_t, mem_v_t,
      *[weights[n] for n in _WEIGHT_ORDER])


def _mem_kv_kernel(mem_ref, g_mem_ref, w_ref, bd_ref, gmk_ref, w0_in_ref, w0_out_ref, w0_gu_ref,
                   w0_down_ref, mk_ref, mv_ref, w0_in_bf_ref, w0_out_bf_ref, w0_gu_bf_ref,
                   w0_down_bf_ref):
    w = w_ref[0].astype(BF)
    for b in range(mem_ref.shape[0]):
        xn = _rms(mem_ref[b], g_mem_ref[0]).astype(BF)
        kv = _dot(xn, w)
        mk_ref[0, b] = _head_rms(kv[:, :MEM_WIDTH], gmk_ref[0], bd_ref[...]).T
        mv_ref[0, b] = kv[:, MEM_WIDTH:].T
    w0_in_bf_ref[...] = w0_in_ref[...].astype(BF)
    w0_out_bf_ref[...] = w0_out_ref[...].astype(BF)
    w0_gu_bf_ref[...] = w0_gu_ref[...].astype(BF)
    w0_down_bf_ref[...] = w0_down_ref[...].astype(BF)


def _mem_kv(mem, g_mem, w_mem_kv, bd, gmk, f32_matrices):
    batch = mem.shape[0]
    out = jax.ShapeDtypeStruct((DEPTH, batch, MEM_WIDTH, N_MEM), F32)
    casts = [_cast_specs(n, DEPTH, lambda l: 0, lambda l: l) for n in _MATRIX_ORDER]
    return pl.pallas_call(
        _mem_kv_kernel,
        grid=(DEPTH,),
        in_specs=[
            pl.BlockSpec((batch, N_MEM, D_MODEL), lambda l: (0, 0, 0),
                         pipeline_mode=pl.Buffered(1)),
            pl.BlockSpec((1, 1, D_MODEL), lambda l: (l, 0, 0)),
            pl.BlockSpec((1, D_MODEL, 2 * MEM_WIDTH), lambda l: (l, 0, 0)),
            pl.BlockSpec((256, 256), lambda l: (0, 0)),
            pl.BlockSpec((1, 1, MEM_WIDTH), lambda l: (l, 0, 0)),
        ] + [c[0] for c in casts],
        out_specs=(pl.BlockSpec((1, batch, MEM_WIDTH, N_MEM), lambda l: (l, 0, 0, 0)),
                   pl.BlockSpec((1, batch, MEM_WIDTH, N_MEM), lambda l: (l, 0, 0, 0)))
        + tuple(c[1] for c in casts),
        out_shape=(out, out) + tuple(c[2] for c in casts),
        compiler_params=pltpu.CompilerParams(
            dimension_semantics=("arbitrary",),
            vmem_limit_bytes=VMEM_LIMIT_BYTES),
        name="prompt_mem_kv",
    )(mem, g_mem, w_mem_kv, bd, gmk, *[f32_matrices[n] for n in _MATRIX_ORDER])


def _tile_heads(g, n):
    return jnp.tile(g, (1, n)).reshape(g.shape[0], 1, n * g.shape[1])


def _feature_major(c):
    lead = c.shape[:-3]
    n_tok, heads, dim = c.shape[-3:]
    perm = tuple(range(len(lead))) + (len(lead) + 1, len(lead) + 2, len(lead))
    return jnp.transpose(c, perm).reshape(lead + (heads * dim, n_tok))


def _token_major(c_t, heads):
    lead = c_t.shape[:-2]
    n_tok = c_t.shape[-1]
    c = c_t.reshape(lead + (heads, HEAD_DIM, n_tok))
    perm = tuple(range(len(lead))) + (len(lead) + 2, len(lead), len(lead) + 1)
    return jnp.transpose(c, perm)


def kernel(x_prompt, x_sample, mem_prompt, cache_win_k, cache_win_v, cache_conv, cache_mem_k,
           cache_mem_v, attn_norm_g, w_in, q_norm_g, k_norm_g, sinks, conv_w, mem_norm_g,
           w_mem_kv, mq_norm_g, mk_norm_g, out_norm_g, w_out, ffn_norm_g, w_gate_up, w_down):
    batch, seq, _ = x_prompt.shape
    dec_batch, dec_seq, _ = x_sample.shape
    assert dec_seq == CHUNK and seq % TM_PROMPT == 0 and dec_batch % G_SAMPLE == 0

    head = jnp.arange(256) // HEAD_DIM
    bd = jnp.where(head[:, None] == head[None, :], 1.0 / HEAD_DIM, 0.0).astype(BF)

    f32_matrices = dict(w_in=w_in, w_out=w_out, w_gu=w_gate_up, w_down=w_down)
    small = dict(
        g_attn=attn_norm_g.reshape(DEPTH, 1, D_MODEL), bd=bd,
        gq=_tile_heads(q_norm_g, N_HEADS), gk=_tile_heads(k_norm_g, KV_HEADS),
        gmq=_tile_heads(mq_norm_g, MEM_HEADS), conv_w=conv_w,
        g_out=out_norm_g.reshape(DEPTH, 1, MIX_WIDTH),
        g_ffn=ffn_norm_g.reshape(DEPTH, 1, D_MODEL))

    mk_t, mv_t, *first = _mem_kv(mem_prompt, mem_norm_g.reshape(DEPTH, 1, D_MODEL), w_mem_kv, bd,
                                 _tile_heads(mk_norm_g, MEM_HEADS), f32_matrices)
    matrices = dict(zip(_MATRIX_ORDER, first))

    ck_t = _feature_major(cache_win_k)
    cv_t = _feature_major(cache_win_v)
    cmk_t = _feature_major(cache_mem_k)
    cmv_t = _feature_major(cache_mem_v)

    yp = x_prompt
    ys = x_sample.reshape(dec_batch * dec_seq, D_MODEL)
    wk_p, wv_p, cv_p, wk_s, wv_s, cv_s = [], [], [], [], [], []
    for l in range(DEPTH):
        layer = jnp.full((1,), l, jnp.int32)
        weights = dict(small, **matrices)
        yp, k_p, v_p, c_p, *next_matrices = _prompt_layer(layer, yp, mk_t, mv_t, sinks, weights,
                                                          f32_matrices)
        wk_p.append(k_p); wv_p.append(v_p); cv_p.append(c_p)
        ys, k_s, v_s, c_s = _sample_layer(layer, ys, ck_t, cv_t, cache_conv, cmk_t, cmv_t, sinks,
                                          weights)
        wk_s.append(k_s); wv_s.append(v_s); cv_s.append(c_s)
        matrices = dict(zip(_MATRIX_ORDER, next_matrices))

    return (yp, ys.reshape(dec_batch, dec_seq, D_MODEL),
            _token_major(jnp.stack(wk_p), KV_HEADS), _token_major(jnp.stack(wv_p), KV_HEADS),
            jnp.stack(cv_p),
            _token_major(mk_t, MEM_HEADS), _token_major(mv_t, MEM_HEADS),
            _token_major(jnp.stack(wk_s), KV_HEADS), _token_major(jnp.stack(wv_s), KV_HEADS),
            jnp.stack(cv_s))
```

```python
import functools

import jax
import jax.numpy as jnp
from jax import lax
from jax.experimental import pallas as pl
from jax.experimental.pallas import tpu as pltpu

D_MODEL = 1024
DEPTH = 4
CHUNK = 64
HEAD_DIM = 64
N_HEADS = 8
KV_HEADS = 2
GROUP = N_HEADS // KV_HEADS
WINDOW = 128
ATTN_WIDTH = N_HEADS * HEAD_DIM
KV_WIDTH = KV_HEADS * HEAD_DIM
CONV_DIM = 256
CONV_W = 3
MEM_HEADS = 4
MEM_WIDTH = MEM_HEADS * HEAD_DIM
N_MEM = 256
MIX_WIDTH = ATTN_WIDTH + CONV_DIM + MEM_WIDTH
IN_WIDTH = ATTN_WIDTH + 2 * KV_WIDTH + 3 * CONV_DIM + MEM_WIDTH
D_FF = 2816
EPS = 1e-6
ATTN_SCALE = HEAD_DIM ** -0.5
NEG = -1e30

_Q0, _K0, _V0 = 0, ATTN_WIDTH, ATTN_WIDTH + KV_WIDTH
_CB0 = ATTN_WIDTH + 2 * KV_WIDTH
_CC0 = _CB0 + CONV_DIM
_CX0 = _CC0 + CONV_DIM
_MQ0 = _CX0 + CONV_DIM

PAIR = 2 * CHUNK
TILE = 512
G_SAMPLE = TILE // CHUNK
SAMPLE_LOCKSTEP = 2
FF_BLOCK = 256
VMEM_LIMIT_BYTES = 61 * 1024 * 1024

BF = jnp.bfloat16
F32 = jnp.float32


def _dot(a, b):
    return jnp.dot(a, b, preferred_element_type=F32)


def _dot_nt(a, b):
    return lax.dot_general(a, b, (((1,), (1,)), ((), ())), preferred_element_type=F32)


def _rms(x, g):
    ms = jnp.mean(x * x, axis=-1, keepdims=True)
    return x * lax.rsqrt(ms + EPS) * g


def _head_rms(z, g, bd):
    w = z.shape[-1]
    ms = _dot((z * z).astype(BF), bd[:w, :w])
    return z * lax.rsqrt(ms + EPS) * g


def _lane_iota(shape):
    return lax.broadcasted_iota(jnp.int32, shape, len(shape) - 1)


def _split_heads_lo_hi(z):
    lo = (_lane_iota(z.shape) % (2 * HEAD_DIM)) < HEAD_DIM
    return jnp.where(lo, z, 0.0).astype(BF), jnp.where(lo, 0.0, z).astype(BF)


def _stack_group_queries(q_lo, q_hi, r0, rows, j):
    c0 = j * GROUP * HEAD_DIM
    return jnp.concatenate(
        [q_lo[r0:r0 + rows, c0:c0 + 128], q_hi[r0:r0 + rows, c0:c0 + 128],
         q_lo[r0:r0 + rows, c0 + 128:c0 + 256], q_hi[r0:r0 + rows, c0 + 128:c0 + 256]], axis=0)


def _dup_halves(z):
    sw = pltpu.roll(z, HEAD_DIM, axis=1)
    lo = _lane_iota(z.shape) < HEAD_DIM
    return jnp.where(lo, z, sw), jnp.where(lo, sw, z)


def _pick_head_lanes(o_all, rows):
    grp = _lane_iota((rows, GROUP * HEAD_DIM)) // HEAD_DIM
    out = o_all[3 * rows:4 * rows]
    for h in (2, 1, 0):
        out = jnp.where(grp == h, o_all[h * rows:(h + 1) * rows], out)
    return out


def _softmax_rows(s, sink):
    m = jnp.max(s, axis=-1, keepdims=True)
    if sink is not None:
        m = jnp.maximum(m, sink)
    p = jnp.exp(s - m)
    den = jnp.sum(p, axis=-1, keepdims=True)
    if sink is not None:
        den = den + jnp.exp(sink - m)
    return p * (1.0 / den)


def _project_in(x, g_attn, w_in, bd, gq, gk, gmq):
    xn = _rms(x, g_attn).astype(BF)
    u = _dot(xn, w_in)
    half = ATTN_WIDTH // 2
    qn = jnp.concatenate(
        [_head_rms(u[:, _Q0:_Q0 + half], gq[:, :half], bd),
         _head_rms(u[:, _Q0 + half:_K0], gq[:, half:], bd)], axis=-1) * ATTN_SCALE
    kn = _head_rms(u[:, _K0:_V0], gk, bd)
    v = u[:, _V0:_CB0]
    cb = u[:, _CB0:_CC0]
    ccx = u[:, _CC0:_CX0] * u[:, _CX0:_MQ0]
    mqn = _head_rms(u[:, _MQ0:], gmq, bd) * ATTN_SCALE
    return qn, kn, v, cb, ccx, mqn


def _conv_from_scratch(conv_scr, ccx, cb, conv_w, rows):
    sh2 = conv_scr[6:6 + rows, :]
    sh1 = conv_scr[7:7 + rows, :]
    return cb * (sh2 * conv_w[0:1, :] + sh1 * conv_w[1:2, :] + ccx * conv_w[2:3, :])


_A_END = ATTN_WIDTH
_C_END = ATTN_WIDTH + CONV_DIM

_N_BACK_UNITS = 2 * (D_FF // FF_BLOCK) + (D_FF // FF_BLOCK + 1) // 2
_N_PROMPT_UNITS = (TILE // PAIR) * KV_HEADS * 4 + 1 + MEM_HEADS * 3
_N_SAMPLE_UNITS = (G_SAMPLE // SAMPLE_LOCKSTEP) * (SAMPLE_LOCKSTEP * (1 + KV_HEADS * 4 + 3) + 1)


def _round_robin(chains):
    chains = list(chains)
    while chains:
        for chain in list(chains):
            try:
                next(chain)
            except StopIteration:
                chains.remove(chain)
            else:
                yield


def _layer_kernel(layer_ref, sinks_ref,
                  xp_ref, mkp_ref, mvp_ref,
                  xs_ref, ck_ref, cv_ref, cc_ref, mks_ref, mvs_ref,
                  g_attn_ref, w_in_ref, bd_ref, gq_ref, gk_ref, gmq_ref, conv_w_ref, g_out_ref,
                  w_out_ref, g_ffn_ref, w_gu_ref, w_down_ref,
                  nw_in_ref, nw_out_ref, nw_gu_ref, nw_down_ref,
                  yp_ref, newkp_ref, newvp_ref, newcp_ref,
                  ys_ref, newks_ref, newvs_ref, newcs_ref,
                  nw_in_bf_ref, nw_out_bf_ref, nw_gu_bf_ref, nw_down_bf_ref,
                  kcar_scr, vcar_scr, ccar_scr, conv_scr, a_scr, mo_scr, mix_scr, xres_scr, *,
                  tiles_per_seq, n_prompt_tiles):
    layer = layer_ref[0]
    s = pl.program_id(0)
    n_tiles = pl.num_programs(0) - 1
    f = jnp.minimum(s, n_tiles - 1)
    g_out = g_out_ref[0]
    bd = bd_ref[...]

    @pl.when(s == 0)
    def _():
        mix_scr[...] = jnp.zeros(mix_scr.shape, BF)
        xres_scr[...] = jnp.zeros(xres_scr.shape, F32)

    def back_units(result):
        h_prev = xres_scr[...] + _dot(mix_scr[...], w_out_ref[0])
        hn_prev = _rms(h_prev, g_ffn_ref[0]).astype(BF)
        yield
        blk = FF_BLOCK
        n_blk = D_FF // blk
        acts = []
        y = h_prev

        def down(c):
            lo, hi = 2 * c, min(2 * c + 2, n_blk)
            act = acts[lo] if hi - lo == 1 else jnp.concatenate(acts[lo:hi], axis=-1)
            return _dot(act, w_down_ref[0, lo * blk:hi * blk, :])

        for b in range(n_blk):
            gate = _dot(hn_prev, w_gu_ref[0, :, b * blk:(b + 1) * blk])
            yield
            up = _dot(hn_prev, w_gu_ref[0, :, D_FF + b * blk:D_FF + (b + 1) * blk])
            acts.append((gate * jax.nn.sigmoid(gate) * up).astype(BF))
            yield
            if b >= 2 and b % 2 == 0:
                y = y + down(b // 2 - 1)
                yield
        for c in range((n_blk - 1) // 2, (n_blk + 1) // 2):
            y = y + down(c)
            yield
        result.append(y)

    def prompt_units():
        tm = TILE
        t = f % tiles_per_seq
        started = t > 0
        k_prev = [jnp.where(started, kcar_scr[j], 0.0).astype(BF) for j in range(KV_HEADS)]
        v_prev = [jnp.where(started, vcar_scr[j], 0.0).astype(BF) for j in range(KV_HEADS)]
        conv_scr[0:8, :] = jnp.where(started, ccar_scr[...], 0.0)

        x = xp_ref[0]
        qn, kn, v, cb, ccx, mqn = _project_in(x, g_attn_ref[0], w_in_ref[0], bd, gq_ref[0],
                                              gk_ref[0], gmq_ref[0])
        xres_scr[...] = x

        k_dup = _dup_halves(kn)
        v_dup = _dup_halves(v)
        for j in range(KV_HEADS):
            kcar_scr[j] = k_dup[j][tm - WINDOW:tm, :]
            vcar_scr[j] = v_dup[j][tm - WINDOW:tm, :]
        k_rep = [k_dup[j].astype(BF) for j in range(KV_HEADS)]
        v_rep = [v_dup[j].astype(BF) for j in range(KV_HEADS)]
        q_lo, q_hi = _split_heads_lo_hi(qn)

        row_chunk = lax.broadcasted_iota(jnp.int32, (PAIR, 2 * PAIR), 0) // CHUNK
        col = _lane_iota((PAIR, 2 * PAIR))
        first_key = jnp.where(started, 0, WINDOW)
        visible = (col >= row_chunk * CHUNK) & (col < (row_chunk + 3) * CHUNK)
        bias_rest = jnp.where(visible, 0.0, NEG)
        bias_first = jnp.where(col >= first_key, bias_rest, NEG)

        conv_scr[8:8 + tm, :] = ccx
        cy = _conv_from_scratch(conv_scr, ccx, cb, conv_w_ref[0], tm)
        ccar_scr[...] = ccx[tm - 8:tm, :]
        mix_scr[:, _A_END:_C_END] = _rms(cy, g_out[:, _A_END:_C_END]).astype(BF)
        yield

        for p in range(tm // PAIR):
            r0 = p * PAIR
            bias = bias_first if p == 0 else bias_rest
            for j in range(KV_HEADS):
                c0 = j * GROUP * HEAD_DIM
                qf = _stack_group_queries(q_lo, q_hi, r0, PAIR, j)
                if p == 0:
                    k_win = jnp.concatenate([k_prev[j], k_rep[j][0:PAIR]], axis=0)
                    v_win = jnp.concatenate([v_prev[j], v_rep[j][0:PAIR]], axis=0)
                else:
                    k_win = k_rep[j][r0 - PAIR:r0 + PAIR]
                    v_win = v_rep[j][r0 - PAIR:r0 + PAIR]
                sc = _dot_nt(qf, k_win)
                yield
                probs = []
                for h in range(GROUP):
                    sh = sc[h * PAIR:(h + 1) * PAIR] + bias
                    probs.append(_softmax_rows(sh, sinks_ref[layer, j * GROUP + h]).astype(BF))
                    if h % 2 == 1:
                        yield
                o_all = _dot(jnp.concatenate(probs, axis=0),
                             jnp.concatenate([v_win, v_win], axis=-1))
                a_scr[r0:r0 + PAIR, c0:c0 + GROUP * HEAD_DIM] = _pick_head_lanes(o_all, PAIR)
                yield
        mix_scr[:, :_A_END] = _rms(a_scr[...], g_out[:, :_A_END]).astype(BF)
        yield

        grp = _lane_iota(mqn.shape) // HEAD_DIM
        mk_t = mkp_ref[0, 0].astype(BF)
        mv_t = mvp_ref[0, 0].astype(BF)
        for h in range(MEM_HEADS):
            mq_h = jnp.where(grp == h, mqn, 0.0).astype(BF)
            sc = _dot(mq_h, mk_t)
            yield
            pm = _softmax_rows(sc, None).astype(BF)
            yield
            o_h = _dot_nt(pm, mv_t)
            mo_scr[:, h * HEAD_DIM:(h + 1) * HEAD_DIM] = o_h[:, h * HEAD_DIM:(h + 1) * HEAD_DIM]
            if h == MEM_HEADS - 1:
                mix_scr[:, _C_END:] = _rms(mo_scr[...], g_out[:, _C_END:]).astype(BF)
            yield

        @pl.when(t == tiles_per_seq - 1)
        def _():
            newkp_ref[0] = kn[tm - WINDOW:tm, :].T
            newvp_ref[0] = v[tm - WINDOW:tm, :].T
            newcp_ref[0] = ccx[tm - (CONV_W - 1):tm, :]

    def sample_units():
        rows = CHUNK
        x = xs_ref[...]
        qn, kn, v, cb, ccx, mqn = _project_in(x, g_attn_ref[0], w_in_ref[0], bd, gq_ref[0],
                                              gk_ref[0], gmq_ref[0])
        xres_scr[...] = x
        q_lo, q_hi = _split_heads_lo_hi(qn)
        grp = _lane_iota(mqn.shape) // HEAD_DIM
        mq_heads = [jnp.where(grp == h, mqn, 0.0).astype(BF) for h in range(MEM_HEADS)]
        conv_w = conv_w_ref[0]
        old_lane = _lane_iota((KV_WIDTH, WINDOW)) < WINDOW - rows

        def shifted_window(cache_t, new_rows):
            new_t = jnp.concatenate([new_rows, new_rows], axis=0).T
            return jnp.where(old_lane, pltpu.roll(cache_t, WINDOW - rows, axis=1), new_t)
        yield

        def window_chain(b, j, ck_bf, cv_bf, k_dup_j, v_dup_j):
            r0 = b * rows
            c0 = j * GROUP * HEAD_DIM
            qf = _stack_group_queries(q_lo, q_hi, r0, rows, j)
            kj_t = ck_bf[j * HEAD_DIM:(j + 1) * HEAD_DIM, :]
            vj_t = cv_bf[j * HEAD_DIM:(j + 1) * HEAD_DIM, :]
            sc = jnp.concatenate(
                [_dot(qf, jnp.concatenate([kj_t, kj_t], axis=0)),
                 _dot_nt(qf, k_dup_j.astype(BF))], axis=-1)
            yield
            probs = []
            for h in range(GROUP):
                probs.append(_softmax_rows(sc[h * rows:(h + 1) * rows],
                                           sinks_ref[layer, j * GROUP + h]).astype(BF))
                if h % 2 == 1:
                    yield
            pm = jnp.concatenate(probs, axis=0)
            vb = v_dup_j.astype(BF)
            o_all = (_dot_nt(pm[:, :WINDOW], jnp.concatenate([vj_t] * GROUP, axis=0))
                     + _dot(pm[:, WINDOW:], jnp.concatenate([vb, vb], axis=-1)))
            a_scr[r0:r0 + rows, c0:c0 + GROUP * HEAD_DIM] = _pick_head_lanes(o_all, rows)
            yield

        def memory_chain(b):
            r0 = b * rows
            mq_blocks = jnp.concatenate([mq_heads[h][r0:r0 + rows] for h in range(MEM_HEADS)],
                                        axis=0)
            sc = _dot(mq_blocks, mks_ref[0, b].astype(BF))
            yield
            pm = _softmax_rows(sc, None).astype(BF)
            yield
            mo_b = _pick_head_lanes(_dot_nt(pm, mvs_ref[0, b].astype(BF)), rows)
            mix_scr[r0:r0 + rows, _C_END:] = _rms(mo_b, g_out[:, _C_END:]).astype(BF)
            yield

        for b0 in range(0, G_SAMPLE, SAMPLE_LOCKSTEP):
            chains = []
            for b in range(b0, b0 + SAMPLE_LOCKSTEP):
                r0 = b * rows
                ck_t = ck_ref[0, b]
                cv_t = cv_ref[0, b]
                kn_b = kn[r0:r0 + rows]
                v_b = v[r0:r0 + rows]
                newks_ref[b] = shifted_window(ck_t, kn_b)
                newvs_ref[b] = shifted_window(cv_t, v_b)
                k_dup = _dup_halves(kn_b)
                v_dup = _dup_halves(v_b)
                ck_bf = ck_t.astype(BF)
                cv_bf = cv_t.astype(BF)

                ccx_b = ccx[r0:r0 + rows]
                conv_scr[6:8, :] = cc_ref[0, b]
                conv_scr[8:8 + rows, :] = ccx_b
                cy_b = _conv_from_scratch(conv_scr, ccx_b, cb[r0:r0 + rows], conv_w, rows)
                mix_scr[r0:r0 + rows, _A_END:_C_END] = _rms(cy_b,
                                                            g_out[:, _A_END:_C_END]).astype(BF)
                newcs_ref[b] = ccx_b[rows - (CONV_W - 1):rows, :]
                yield
                chains += [window_chain(b, j, ck_bf, cv_bf, k_dup[j], v_dup[j])
                           for j in range(KV_HEADS)]
                chains.append(memory_chain(b))
            yield from _round_robin(chains)
            r0, r1 = b0 * rows, (b0 + SAMPLE_LOCKSTEP) * rows
            mix_scr[r0:r1, :_A_END] = _rms(a_scr[r0:r1, :], g_out[:, :_A_END]).astype(BF)
            yield

    def merged(front, n_front):
        result = []
        back = back_units(result)
        next(back)
        next(front)
        done_front = 0
        for i in range(_N_BACK_UNITS):
            next(back)
            want = ((i + 1) * n_front) // _N_BACK_UNITS
            for _ in range(want - done_front):
                next(front)
            done_front = want
        assert next(back, "end") == "end" and next(front, "end") == "end"
        return result[0]

    @pl.when(f < n_prompt_tiles)
    def _():
        yp_ref[0] = merged(prompt_units(), _N_PROMPT_UNITS)

    @pl.when(f >= n_prompt_tiles)
    def _():
        y = merged(sample_units(), _N_SAMPLE_UNITS)

        @pl.when(s == n_prompt_tiles)
        def _():
            yp_ref[0] = y

        @pl.when(s > n_prompt_tiles)
        def _():
            ys_ref[...] = y

    nw_in_bf_ref[...] = nw_in_ref[...].astype(BF)
    nw_out_bf_ref[...] = nw_out_ref[...].astype(BF)
    nw_gu_bf_ref[...] = nw_gu_ref[...].astype(BF)
    nw_down_bf_ref[...] = nw_down_ref[...].astype(BF)


def _layer_spec(shape):
    zeros = (0,) * len(shape)
    return pl.BlockSpec((1,) + shape, lambda *args: (args[-1][0],) + zeros,
                        pipeline_mode=pl.Buffered(1))


def _matrix_spec(shape):
    return pl.BlockSpec((1,) + shape, lambda *args: (0, 0, 0), pipeline_mode=pl.Buffered(1))


_MATRIX_SHAPES = dict(w_in=(D_MODEL, IN_WIDTH), w_out=(MIX_WIDTH, D_MODEL),
                      w_gu=(D_MODEL, 2 * D_FF), w_down=(D_FF, D_MODEL))
_MATRIX_ORDER = ("w_in", "w_out", "w_gu", "w_down")
_CAST_BLOCKS = dict(w_in=32, w_out=32, w_gu=32, w_down=22)


def _cast_specs(name, n_blk, layer_of, step_of):
    rows, cols = _MATRIX_SHAPES[name]
    assert rows % (16 * n_blk) == 0
    blk = (1, rows // n_blk, cols)
    src = pl.BlockSpec(blk, lambda *a: (layer_of(*a), jnp.minimum(step_of(*a), n_blk - 1), 0))
    dst = pl.BlockSpec(blk, lambda *a: (0, jnp.minimum(step_of(*a), n_blk - 1), 0))
    return src, dst, jax.ShapeDtypeStruct((1, rows, cols), BF)


def _weight_specs():
    return [
        _layer_spec((1, D_MODEL)),
        _matrix_spec(_MATRIX_SHAPES["w_in"]),
        pl.BlockSpec((256, 256), lambda *args: (0, 0), pipeline_mode=pl.Buffered(1)),
        _layer_spec((1, ATTN_WIDTH)),
        _layer_spec((1, KV_WIDTH)),
        _layer_spec((1, MEM_WIDTH)),
        _layer_spec((CONV_W, CONV_DIM)),
        _layer_spec((1, MIX_WIDTH)),
        _matrix_spec(_MATRIX_SHAPES["w_out"]),
        _layer_spec((1, D_MODEL)),
        _matrix_spec(_MATRIX_SHAPES["w_gu"]),
        _matrix_spec(_MATRIX_SHAPES["w_down"]),
    ]


_WEIGHT_ORDER = ("g_attn", "w_in", "bd", "gq", "gk", "gmq", "conv_w", "g_out", "w_out", "g_ffn",
                 "w_gu", "w_down")

_SMEM_SPEC = pl.BlockSpec(memory_space=pltpu.SMEM)


def _layer(layer, xp, mem_k_t, mem_v_t, xs, cache_k_t, cache_v_t, cache_conv, cmem_k_t, cmem_v_t,
           sinks, weights, f32_matrices):
    batch, seq, _ = xp.shape
    n_b = cache_k_t.shape[1]
    g = G_SAMPLE
    tps = seq // TILE
    n_pt = batch * tps
    n_tiles = n_pt + n_b // g

    def front(s):
        return jnp.minimum(s, n_tiles - 1)

    def back(s):
        return jnp.maximum(s - 1, 0)

    def ptile(tile):
        return jnp.minimum(tile, n_pt - 1)

    def stile(tile):
        return jnp.maximum(tile - n_pt, 0)

    def held_once(block, index_map):
        return pl.BlockSpec(block, index_map, pipeline_mode=pl.Buffered(1))

    assert max(_CAST_BLOCKS.values()) <= n_tiles
    casts = [_cast_specs(n, _CAST_BLOCKS[n], lambda s, l: jnp.minimum(l[0] + 1, DEPTH - 1),
                         lambda s, l: s) for n in _MATRIX_ORDER]
    in_specs = [
        _SMEM_SPEC,
        pl.BlockSpec((1, TILE, D_MODEL),
                     lambda s, l: (ptile(front(s)) // tps, ptile(front(s)) % tps, 0)),
        held_once((1, 1, MEM_WIDTH, N_MEM), lambda s, l: (l[0], ptile(front(s)) // tps, 0, 0)),
        held_once((1, 1, MEM_WIDTH, N_MEM), lambda s, l: (l[0], ptile(front(s)) // tps, 0, 0)),
        held_once((TILE, D_MODEL), lambda s, l: (stile(front(s)), 0)),
        held_once((1, g, KV_WIDTH, WINDOW), lambda s, l: (l[0], stile(front(s)), 0, 0)),
        held_once((1, g, KV_WIDTH, WINDOW), lambda s, l: (l[0], stile(front(s)), 0, 0)),
        held_once((1, g, CONV_W - 1, CONV_DIM), lambda s, l: (l[0], stile(front(s)), 0, 0)),
        held_once((1, g, MEM_WIDTH, N_MEM), lambda s, l: (l[0], stile(front(s)), 0, 0)),
        held_once((1, g, MEM_WIDTH, N_MEM), lambda s, l: (l[0], stile(front(s)), 0, 0)),
    ] + _weight_specs() + [c[0] for c in casts]
    out_shape = (
        jax.ShapeDtypeStruct((batch, seq, D_MODEL), F32),
        jax.ShapeDtypeStruct((batch, KV_WIDTH, WINDOW), F32),
        jax.ShapeDtypeStruct((batch, KV_WIDTH, WINDOW), F32),
        jax.ShapeDtypeStruct((batch, CONV_W - 1, CONV_DIM), F32),
        jax.ShapeDtypeStruct((n_b * CHUNK, D_MODEL), F32),
        jax.ShapeDtypeStruct((n_b, KV_WIDTH, WINDOW), F32),
        jax.ShapeDtypeStruct((n_b, KV_WIDTH, WINDOW), F32),
        jax.ShapeDtypeStruct((n_b, CONV_W - 1, CONV_DIM), F32),
    ) + tuple(c[2] for c in casts)
    out_specs = (
        pl.BlockSpec((1, TILE, D_MODEL),
                     lambda s, l: (ptile(back(s)) // tps, ptile(back(s)) % tps, 0)),
        pl.BlockSpec((1, KV_WIDTH, WINDOW), lambda s, l: (ptile(front(s)) // tps, 0, 0)),
        pl.BlockSpec((1, KV_WIDTH, WINDOW), lambda s, l: (ptile(front(s)) // tps, 0, 0)),
        pl.BlockSpec((1, CONV_W - 1, CONV_DIM), lambda s, l: (ptile(front(s)) // tps, 0, 0)),
        held_once((TILE, D_MODEL), lambda s, l: (stile(back(s)), 0)),
        held_once((g, KV_WIDTH, WINDOW), lambda s, l: (stile(front(s)), 0, 0)),
        held_once((g, KV_WIDTH, WINDOW), lambda s, l: (stile(front(s)), 0, 0)),
        pl.BlockSpec((g, CONV_W - 1, CONV_DIM), lambda s, l: (stile(front(s)), 0, 0)),
    ) + tuple(c[1] for c in casts)
    scratch = [
        pltpu.VMEM((KV_HEADS, WINDOW, 2 * HEAD_DIM), F32),
        pltpu.VMEM((KV_HEADS, WINDOW, 2 * HEAD_DIM), F32),
        pltpu.VMEM((8, CONV_DIM), F32),
        pltpu.VMEM((8 + TILE, CONV_DIM), F32),
        pltpu.VMEM((TILE, ATTN_WIDTH), F32),
        pltpu.VMEM((TILE, MEM_WIDTH), F32),
        pltpu.VMEM((TILE, MIX_WIDTH), BF),
        pltpu.VMEM((TILE, D_MODEL), F32),
    ]
    return pl.pallas_call(
        functools.partial(_layer_kernel, tiles_per_seq=tps, n_prompt_tiles=n_pt),
        grid_spec=pltpu.PrefetchScalarGridSpec(
            num_scalar_prefetch=1, grid=(n_tiles + 1,), in_specs=in_specs,
            out_specs=out_specs, scratch_shapes=scratch),
        out_shape=out_shape,
        compiler_params=pltpu.CompilerParams(
            dimension_semantics=("arbitrary",),
            vmem_limit_bytes=VMEM_LIMIT_BYTES),
        name="trunk_layer",
    )(layer, sinks, xp, mem_k_t, mem_v_t, xs, cache_k_t, cache_v_t, cache_conv, cmem_k_t,
      cmem_v_t, *[weights[n] for n in _WEIGHT_ORDER], *[f32_matrices[n] for n in _MATRIX_ORDER])


def _mem_kv_kernel(mem_ref, g_mem_ref, w_ref, bd_ref, gmk_ref, w0_in_ref, w0_out_ref, w0_gu_ref,
                   w0_down_ref, mk_ref, mv_ref, w0_in_bf_ref, w0_out_bf_ref, w0_gu_bf_ref,
                   w0_down_bf_ref):
    w = w_ref[0].astype(BF)
    for b in range(mem_ref.shape[0]):
        xn = _rms(mem_ref[b], g_mem_ref[0]).astype(BF)
        kv = _dot(xn, w)
        mk_ref[0, b] = _head_rms(kv[:, :MEM_WIDTH], gmk_ref[0], bd_ref[...]).T
        mv_ref[0, b] = kv[:, MEM_WIDTH:].T
    w0_in_bf_ref[...] = w0_in_ref[...].astype(BF)
    w0_out_bf_ref[...] = w0_out_ref[...].astype(BF)
    w0_gu_bf_ref[...] = w0_gu_ref[...].astype(BF)
    w0_down_bf_ref[...] = w0_down_ref[...].astype(BF)


def _mem_kv(mem, g_mem, w_mem_kv, bd, gmk, f32_matrices):
    batch = mem.shape[0]
    out = jax.ShapeDtypeStruct((DEPTH, batch, MEM_WIDTH, N_MEM), F32)
    casts = [_cast_specs(n, DEPTH, lambda l: 0, lambda l: l) for n in _MATRIX_ORDER]
    return pl.pallas_call(
        _mem_kv_kernel,
        grid=(DEPTH,),
        in_specs=[
            pl.BlockSpec((batch, N_MEM, D_MODEL), lambda l: (0, 0, 0),
                         pipeline_mode=pl.Buffered(1)),
            pl.BlockSpec((1, 1, D_MODEL), lambda l: (l, 0, 0)),
            pl.BlockSpec((1, D_MODEL, 2 * MEM_WIDTH), lambda l: (l, 0, 0)),
            pl.BlockSpec((256, 256), lambda l: (0, 0)),
            pl.BlockSpec((1, 1, MEM_WIDTH), lambda l: (l, 0, 0)),
        ] + [c[0] for c in casts],
        out_specs=(pl.BlockSpec((1, batch, MEM_WIDTH, N_MEM), lambda l: (l, 0, 0, 0)),
                   pl.BlockSpec((1, batch, MEM_WIDTH, N_MEM), lambda l: (l, 0, 0, 0)))
        + tuple(c[1] for c in casts),
        out_shape=(out, out) + tuple(c[2] for c in casts),
        compiler_params=pltpu.CompilerParams(
            dimension_semantics=("arbitrary",),
            vmem_limit_bytes=VMEM_LIMIT_BYTES),
        name="prompt_mem_kv",
    )(mem, g_mem, w_mem_kv, bd, gmk, *[f32_matrices[n] for n in _MATRIX_ORDER])


def _tile_heads(g, n):
    return jnp.tile(g, (1, n)).reshape(g.shape[0], 1, n * g.shape[1])


def _feature_major(c):
    lead = c.shape[:-3]
    n_tok, heads, dim = c.shape[-3:]
    perm = tuple(range(len(lead))) + (len(lead) + 1, len(lead) + 2, len(lead))
    return jnp.transpose(c, perm).reshape(lead + (heads * dim, n_tok))


def _token_major(c_t, heads):
    lead = c_t.shape[:-2]
    n_tok = c_t.shape[-1]
    c = c_t.reshape(lead + (heads, HEAD_DIM, n_tok))
    perm = tuple(range(len(lead))) + (len(lead) + 2, len(lead), len(lead) + 1)
    return jnp.transpose(c, perm)


def kernel(x_prompt, x_sample, mem_prompt, cache_win_k, cache_win_v, cache_conv, cache_mem_k,
           cache_mem_v, attn_norm_g, w_in, q_norm_g, k_norm_g, sinks, conv_w, mem_norm_g,
           w_mem_kv, mq_norm_g, mk_norm_g, out_norm_g, w_out, ffn_norm_g, w_gate_up, w_down):
    batch, seq, _ = x_prompt.shape
    dec_batch, dec_seq, _ = x_sample.shape
    assert dec_seq == CHUNK and seq % TILE == 0 and dec_batch % G_SAMPLE == 0

    head = jnp.arange(256) // HEAD_DIM
    bd = jnp.where(head[:, None] == head[None, :], 1.0 / HEAD_DIM, 0.0).astype(BF)

    f32_matrices = dict(w_in=w_in, w_out=w_out, w_gu=w_gate_up, w_down=w_down)
    small = dict(
        g_attn=attn_norm_g.reshape(DEPTH, 1, D_MODEL), bd=bd,
        gq=_tile_heads(q_norm_g, N_HEADS), gk=_tile_heads(k_norm_g, KV_HEADS),
        gmq=_tile_heads(mq_norm_g, MEM_HEADS), conv_w=conv_w,
        g_out=out_norm_g.reshape(DEPTH, 1, MIX_WIDTH),
        g_ffn=ffn_norm_g.reshape(DEPTH, 1, D_MODEL))

    mk_t, mv_t, *first = _mem_kv(mem_prompt, mem_norm_g.reshape(DEPTH, 1, D_MODEL), w_mem_kv, bd,
                                 _tile_heads(mk_norm_g, MEM_HEADS), f32_matrices)
    matrices = dict(zip(_MATRIX_ORDER, first))

    ck_t = _feature_major(cache_win_k)
    cv_t = _feature_major(cache_win_v)
    cmk_t = _feature_major(cache_mem_k)
    cmv_t = _feature_major(cache_mem_v)

    yp = x_prompt
    ys = x_sample.reshape(dec_batch * dec_seq, D_MODEL)
    wk_p, wv_p, cv_p, wk_s, wv_s, cv_s = [], [], [], [], [], []
    for l in range(DEPTH):
        layer = jnp.full((1,), l, jnp.int32)
        yp, k_p, v_p, c_p, ys, k_s, v_s, c_s, *next_matrices = _layer(
            layer, yp, mk_t, mv_t, ys, ck_t, cv_t, cache_conv, cmk_t, cmv_t, sinks,
            dict(small, **matrices), f32_matrices)
        wk_p.append(k_p); wv_p.append(v_p); cv_p.append(c_p)
        wk_s.append(k_s); wv_s.append(v_s); cv_s.append(c_s)
        matrices = dict(zip(_MATRIX_ORDER, next_matrices))

    return (yp, ys.reshape(dec_batch, dec_seq, D_MODEL),
            _token_major(jnp.stack(wk_p), KV_HEADS), _token_major(jnp.stack(wv_p), KV_HEADS),
            jnp.stack(cv_p),
            _token_major(mk_t, MEM_HEADS), _token_major(mv_t, MEM_HEADS),
            _token_major(jnp.stack(wk_s), KV_HEADS), _token_major(jnp.stack(wv_s), KV_HEADS),
            jnp.stack(cv_s))
```

```python
import functools

import jax
import jax.numpy as jnp
from jax import lax
from jax.experimental import pallas as pl
from jax.experimental.pallas import tpu as pltpu

D_MODEL = 1024
DEPTH = 4
CHUNK = 64
HEAD_DIM = 64
N_HEADS = 8
KV_HEADS = 2
GROUP = N_HEADS // KV_HEADS
WINDOW = 128
ATTN_WIDTH = N_HEADS * HEAD_DIM
KV_WIDTH = KV_HEADS * HEAD_DIM
CONV_DIM = 256
CONV_W = 3
MEM_HEADS = 4
MEM_WIDTH = MEM_HEADS * HEAD_DIM
N_MEM = 256
MIX_WIDTH = ATTN_WIDTH + CONV_DIM + MEM_WIDTH
IN_WIDTH = ATTN_WIDTH + 2 * KV_WIDTH + 3 * CONV_DIM + MEM_WIDTH
D_FF = 2816
EPS = 1e-6
ATTN_SCALE = HEAD_DIM ** -0.5
NEG = -1e30

_Q0, _K0, _V0 = 0, ATTN_WIDTH, ATTN_WIDTH + KV_WIDTH
_CB0 = ATTN_WIDTH + 2 * KV_WIDTH
_CC0 = _CB0 + CONV_DIM
_CX0 = _CC0 + CONV_DIM
_MQ0 = _CX0 + CONV_DIM
_A_END = ATTN_WIDTH
_C_END = ATTN_WIDTH + CONV_DIM

PAIR = 2 * CHUNK
TM_PROMPT = 512
G_SAMPLE = 8
SAMPLE_LOCKSTEP = 2
FF_BLOCK = 256
VMEM_LIMIT_BYTES = 56 * 1024 * 1024

BF = jnp.bfloat16
F32 = jnp.float32


def _dot(a, b):
    return jnp.dot(a, b, preferred_element_type=F32)


def _dot_nt(a, b):
    return lax.dot_general(a, b, (((1,), (1,)), ((), ())), preferred_element_type=F32)


def _rms(x, g):
    ms = jnp.mean(x * x, axis=-1, keepdims=True)
    return x * lax.rsqrt(ms + EPS) * g


def _head_rms(z, g, bd):
    w = z.shape[-1]
    ms = _dot((z * z).astype(BF), bd[:w, :w])
    return z * lax.rsqrt(ms + EPS) * g


def _lane_iota(shape):
    return lax.broadcasted_iota(jnp.int32, shape, len(shape) - 1)


def _split_heads_lo_hi(z):
    lo = (_lane_iota(z.shape) % (2 * HEAD_DIM)) < HEAD_DIM
    return jnp.where(lo, z, 0.0).astype(BF), jnp.where(lo, 0.0, z).astype(BF)


def _stack_group_queries(q_lo, q_hi, r0, rows, j):
    c0 = j * GROUP * HEAD_DIM
    return jnp.concatenate(
        [q_lo[r0:r0 + rows, c0:c0 + 128], q_hi[r0:r0 + rows, c0:c0 + 128],
         q_lo[r0:r0 + rows, c0 + 128:c0 + 256], q_hi[r0:r0 + rows, c0 + 128:c0 + 256]], axis=0)


def _dup_halves(z):
    sw = pltpu.roll(z, HEAD_DIM, axis=1)
    lo = _lane_iota(z.shape) < HEAD_DIM
    return jnp.where(lo, z, sw), jnp.where(lo, sw, z)


def _pick_head_lanes(o_all, rows):
    grp = _lane_iota((rows, GROUP * HEAD_DIM)) // HEAD_DIM
    out = o_all[3 * rows:4 * rows]
    for h in (2, 1, 0):
        out = jnp.where(grp == h, o_all[h * rows:(h + 1) * rows], out)
    return out


def _softmax_rows(s, sink):
    m = jnp.max(s, axis=-1, keepdims=True)
    if sink is not None:
        m = jnp.maximum(m, sink)
    p = jnp.exp(s - m)
    den = jnp.sum(p, axis=-1, keepdims=True)
    if sink is not None:
        den = den + jnp.exp(sink - m)
    return p * (1.0 / den)


def _project_in(x, g_attn, w_in, bd, gq, gk, gmq):
    xn = _rms(x, g_attn).astype(BF)
    u = _dot(xn, w_in)
    half = ATTN_WIDTH // 2
    qn = jnp.concatenate(
        [_head_rms(u[:, _Q0:_Q0 + half], gq[:, :half], bd),
         _head_rms(u[:, _Q0 + half:_K0], gq[:, half:], bd)], axis=-1) * ATTN_SCALE
    kn = _head_rms(u[:, _K0:_V0], gk, bd)
    v = u[:, _V0:_CB0]
    cb = u[:, _CB0:_CC0]
    ccx = u[:, _CC0:_CX0] * u[:, _CX0:_MQ0]
    mqn = _head_rms(u[:, _MQ0:], gmq, bd) * ATTN_SCALE
    return qn, kn, v, cb, ccx, mqn


def _conv_from_scratch(conv_scr, ccx, cb, conv_w, rows):
    sh2 = conv_scr[6:6 + rows, :]
    sh1 = conv_scr[7:7 + rows, :]
    return cb * (sh2 * conv_w[0:1, :] + sh1 * conv_w[1:2, :] + ccx * conv_w[2:3, :])


_N_SWIGLU_UNITS = 2 * (D_FF // FF_BLOCK) + (D_FF // FF_BLOCK + 1) // 2


def _swiglu_units(h, hn, w_gu_ref, w_down_ref, result):
    blk = FF_BLOCK
    n_blk = D_FF // blk
    acts = []
    y = h

    def down(c):
        lo, hi = 2 * c, min(2 * c + 2, n_blk)
        act = acts[lo] if hi - lo == 1 else jnp.concatenate(acts[lo:hi], axis=-1)
        return _dot(act, w_down_ref[0, lo * blk:hi * blk, :])

    for b in range(n_blk):
        gate = _dot(hn, w_gu_ref[0, :, b * blk:(b + 1) * blk])
        yield
        up = _dot(hn, w_gu_ref[0, :, D_FF + b * blk:D_FF + (b + 1) * blk])
        acts.append((gate * jax.nn.sigmoid(gate) * up).astype(BF))
        yield
        if b >= 2 and b % 2 == 0:
            y = y + down(b // 2 - 1)
            yield
    for c in range((n_blk - 1) // 2, (n_blk + 1) // 2):
        y = y + down(c)
        yield
    result.append(y)


def _merge(first, n_first, second, n_second):
    done = 0
    for i in range(n_first):
        next(first)
        want = ((i + 1) * n_second) // n_first
        for _ in range(want - done):
            next(second)
        done = want
    assert next(first, "end") == "end" and next(second, "end") == "end"


def _round_robin(chains):
    chains = list(chains)
    while chains:
        for chain in list(chains):
            try:
                next(chain)
            except StopIteration:
                chains.remove(chain)
            else:
                yield


_N_PROMPT_UNITS = (TM_PROMPT // PAIR) * KV_HEADS * 4 + 1 + MEM_HEADS * 3


def _prompt_layer_kernel(layer_ref, sinks_ref, x_ref, mk_ref, mv_ref, g_attn_ref, w_in_ref, bd_ref,
                         gq_ref, gk_ref, gmq_ref, conv_w_ref, g_out_ref, w_out_ref, g_ffn_ref,
                         w_gu_ref, w_down_ref, nw_in_ref, nw_out_ref, nw_gu_ref, nw_down_ref,
                         y_ref, newk_ref, newv_ref, newc_ref,
                         nw_in_bf_ref, nw_out_bf_ref, nw_gu_bf_ref, nw_down_bf_ref,
                         kcar_scr, vcar_scr, ccar_scr, conv_scr, a_scr, mo_scr, mix_scr,
                         xres_scr, *,
                         tiles_per_seq):
    tm = TM_PROMPT
    layer = layer_ref[0]
    s = pl.program_id(0)
    n_tiles = pl.num_programs(0) - 1
    t = jnp.minimum(s, n_tiles - 1) % tiles_per_seq

    @pl.when(s == 0)
    def _():
        mix_scr[...] = jnp.zeros(mix_scr.shape, BF)
        xres_scr[...] = jnp.zeros(xres_scr.shape, F32)

    h_prev = xres_scr[...] + _dot(mix_scr[...], w_out_ref[0])
    hn_prev = _rms(h_prev, g_ffn_ref[0]).astype(BF)
    y_prev = []
    back = _swiglu_units(h_prev, hn_prev, w_gu_ref, w_down_ref, y_prev)

    started = t > 0
    k_prev = [jnp.where(started, kcar_scr[j], 0.0).astype(BF) for j in range(KV_HEADS)]
    v_prev = [jnp.where(started, vcar_scr[j], 0.0).astype(BF) for j in range(KV_HEADS)]
    conv_scr[0:8, :] = jnp.where(started, ccar_scr[...], 0.0)

    x = x_ref[0]
    qn, kn, v, cb, ccx, mqn = _project_in(x, g_attn_ref[0], w_in_ref[0], bd_ref[...], gq_ref[0],
                                          gk_ref[0], gmq_ref[0])

    k_dup = _dup_halves(kn)
    v_dup = _dup_halves(v)
    for j in range(KV_HEADS):
        kcar_scr[j] = k_dup[j][tm - WINDOW:tm, :]
        vcar_scr[j] = v_dup[j][tm - WINDOW:tm, :]
    k_rep = [k_dup[j].astype(BF) for j in range(KV_HEADS)]
    v_rep = [v_dup[j].astype(BF) for j in range(KV_HEADS)]

    q_lo, q_hi = _split_heads_lo_hi(qn)

    row_chunk = lax.broadcasted_iota(jnp.int32, (PAIR, 2 * PAIR), 0) // CHUNK
    col = _lane_iota((PAIR, 2 * PAIR))
    first_key = jnp.where(started, 0, WINDOW)
    visible = (col >= row_chunk * CHUNK) & (col < (row_chunk + 3) * CHUNK)
    bias_rest = jnp.where(visible, 0.0, NEG)
    bias_first = jnp.where(col >= first_key, bias_rest, NEG)

    grp = _lane_iota(mqn.shape) // HEAD_DIM
    mk_t = mk_ref[0, 0].astype(BF)
    mv_t = mv_ref[0, 0].astype(BF)
    g_out = g_out_ref[0]

    def front_units():
        for p in range(tm // PAIR):
            r0 = p * PAIR
            bias = bias_first if p == 0 else bias_rest
            for j in range(KV_HEADS):
                c0 = j * GROUP * HEAD_DIM
                qf = _stack_group_queries(q_lo, q_hi, r0, PAIR, j)
                if p == 0:
                    k_win = jnp.concatenate([k_prev[j], k_rep[j][0:PAIR]], axis=0)
                    v_win = jnp.concatenate([v_prev[j], v_rep[j][0:PAIR]], axis=0)
                else:
                    k_win = k_rep[j][r0 - PAIR:r0 + PAIR]
                    v_win = v_rep[j][r0 - PAIR:r0 + PAIR]
                sc = _dot_nt(qf, k_win)
                yield
                probs = []
                for h in range(GROUP):
                    sh = sc[h * PAIR:(h + 1) * PAIR] + bias
                    probs.append(_softmax_rows(sh, sinks_ref[layer, j * GROUP + h]).astype(BF))
                    if h % 2 == 1:
                        yield
                o_all = _dot(jnp.concatenate(probs, axis=0),
                             jnp.concatenate([v_win, v_win], axis=-1))
                a_scr[r0:r0 + PAIR, c0:c0 + GROUP * HEAD_DIM] = _pick_head_lanes(o_all, PAIR)
                yield
        mix_scr[:, :_A_END] = _rms(a_scr[...], g_out[:, :_A_END]).astype(BF)
        yield
        for h in range(MEM_HEADS):
            mq_h = jnp.where(grp == h, mqn, 0.0).astype(BF)
            sc = _dot(mq_h, mk_t)
            yield
            pm = _softmax_rows(sc, None).astype(BF)
            yield
            o_h = _dot_nt(pm, mv_t)
            mo_scr[:, h * HEAD_DIM:(h + 1) * HEAD_DIM] = o_h[:, h * HEAD_DIM:(h + 1) * HEAD_DIM]
            yield

    xres_scr[...] = x

    conv_scr[8:8 + tm, :] = ccx
    cy = _conv_from_scratch(conv_scr, ccx, cb, conv_w_ref[0], tm)
    ccar_scr[...] = ccx[tm - 8:tm, :]
    mix_scr[:, _A_END:_C_END] = _rms(cy, g_out[:, _A_END:_C_END]).astype(BF)

    _merge(back, _N_SWIGLU_UNITS, front_units(), _N_PROMPT_UNITS)
    y_ref[0] = y_prev[0]
    mix_scr[:, _C_END:] = _rms(mo_scr[...], g_out[:, _C_END:]).astype(BF)

    @pl.when(t == tiles_per_seq - 1)
    def _():
        newk_ref[0] = kn[tm - WINDOW:tm, :].T
        newv_ref[0] = v[tm - WINDOW:tm, :].T
        newc_ref[0] = ccx[tm - (CONV_W - 1):tm, :]

    nw_in_bf_ref[...] = nw_in_ref[...].astype(BF)
    nw_out_bf_ref[...] = nw_out_ref[...].astype(BF)
    nw_gu_bf_ref[...] = nw_gu_ref[...].astype(BF)
    nw_down_bf_ref[...] = nw_down_ref[...].astype(BF)


def _layer_spec(shape):
    zeros = (0,) * len(shape)
    return pl.BlockSpec((1,) + shape, lambda *args: (args[-1][0],) + zeros,
                        pipeline_mode=pl.Buffered(1))


def _matrix_spec(shape):
    return pl.BlockSpec((1,) + shape, lambda *args: (0, 0, 0), pipeline_mode=pl.Buffered(1))


_MATRIX_SHAPES = dict(w_in=(D_MODEL, IN_WIDTH), w_out=(MIX_WIDTH, D_MODEL),
                      w_gu=(D_MODEL, 2 * D_FF), w_down=(D_FF, D_MODEL))
_MATRIX_ORDER = ("w_in", "w_out", "w_gu", "w_down")
_CAST_BLOCKS = dict(w_in=32, w_out=32, w_gu=32, w_down=16)


def _cast_specs(name, n_blk, layer_of, step_of):
    rows, cols = _MATRIX_SHAPES[name]
    assert rows % (16 * n_blk) == 0
    blk = (1, rows // n_blk, cols)
    src = pl.BlockSpec(blk, lambda *a: (layer_of(*a), jnp.minimum(step_of(*a), n_blk - 1), 0))
    dst = pl.BlockSpec(blk, lambda *a: (0, jnp.minimum(step_of(*a), n_blk - 1), 0))
    return src, dst, jax.ShapeDtypeStruct((1, rows, cols), BF)


def _weight_specs():
    return [
        _layer_spec((1, D_MODEL)),
        _matrix_spec(_MATRIX_SHAPES["w_in"]),
        pl.BlockSpec((256, 256), lambda *args: (0, 0), pipeline_mode=pl.Buffered(1)),
        _layer_spec((1, ATTN_WIDTH)),
        _layer_spec((1, KV_WIDTH)),
        _layer_spec((1, MEM_WIDTH)),
        _layer_spec((CONV_W, CONV_DIM)),
        _layer_spec((1, MIX_WIDTH)),
        _matrix_spec(_MATRIX_SHAPES["w_out"]),
        _layer_spec((1, D_MODEL)),
        _matrix_spec(_MATRIX_SHAPES["w_gu"]),
        _matrix_spec(_MATRIX_SHAPES["w_down"]),
    ]


_WEIGHT_ORDER = ("g_attn", "w_in", "bd", "gq", "gk", "gmq", "conv_w", "g_out", "w_out", "g_ffn",
                 "w_gu", "w_down")

_SMEM_SPEC = pl.BlockSpec(memory_space=pltpu.SMEM)


def _prompt_layer(layer, x, mem_k_t, mem_v_t, sinks, weights, f32_matrices):
    batch, seq, _ = x.shape
    tm = TM_PROMPT
    tps = seq // tm
    n_tiles = batch * tps

    def front(s):
        return jnp.minimum(s, n_tiles - 1)

    def back(s):
        return jnp.maximum(s - 1, 0)

    assert max(_CAST_BLOCKS.values()) <= n_tiles
    casts = [_cast_specs(n, _CAST_BLOCKS[n], lambda s, l: jnp.minimum(l[0] + 1, DEPTH - 1),
                         lambda s, l: s) for n in _MATRIX_ORDER]
    in_specs = [
        _SMEM_SPEC,
        pl.BlockSpec((1, tm, D_MODEL), lambda s, l: (front(s) // tps, front(s) % tps, 0)),
        pl.BlockSpec((1, 1, MEM_WIDTH, N_MEM), lambda s, l: (l[0], front(s) // tps, 0, 0)),
        pl.BlockSpec((1, 1, MEM_WIDTH, N_MEM), lambda s, l: (l[0], front(s) // tps, 0, 0)),
    ] + _weight_specs() + [c[0] for c in casts]
    out_shape = (
        jax.ShapeDtypeStruct((batch, seq, D_MODEL), F32),
        jax.ShapeDtypeStruct((batch, KV_WIDTH, WINDOW), F32),
        jax.ShapeDtypeStruct((batch, KV_WIDTH, WINDOW), F32),
        jax.ShapeDtypeStruct((batch, CONV_W - 1, CONV_DIM), F32),
    ) + tuple(c[2] for c in casts)
    out_specs = (
        pl.BlockSpec((1, tm, D_MODEL), lambda s, l: (back(s) // tps, back(s) % tps, 0)),
        pl.BlockSpec((1, KV_WIDTH, WINDOW), lambda s, l: (front(s) // tps, 0, 0)),
        pl.BlockSpec((1, KV_WIDTH, WINDOW), lambda s, l: (front(s) // tps, 0, 0)),
        pl.BlockSpec((1, CONV_W - 1, CONV_DIM), lambda s, l: (front(s) // tps, 0, 0)),
    ) + tuple(c[1] for c in casts)
    scratch = [
        pltpu.VMEM((KV_HEADS, WINDOW, 2 * HEAD_DIM), F32),
        pltpu.VMEM((KV_HEADS, WINDOW, 2 * HEAD_DIM), F32),
        pltpu.VMEM((8, CONV_DIM), F32),
        pltpu.VMEM((8 + tm, CONV_DIM), F32),
        pltpu.VMEM((tm, ATTN_WIDTH), F32),
        pltpu.VMEM((tm, MEM_WIDTH), F32),
        pltpu.VMEM((tm, MIX_WIDTH), BF),
        pltpu.VMEM((tm, D_MODEL), F32),
    ]
    return pl.pallas_call(
        functools.partial(_prompt_layer_kernel, tiles_per_seq=tps),
        grid_spec=pltpu.PrefetchScalarGridSpec(
            num_scalar_prefetch=1, grid=(n_tiles + 1,), in_specs=in_specs,
            out_specs=out_specs, scratch_shapes=scratch),
        out_shape=out_shape,
        compiler_params=pltpu.CompilerParams(
            dimension_semantics=("arbitrary",),
            vmem_limit_bytes=VMEM_LIMIT_BYTES),
        name="prompt_layer",
    )(layer, sinks, x, mem_k_t, mem_v_t, *[weights[n] for n in _WEIGHT_ORDER],
      *[f32_matrices[n] for n in _MATRIX_ORDER])


_N_SAMPLE_HALF_UNITS = (G_SAMPLE // 2 // SAMPLE_LOCKSTEP) * (
    SAMPLE_LOCKSTEP * (1 + KV_HEADS * 4 + 3) + 1)


def _sample_layer_kernel(layer_ref, sinks_ref, x_ref, ck_ref, cv_ref, cc_ref, mk_ref, mv_ref,
                         g_attn_ref, w_in_ref, bd_ref, gq_ref, gk_ref, gmq_ref, conv_w_ref,
                         g_out_ref, w_out_ref, g_ffn_ref, w_gu_ref, w_down_ref, y_ref, newk_ref,
                         newv_ref, newc_ref, conv_scr, a_scr, mix_scr):
    rows = CHUNK
    layer = layer_ref[0]
    g_out = g_out_ref[0]
    x = x_ref[...]
    qn, kn, v, cb, ccx, mqn = _project_in(x, g_attn_ref[0], w_in_ref[0], bd_ref[...], gq_ref[0],
                                          gk_ref[0], gmq_ref[0])
    q_lo, q_hi = _split_heads_lo_hi(qn)
    grp = _lane_iota(mqn.shape) // HEAD_DIM
    mq_heads = [jnp.where(grp == h, mqn, 0.0).astype(BF) for h in range(MEM_HEADS)]
    conv_w = conv_w_ref[0]
    old_lane = _lane_iota((KV_WIDTH, WINDOW)) < WINDOW - rows

    def shifted_window(cache_t, new_rows):
        new_t = jnp.concatenate([new_rows, new_rows], axis=0).T
        return jnp.where(old_lane, pltpu.roll(cache_t, WINDOW - rows, axis=1), new_t)

    def window_chain(b, j, ck_bf, cv_bf, k_dup_j, v_dup_j):
        r0 = b * rows
        c0 = j * GROUP * HEAD_DIM
        qf = _stack_group_queries(q_lo, q_hi, r0, rows, j)
        kj_t = ck_bf[j * HEAD_DIM:(j + 1) * HEAD_DIM, :]
        vj_t = cv_bf[j * HEAD_DIM:(j + 1) * HEAD_DIM, :]
        sc = jnp.concatenate(
            [_dot(qf, jnp.concatenate([kj_t, kj_t], axis=0)),
             _dot_nt(qf, k_dup_j.astype(BF))], axis=-1)
        yield
        probs = []
        for h in range(GROUP):
            probs.append(_softmax_rows(sc[h * rows:(h + 1) * rows],
                                       sinks_ref[layer, j * GROUP + h]).astype(BF))
            if h % 2 == 1:
                yield
        pm = jnp.concatenate(probs, axis=0)
        vb = v_dup_j.astype(BF)
        o_all = (_dot_nt(pm[:, :WINDOW], jnp.concatenate([vj_t] * GROUP, axis=0))
                 + _dot(pm[:, WINDOW:], jnp.concatenate([vb, vb], axis=-1)))
        a_scr[r0:r0 + rows, c0:c0 + GROUP * HEAD_DIM] = _pick_head_lanes(o_all, rows)
        yield

    def memory_chain(b):
        r0 = b * rows
        mq_blocks = jnp.concatenate([mq_heads[h][r0:r0 + rows] for h in range(MEM_HEADS)], axis=0)
        sc = _dot(mq_blocks, mk_ref[0, b].astype(BF))
        yield
        pm = _softmax_rows(sc, None).astype(BF)
        yield
        mo_b = _pick_head_lanes(_dot_nt(pm, mv_ref[0, b].astype(BF)), rows)
        mix_scr[r0:r0 + rows, _C_END:] = _rms(mo_b, g_out[:, _C_END:]).astype(BF)
        yield

    def mixer_units(b_lo, b_hi):
        for b0 in range(b_lo, b_hi, SAMPLE_LOCKSTEP):
            chains = []
            for b in range(b0, b0 + SAMPLE_LOCKSTEP):
                r0 = b * rows
                ck_t = ck_ref[0, b]
                cv_t = cv_ref[0, b]
                kn_b = kn[r0:r0 + rows]
                v_b = v[r0:r0 + rows]
                newk_ref[b] = shifted_window(ck_t, kn_b)
                newv_ref[b] = shifted_window(cv_t, v_b)
                k_dup = _dup_halves(kn_b)
                v_dup = _dup_halves(v_b)

                ccx_b = ccx[r0:r0 + rows]
                conv_scr[6:8, :] = cc_ref[0, b]
                conv_scr[8:8 + rows, :] = ccx_b
                cy_b = _conv_from_scratch(conv_scr, ccx_b, cb[r0:r0 + rows], conv_w, rows)
                mix_scr[r0:r0 + rows, _A_END:_C_END] = _rms(cy_b,
                                                            g_out[:, _A_END:_C_END]).astype(BF)
                newc_ref[b] = ccx_b[rows - (CONV_W - 1):rows, :]
                yield
                chains += [window_chain(b, j, ck_t.astype(BF), cv_t.astype(BF), k_dup[j], v_dup[j])
                           for j in range(KV_HEADS)]
                chains.append(memory_chain(b))
            yield from _round_robin(chains)
            r0, r1 = b0 * rows, (b0 + SAMPLE_LOCKSTEP) * rows
            mix_scr[r0:r1, :_A_END] = _rms(a_scr[r0:r1, :], g_out[:, :_A_END]).astype(BF)
            yield

    def output_units(r_lo, r_hi):
        h = x[r_lo:r_hi] + _dot(mix_scr[r_lo:r_hi, :], w_out_ref[0])
        hn = _rms(h, g_ffn_ref[0]).astype(BF)
        yield
        y = []
        yield from _swiglu_units(h, hn, w_gu_ref, w_down_ref, y)
        y_ref[r_lo:r_hi, :] = y[0]

    half = G_SAMPLE // 2
    for _ in mixer_units(0, half):
        pass
    _merge(output_units(0, half * rows), 1 + _N_SWIGLU_UNITS,
           mixer_units(half, G_SAMPLE), _N_SAMPLE_HALF_UNITS)
    for _ in output_units(half * rows, G_SAMPLE * rows):
        pass


def _sample_layer(layer, x, cache_k_t, cache_v_t, cache_conv, mem_k_t, mem_v_t, sinks, weights):
    n_b = cache_k_t.shape[1]
    g = G_SAMPLE
    tm = g * CHUNK
    in_specs = [
        _SMEM_SPEC,
        pl.BlockSpec((tm, D_MODEL), lambda i, l: (i, 0)),
        pl.BlockSpec((1, g, KV_WIDTH, WINDOW), lambda i, l: (l[0], i, 0, 0)),
        pl.BlockSpec((1, g, KV_WIDTH, WINDOW), lambda i, l: (l[0], i, 0, 0)),
        pl.BlockSpec((1, g, CONV_W - 1, CONV_DIM), lambda i, l: (l[0], i, 0, 0)),
        pl.BlockSpec((1, g, MEM_WIDTH, N_MEM), lambda i, l: (l[0], i, 0, 0)),
        pl.BlockSpec((1, g, MEM_WIDTH, N_MEM), lambda i, l: (l[0], i, 0, 0)),
    ] + _weight_specs()
    out_shape = (
        jax.ShapeDtypeStruct((n_b * CHUNK, D_MODEL), F32),
        jax.ShapeDtypeStruct((n_b, KV_WIDTH, WINDOW), F32),
        jax.ShapeDtypeStruct((n_b, KV_WIDTH, WINDOW), F32),
        jax.ShapeDtypeStruct((n_b, CONV_W - 1, CONV_DIM), F32),
    )
    out_specs = (
        pl.BlockSpec((tm, D_MODEL), lambda i, l: (i, 0)),
        pl.BlockSpec((g, KV_WIDTH, WINDOW), lambda i, l: (i, 0, 0)),
        pl.BlockSpec((g, KV_WIDTH, WINDOW), lambda i, l: (i, 0, 0)),
        pl.BlockSpec((g, CONV_W - 1, CONV_DIM), lambda i, l: (i, 0, 0)),
    )
    scratch = [
        pltpu.VMEM((8 + CHUNK, CONV_DIM), F32),
        pltpu.VMEM((tm, ATTN_WIDTH), F32),
        pltpu.VMEM((tm, MIX_WIDTH), BF),
    ]
    return pl.pallas_call(
        _sample_layer_kernel,
        grid_spec=pltpu.PrefetchScalarGridSpec(
            num_scalar_prefetch=1, grid=(n_b // g,), in_specs=in_specs, out_specs=out_specs,
            scratch_shapes=scratch),
        out_shape=out_shape,
        compiler_params=pltpu.CompilerParams(
            dimension_semantics=("arbitrary",),
            vmem_limit_bytes=VMEM_LIMIT_BYTES),
        name="sample_layer",
    )(layer, sinks, x, cache_k_t, cache_v_t, cache_conv, mem_k_t, mem_v_t,
      *[weights[n] for n in _WEIGHT_ORDER])


def _mem_kv_kernel(mem_ref, g_mem_ref, w_ref, bd_ref, gmk_ref, w0_in_ref, w0_out_ref, w0_gu_ref,
                   w0_down_ref, mk_ref, mv_ref, w0_in_bf_ref, w0_out_bf_ref, w0_gu_bf_ref,
                   w0_down_bf_ref):
    w = w_ref[0].astype(BF)
    for b in range(mem_ref.shape[0]):
        xn = _rms(mem_ref[b], g_mem_ref[0]).astype(BF)
        kv = _dot(xn, w)
        mk_ref[0, b] = _head_rms(kv[:, :MEM_WIDTH], gmk_ref[0], bd_ref[...]).T
        mv_ref[0, b] = kv[:, MEM_WIDTH:].T
    w0_in_bf_ref[...] = w0_in_ref[...].astype(BF)
    w0_out_bf_ref[...] = w0_out_ref[...].astype(BF)
    w0_gu_bf_ref[...] = w0_gu_ref[...].astype(BF)
    w0_down_bf_ref[...] = w0_down_ref[...].astype(BF)


def _mem_kv(mem, g_mem, w_mem_kv, bd, gmk, f32_matrices):
    batch = mem.shape[0]
    out = jax.ShapeDtypeStruct((DEPTH, batch, MEM_WIDTH, N_MEM), F32)
    casts = [_cast_specs(n, DEPTH, lambda l: 0, lambda l: l) for n in _MATRIX_ORDER]
    return pl.pallas_call(
        _mem_kv_kernel,
        grid=(DEPTH,),
        in_specs=[
            pl.BlockSpec((batch, N_MEM, D_MODEL), lambda l: (0, 0, 0),
                         pipeline_mode=pl.Buffered(1)),
            pl.BlockSpec((1, 1, D_MODEL), lambda l: (l, 0, 0)),
            pl.BlockSpec((1, D_MODEL, 2 * MEM_WIDTH), lambda l: (l, 0, 0)),
            pl.BlockSpec((256, 256), lambda l: (0, 0)),
            pl.BlockSpec((1, 1, MEM_WIDTH), lambda l: (l, 0, 0)),
        ] + [c[0] for c in casts],
        out_specs=(pl.BlockSpec((1, batch, MEM_WIDTH, N_MEM), lambda l: (l, 0, 0, 0)),
                   pl.BlockSpec((1, batch, MEM_WIDTH, N_MEM), lambda l: (l, 0, 0, 0)))
        + tuple(c[1] for c in casts),
        out_shape=(out, out) + tuple(c[2] for c in casts),
        compiler_params=pltpu.CompilerParams(
            dimension_semantics=("arbitrary",),
            vmem_limit_bytes=VMEM_LIMIT_BYTES),
        name="prompt_mem_kv",
    )(mem, g_mem, w_mem_kv, bd, gmk, *[f32_matrices[n] for n in _MATRIX_ORDER])


def _tile_heads(g, n):
    return jnp.tile(g, (1, n)).reshape(g.shape[0], 1, n * g.shape[1])


def _feature_major(c):
    lead = c.shape[:-3]
    n_tok, heads, dim = c.shape[-3:]
    perm = tuple(range(len(lead))) + (len(lead) + 1, len(lead) + 2, len(lead))
    return jnp.transpose(c, perm).reshape(lead + (heads * dim, n_tok))


def _token_major(c_t, heads):
    lead = c_t.shape[:-2]
    n_tok = c_t.shape[-1]
    c = c_t.reshape(lead + (heads, HEAD_DIM, n_tok))
    perm = tuple(range(len(lead))) + (len(lead) + 2, len(lead), len(lead) + 1)
    return jnp.transpose(c, perm)


def kernel(x_prompt, x_sample, mem_prompt, cache_win_k, cache_win_v, cache_conv, cache_mem_k,
           cache_mem_v, attn_norm_g, w_in, q_norm_g, k_norm_g, sinks, conv_w, mem_norm_g,
           w_mem_kv, mq_norm_g, mk_norm_g, out_norm_g, w_out, ffn_norm_g, w_gate_up, w_down):
    batch, seq, _ = x_prompt.shape
    dec_batch, dec_seq, _ = x_sample.shape
    assert dec_seq == CHUNK and seq % TM_PROMPT == 0 and dec_batch % G_SAMPLE == 0
    assert G_SAMPLE % (2 * SAMPLE_LOCKSTEP) == 0

    head = jnp.arange(256) // HEAD_DIM
    bd = jnp.where(head[:, None] == head[None, :], 1.0 / HEAD_DIM, 0.0).astype(BF)

    f32_matrices = dict(w_in=w_in, w_out=w_out, w_gu=w_gate_up, w_down=w_down)
    small = dict(
        g_attn=attn_norm_g.reshape(DEPTH, 1, D_MODEL), bd=bd,
        gq=_tile_heads(q_norm_g, N_HEADS), gk=_tile_heads(k_norm_g, KV_HEADS),
        gmq=_tile_heads(mq_norm_g, MEM_HEADS), conv_w=conv_w,
        g_out=out_norm_g.reshape(DEPTH, 1, MIX_WIDTH),
        g_ffn=ffn_norm_g.reshape(DEPTH, 1, D_MODEL))

    mk_t, mv_t, *first = _mem_kv(mem_prompt, mem_norm_g.reshape(DEPTH, 1, D_MODEL), w_mem_kv, bd,
                                 _tile_heads(mk_norm_g, MEM_HEADS), f32_matrices)
    matrices = dict(zip(_MATRIX_ORDER, first))

    ck_t = _feature_major(cache_win_k)
    cv_t = _feature_major(cache_win_v)
    cmk_t = _feature_major(cache_mem_k)
    cmv_t = _feature_major(cache_mem_v)

    yp = x_prompt
    ys = x_sample.reshape(dec_batch * dec_seq, D_MODEL)
    wk_p, wv_p, cv_p, wk_s, wv_s, cv_s = [], [], [], [], [], []
    for l in range(DEPTH):
        layer = jnp.full((1,), l, jnp.int32)
        weights = dict(small, **matrices)
        yp, k_p, v_p, c_p, *next_matrices = _prompt_layer(layer, yp, mk_t, mv_t, sinks, weights,
                                                          f32_matrices)
        wk_p.append(k_p); wv_p.append(v_p); cv_p.append(c_p)
        ys, k_s, v_s, c_s = _sample_layer(layer, ys, ck_t, cv_t, cache_conv, cmk_t, cmv_t, sinks,
                                          weights)
        wk_s.append(k_s); wv_s.append(v_s); cv_s.append(c_s)
        matrices = dict(zip(_MATRIX_ORDER, next_matrices))

    return (yp, ys.reshape(dec_batch, dec_seq, D_MODEL),
            _token_major(jnp.stack(wk_p), KV_HEADS), _token_major(jnp.stack(wv_p), KV_HEADS),
            jnp.stack(cv_p),
            _token_major(mk_t, MEM_HEADS), _token_major(mv_t, MEM_HEADS),
            _token_major(jnp.stack(wk_s), KV_HEADS), _token_major(jnp.stack(wv_s), KV_HEADS),
            jnp.stack(cv_s))
```

```python
import functools

import jax
import jax.numpy as jnp
from jax import lax
from jax.experimental import pallas as pl
from jax.experimental.pallas import tpu as pltpu

D_MODEL = 1024
DEPTH = 4
CHUNK = 64
HEAD_DIM = 64
N_HEADS = 8
KV_HEADS = 2
GROUP = N_HEADS // KV_HEADS
WINDOW = 128
ATTN_WIDTH = N_HEADS * HEAD_DIM
KV_WIDTH = KV_HEADS * HEAD_DIM
CONV_DIM = 256
CONV_W = 3
MEM_HEADS = 4
MEM_WIDTH = MEM_HEADS * HEAD_DIM
N_MEM = 256
MIX_WIDTH = ATTN_WIDTH + CONV_DIM + MEM_WIDTH
IN_WIDTH = ATTN_WIDTH + 2 * KV_WIDTH + 3 * CONV_DIM + MEM_WIDTH
D_FF = 2816
EPS = 1e-6
ATTN_SCALE = HEAD_DIM ** -0.5
NEG = -1e30

_Q0, _K0, _V0 = 0, ATTN_WIDTH, ATTN_WIDTH + KV_WIDTH
_CB0 = ATTN_WIDTH + 2 * KV_WIDTH
_CC0 = _CB0 + CONV_DIM
_CX0 = _CC0 + CONV_DIM
_MQ0 = _CX0 + CONV_DIM
_A_END = ATTN_WIDTH
_C_END = ATTN_WIDTH + CONV_DIM

PAIR = 2 * CHUNK
TM_PROMPT = 512
G_SAMPLE = 8
SAMPLE_LOCKSTEP = 2
FF_BLOCK = 256
VMEM_LIMIT_BYTES = 56 * 1024 * 1024

BF = jnp.bfloat16
F32 = jnp.float32


def _dot(a, b):
    return jnp.dot(a, b, preferred_element_type=F32)


def _dot_nt(a, b):
    return lax.dot_general(a, b, (((1,), (1,)), ((), ())), preferred_element_type=F32)


def _rms(x, g):
    ms = jnp.mean(x * x, axis=-1, keepdims=True)
    return x * lax.rsqrt(ms + EPS) * g


def _head_rms(z, g, bd):
    w = z.shape[-1]
    ms = _dot((z * z).astype(BF), bd[:w, :w])
    return z * lax.rsqrt(ms + EPS) * g


def _lane_iota(shape):
    return lax.broadcasted_iota(jnp.int32, shape, len(shape) - 1)


def _split_heads_lo_hi(z):
    lo = (_lane_iota(z.shape) % (2 * HEAD_DIM)) < HEAD_DIM
    return jnp.where(lo, z, 0.0).astype(BF), jnp.where(lo, 0.0, z).astype(BF)


def _stack_group_queries(q_lo, q_hi, r0, rows, j):
    c0 = j * GROUP * HEAD_DIM
    return jnp.concatenate(
        [q_lo[r0:r0 + rows, c0:c0 + 128], q_hi[r0:r0 + rows, c0:c0 + 128],
         q_lo[r0:r0 + rows, c0 + 128:c0 + 256], q_hi[r0:r0 + rows, c0 + 128:c0 + 256]], axis=0)


def _dup_halves(z):
    sw = pltpu.roll(z, HEAD_DIM, axis=1)
    lo = _lane_iota(z.shape) < HEAD_DIM
    return jnp.where(lo, z, sw), jnp.where(lo, sw, z)


def _pick_head_lanes(o_all, rows):
    grp = _lane_iota((rows, GROUP * HEAD_DIM)) // HEAD_DIM
    out = o_all[3 * rows:4 * rows]
    for h in (2, 1, 0):
        out = jnp.where(grp == h, o_all[h * rows:(h + 1) * rows], out)
    return out


def _softmax_rows(s, sink):
    m = jnp.max(s, axis=-1, keepdims=True)
    if sink is not None:
        m = jnp.maximum(m, sink)
    p = jnp.exp(s - m)
    den = jnp.sum(p, axis=-1, keepdims=True)
    if sink is not None:
        den = den + jnp.exp(sink - m)
    return p * (1.0 / den)


def _softmax_cols(s, sink):
    m = jnp.max(s, axis=0, keepdims=True)
    if sink is not None:
        m = jnp.maximum(m, sink)
    p = jnp.exp(s - m)
    den = jnp.sum(p, axis=0, keepdims=True)
    if sink is not None:
        den = den + jnp.exp(sink - m)
    return p, 1.0 / den


def _project_in(x, g_attn, w_in, bd, gq, gk, gmq):
    xn = _rms(x, g_attn).astype(BF)
    u = _dot(xn, w_in)
    half = ATTN_WIDTH // 2
    qn = jnp.concatenate(
        [_head_rms(u[:, _Q0:_Q0 + half], gq[:, :half], bd),
         _head_rms(u[:, _Q0 + half:_K0], gq[:, half:], bd)], axis=-1) * ATTN_SCALE
    kn = _head_rms(u[:, _K0:_V0], gk, bd)
    v = u[:, _V0:_CB0]
    cb = u[:, _CB0:_CC0]
    ccx = u[:, _CC0:_CX0] * u[:, _CX0:_MQ0]
    mqn = _head_rms(u[:, _MQ0:], gmq, bd) * ATTN_SCALE
    return qn, kn, v, cb, ccx, mqn


def _conv_from_scratch(conv_scr, ccx, cb, conv_w, rows):
    sh2 = conv_scr[6:6 + rows, :]
    sh1 = conv_scr[7:7 + rows, :]
    return cb * (sh2 * conv_w[0:1, :] + sh1 * conv_w[1:2, :] + ccx * conv_w[2:3, :])


def _swiglu_unit_costs():
    n_blk = D_FF // FF_BLOCK
    costs = []
    for b in range(n_blk):
        costs += [1.0, 1.0]
        if b >= 2 and b % 2 == 0:
            costs.append(2.0)
    for c in range((n_blk - 1) // 2, (n_blk + 1) // 2):
        costs.append(float(min(2 * c + 2, n_blk) - 2 * c))
    return costs


_SWIGLU_COSTS = _swiglu_unit_costs()


def _swiglu_units(h, hn, w_gu_ref, w_down_ref, result):
    blk = FF_BLOCK
    n_blk = D_FF // blk
    acts = []
    y = h

    def down(c):
        lo, hi = 2 * c, min(2 * c + 2, n_blk)
        act = acts[lo] if hi - lo == 1 else jnp.concatenate(acts[lo:hi], axis=-1)
        return _dot(act, w_down_ref[0, lo * blk:hi * blk, :])

    for b in range(n_blk):
        gate = _dot(hn, w_gu_ref[0, :, b * blk:(b + 1) * blk])
        yield
        up = _dot(hn, w_gu_ref[0, :, D_FF + b * blk:D_FF + (b + 1) * blk])
        acts.append((gate * jax.nn.sigmoid(gate) * up).astype(BF))
        yield
        if b >= 2 and b % 2 == 0:
            y = y + down(b // 2 - 1)
            yield
    for c in range((n_blk - 1) // 2, (n_blk + 1) // 2):
        y = y + down(c)
        yield
    result.append(y)


def _merge(first, first_costs, second, second_costs):
    streams = [[first, list(first_costs), 0.0, sum(first_costs)],
               [second, list(second_costs), 0.0, sum(second_costs)]]
    while streams[0][1] or streams[1][1]:
        live = [st for st in streams if st[1]]
        st = min(live, key=lambda st: st[2] / st[3])
        next(st[0])
        st[2] += st[1].pop(0)
        if not st[1]:
            assert next(st[0], "end") == "end"


def _round_robin(chains):
    chains = list(chains)
    while chains:
        for chain in list(chains):
            try:
                next(chain)
            except StopIteration:
                chains.remove(chain)
            else:
                yield


_PROMPT_COSTS = ([2.0, 2.0, 1.5] + [1.0, 0.8, 0.8, 1.0] * ((TM_PROMPT // PAIR) * KV_HEADS) + [3.0]
                 + [0.8, 1.2, 0.8] * MEM_HEADS)


def _prompt_layer_kernel(layer_ref, sinks_ref, x_ref, mk_ref, mv_ref, g_attn_ref, w_in_ref, bd_ref,
                         gq_ref, gk_ref, gmq_ref, conv_w_ref, g_out_ref, w_out_ref, g_ffn_ref,
                         w_gu_ref, w_down_ref, nw_in_ref, nw_out_ref, nw_gu_ref, nw_down_ref,
                         y_ref, newk_ref, newv_ref, newc_ref,
                         nw_in_bf_ref, nw_out_bf_ref, nw_gu_bf_ref, nw_down_bf_ref,
                         kcar_scr, vcar_scr, ccar_scr, conv_scr, a_scr, mo_scr, mix_scr,
                         xres_scr, *,
                         tiles_per_seq):
    tm = TM_PROMPT
    layer = layer_ref[0]
    s = pl.program_id(0)
    n_tiles = pl.num_programs(0) - 1
    t = jnp.minimum(s, n_tiles - 1) % tiles_per_seq

    @pl.when(s == 0)
    def _():
        mix_scr[...] = jnp.zeros(mix_scr.shape, BF)
        xres_scr[...] = jnp.zeros(xres_scr.shape, F32)

    nw_in_bf_ref[...] = nw_in_ref[...].astype(BF)
    nw_out_bf_ref[...] = nw_out_ref[...].astype(BF)
    nw_gu_bf_ref[...] = nw_gu_ref[...].astype(BF)
    nw_down_bf_ref[...] = nw_down_ref[...].astype(BF)

    h_prev = xres_scr[...] + _dot(mix_scr[...], w_out_ref[0])
    hn_prev = _rms(h_prev, g_ffn_ref[0]).astype(BF)
    y_prev = []
    back = _swiglu_units(h_prev, hn_prev, w_gu_ref, w_down_ref, y_prev)

    started = t > 0
    k_prev = [jnp.where(started, kcar_scr[j], 0.0).astype(BF) for j in range(KV_HEADS)]
    v_prev_t = jnp.where(started, vcar_scr[...], 0.0)
    conv_scr[0:8, :] = jnp.where(started, ccar_scr[...], 0.0)

    x = x_ref[0]
    bd = bd_ref[...]
    u = _dot(_rms(x, g_attn_ref[0]).astype(BF), w_in_ref[0])
    xres_scr[...] = x

    key = lax.broadcasted_iota(jnp.int32, (2 * PAIR, GROUP * PAIR), 0)
    q_chunk = (_lane_iota((2 * PAIR, GROUP * PAIR)) % PAIR) // CHUNK
    first_key = jnp.where(started, 0, WINDOW)
    visible = (key >= q_chunk * CHUNK) & (key < (q_chunk + 3) * CHUNK)
    bias_rest = jnp.where(visible, 0.0, NEG)
    bias_first = jnp.where(key >= first_key, bias_rest, NEG)
    head_of_col = _lane_iota((1, GROUP * PAIR)) // PAIR

    def sink_row(j):
        row = jnp.full((1, GROUP * PAIR), sinks_ref[layer, j * GROUP + GROUP - 1], F32)
        for h in range(GROUP - 2, -1, -1):
            row = jnp.where(head_of_col == h, sinks_ref[layer, j * GROUP + h], row)
        return row

    half = GROUP * PAIR // 2
    mk_tok = mk_ref[0, 0].astype(BF)
    mv_t = mv_ref[0, 0].astype(BF)
    g_out = g_out_ref[0]

    kept = {}

    def front_units():
        kn = _head_rms(u[:, _K0:_V0], gk_ref[0], bd)
        k_dup = _dup_halves(kn)
        for j in range(KV_HEADS):
            kcar_scr[j] = k_dup[j][tm - WINDOW:tm, :]
        k_rep = [k_dup[j].astype(BF) for j in range(KV_HEADS)]
        v_t = u[:, _V0:_CB0].T
        vcar_scr[...] = v_t[:, tm - WINDOW:tm]
        v_all_t = jnp.concatenate([v_prev_t, v_t], axis=1).astype(BF)
        kept.update(kn=kn, v_t=v_t)
        yield

        gq = gq_ref[0]
        q_half = ATTN_WIDTH // 2
        qn = jnp.concatenate(
            [_head_rms(u[:, _Q0:_Q0 + q_half], gq[:, :q_half], bd),
             _head_rms(u[:, _Q0 + q_half:_K0], gq[:, q_half:], bd)], axis=-1) * ATTN_SCALE
        q_lo, q_hi = _split_heads_lo_hi(qn)
        yield

        ccx = u[:, _CC0:_CX0] * u[:, _CX0:_MQ0]
        conv_scr[8:8 + tm, :] = ccx
        cy = _conv_from_scratch(conv_scr, ccx, u[:, _CB0:_CC0], conv_w_ref[0], tm)
        ccar_scr[...] = ccx[tm - 8:tm, :]
        mix_scr[:, _A_END:_C_END] = _rms(cy, g_out[:, _A_END:_C_END]).astype(BF)
        kept.update(ccx=ccx)
        yield

        for p in range(tm // PAIR):
            r0 = p * PAIR
            bias = bias_first if p == 0 else bias_rest
            for j in range(KV_HEADS):
                qf = _stack_group_queries(q_lo, q_hi, r0, PAIR, j)
                if p == 0:
                    k_win = jnp.concatenate([k_prev[j], k_rep[j][0:PAIR]], axis=0)
                else:
                    k_win = k_rep[j][r0 - PAIR:r0 + PAIR]
                v_win_t = v_all_t[j * HEAD_DIM:(j + 1) * HEAD_DIM, r0:r0 + 2 * PAIR]
                sc = _dot_nt(k_win, qf) + bias
                sinks = sink_row(j)
                yield
                parts = []
                for c in range(2):
                    parts.append(_softmax_cols(sc[:, c * half:(c + 1) * half],
                                               sinks[:, c * half:(c + 1) * half]))
                    yield
                pm = jnp.concatenate([pt[0] for pt in parts], axis=1).astype(BF)
                inv = jnp.concatenate([pt[1] for pt in parts], axis=1)
                o_t = _dot(v_win_t, pm) * inv
                o_heads = jnp.concatenate(
                    [o_t[:, h * PAIR:(h + 1) * PAIR] for h in range(GROUP)], axis=0)
                c0 = j * GROUP * HEAD_DIM
                a_scr[r0:r0 + PAIR, c0:c0 + GROUP * HEAD_DIM] = o_heads.T
                yield
        mix_scr[:, :_A_END] = _rms(a_scr[...], g_out[:, :_A_END]).astype(BF)
        mq_t = (_head_rms(u[:, _MQ0:], gmq_ref[0], bd) * ATTN_SCALE).T
        row_head = lax.broadcasted_iota(jnp.int32, mq_t.shape, 0) // HEAD_DIM
        yield
        o_pair = []
        for h in range(MEM_HEADS):
            mq_h = jnp.where(row_head == h, mq_t, 0.0).astype(BF)
            sc = _dot(mk_tok, mq_h)
            yield
            pm, inv = _softmax_cols(sc, None)
            pm = pm.astype(BF)
            yield
            o_pair.append(_dot(mv_t[h * HEAD_DIM:(h + 1) * HEAD_DIM, :], pm) * inv)
            if h % 2 == 1:
                c0 = (h - 1) * HEAD_DIM
                mo_scr[:, c0:c0 + 2 * HEAD_DIM] = jnp.concatenate(o_pair, axis=0).T
                o_pair = []
            yield

    def front_then_norm():
        yield from front_units()
        mix_scr[:, _C_END:] = _rms(mo_scr[...], g_out[:, _C_END:]).astype(BF)

    _merge(back, _SWIGLU_COSTS, front_then_norm(), _PROMPT_COSTS)
    y_ref[0] = y_prev[0]

    @pl.when(t == tiles_per_seq - 1)
    def _():
        newk_ref[0] = kept["kn"][tm - WINDOW:tm, :].T
        newv_ref[0] = kept["v_t"][:, tm - WINDOW:tm]
        newc_ref[0] = kept["ccx"][tm - (CONV_W - 1):tm, :]


def _layer_spec(shape):
    zeros = (0,) * len(shape)
    return pl.BlockSpec((1,) + shape, lambda *args: (args[-1][0],) + zeros,
                        pipeline_mode=pl.Buffered(1))


def _matrix_spec(shape):
    return pl.BlockSpec((1,) + shape, lambda *args: (0, 0, 0), pipeline_mode=pl.Buffered(1))


_MATRIX_SHAPES = dict(w_in=(D_MODEL, IN_WIDTH), w_out=(MIX_WIDTH, D_MODEL),
                      w_gu=(D_MODEL, 2 * D_FF), w_down=(D_FF, D_MODEL))
_MATRIX_ORDER = ("w_in", "w_out", "w_gu", "w_down")
_CAST_BLOCKS = dict(w_in=32, w_out=32, w_gu=32, w_down=16)


def _cast_specs(name, n_blk, layer_of, step_of):
    rows, cols = _MATRIX_SHAPES[name]
    assert rows % (16 * n_blk) == 0
    blk = (1, rows // n_blk, cols)
    src = pl.BlockSpec(blk, lambda *a: (layer_of(*a), jnp.minimum(step_of(*a), n_blk - 1), 0))
    dst = pl.BlockSpec(blk, lambda *a: (0, jnp.minimum(step_of(*a), n_blk - 1), 0))
    return src, dst, jax.ShapeDtypeStruct((1, rows, cols), BF)


def _weight_specs():
    return [
        _layer_spec((1, D_MODEL)),
        _matrix_spec(_MATRIX_SHAPES["w_in"]),
        pl.BlockSpec((256, 256), lambda *args: (0, 0), pipeline_mode=pl.Buffered(1)),
        _layer_spec((1, ATTN_WIDTH)),
        _layer_spec((1, KV_WIDTH)),
        _layer_spec((1, MEM_WIDTH)),
        _layer_spec((CONV_W, CONV_DIM)),
        _layer_spec((1, MIX_WIDTH)),
        _matrix_spec(_MATRIX_SHAPES["w_out"]),
        _layer_spec((1, D_MODEL)),
        _matrix_spec(_MATRIX_SHAPES["w_gu"]),
        _matrix_spec(_MATRIX_SHAPES["w_down"]),
    ]


_WEIGHT_ORDER = ("g_attn", "w_in", "bd", "gq", "gk", "gmq", "conv_w", "g_out", "w_out", "g_ffn",
                 "w_gu", "w_down")

_SMEM_SPEC = pl.BlockSpec(memory_space=pltpu.SMEM)


def _prompt_layer(layer, x, mem_k_t, mem_v_t, sinks, weights, f32_matrices):
    batch, seq, _ = x.shape
    tm = TM_PROMPT
    tps = seq // tm
    n_tiles = batch * tps

    def front(s):
        return jnp.minimum(s, n_tiles - 1)

    def back(s):
        return jnp.maximum(s - 1, 0)

    assert max(_CAST_BLOCKS.values()) <= n_tiles
    casts = [_cast_specs(n, _CAST_BLOCKS[n], lambda s, l: jnp.minimum(l[0] + 1, DEPTH - 1),
                         lambda s, l: s) for n in _MATRIX_ORDER]
    in_specs = [
        _SMEM_SPEC,
        pl.BlockSpec((1, tm, D_MODEL), lambda s, l: (front(s) // tps, front(s) % tps, 0)),
        pl.BlockSpec((1, 1, MEM_WIDTH, N_MEM), lambda s, l: (l[0], front(s) // tps, 0, 0)),
        pl.BlockSpec((1, 1, MEM_WIDTH, N_MEM), lambda s, l: (l[0], front(s) // tps, 0, 0)),
    ] + _weight_specs() + [c[0] for c in casts]
    out_shape = (
        jax.ShapeDtypeStruct((batch, seq, D_MODEL), F32),
        jax.ShapeDtypeStruct((batch, KV_WIDTH, WINDOW), F32),
        jax.ShapeDtypeStruct((batch, KV_WIDTH, WINDOW), F32),
        jax.ShapeDtypeStruct((batch, CONV_W - 1, CONV_DIM), F32),
    ) + tuple(c[2] for c in casts)
    out_specs = (
        pl.BlockSpec((1, tm, D_MODEL), lambda s, l: (back(s) // tps, back(s) % tps, 0)),
        pl.BlockSpec((1, KV_WIDTH, WINDOW), lambda s, l: (front(s) // tps, 0, 0)),
        pl.BlockSpec((1, KV_WIDTH, WINDOW), lambda s, l: (front(s) // tps, 0, 0)),
        pl.BlockSpec((1, CONV_W - 1, CONV_DIM), lambda s, l: (front(s) // tps, 0, 0)),
    ) + tuple(c[1] for c in casts)
    scratch = [
        pltpu.VMEM((KV_HEADS, WINDOW, 2 * HEAD_DIM), F32),
        pltpu.VMEM((KV_WIDTH, WINDOW), F32),
        pltpu.VMEM((8, CONV_DIM), F32),
        pltpu.VMEM((8 + tm, CONV_DIM), F32),
        pltpu.VMEM((tm, ATTN_WIDTH), F32),
        pltpu.VMEM((tm, MEM_WIDTH), F32),
        pltpu.VMEM((tm, MIX_WIDTH), BF),
        pltpu.VMEM((tm, D_MODEL), F32),
    ]
    return pl.pallas_call(
        functools.partial(_prompt_layer_kernel, tiles_per_seq=tps),
        grid_spec=pltpu.PrefetchScalarGridSpec(
            num_scalar_prefetch=1, grid=(n_tiles + 1,), in_specs=in_specs,
            out_specs=out_specs, scratch_shapes=scratch),
        out_shape=out_shape,
        compiler_params=pltpu.CompilerParams(
            dimension_semantics=("arbitrary",),
            vmem_limit_bytes=VMEM_LIMIT_BYTES),
        name="prompt_layer",
    )(layer, sinks, x, mem_k_t, mem_v_t, *[weights[n] for n in _WEIGHT_ORDER],
      *[f32_matrices[n] for n in _MATRIX_ORDER])


_N_SAMPLE_HALF_UNITS = (G_SAMPLE // 2 // SAMPLE_LOCKSTEP) * (
    SAMPLE_LOCKSTEP * (1 + KV_HEADS * 4 + 3) + 1)


def _sample_layer_kernel(layer_ref, sinks_ref, x_ref, ck_ref, cv_ref, cc_ref, mk_ref, mv_ref,
                         g_attn_ref, w_in_ref, bd_ref, gq_ref, gk_ref, gmq_ref, conv_w_ref,
                         g_out_ref, w_out_ref, g_ffn_ref, w_gu_ref, w_down_ref, y_ref, newk_ref,
                         newv_ref, newc_ref, conv_scr, a_scr, mix_scr):
    rows = CHUNK
    layer = layer_ref[0]
    g_out = g_out_ref[0]
    x = x_ref[...]
    qn, kn, v, cb, ccx, mqn = _project_in(x, g_attn_ref[0], w_in_ref[0], bd_ref[...], gq_ref[0],
                                          gk_ref[0], gmq_ref[0])
    q_lo, q_hi = _split_heads_lo_hi(qn)
    grp = _lane_iota(mqn.shape) // HEAD_DIM
    mq_heads = [jnp.where(grp == h, mqn, 0.0).astype(BF) for h in range(MEM_HEADS)]
    conv_w = conv_w_ref[0]
    old_lane = _lane_iota((KV_WIDTH, WINDOW)) < WINDOW - rows

    def shifted_window(cache_t, new_rows):
        new_t = jnp.concatenate([new_rows, new_rows], axis=0).T
        return jnp.where(old_lane, pltpu.roll(cache_t, WINDOW - rows, axis=1), new_t)

    def window_chain(b, j, ck_bf, cv_bf, k_dup_j, v_dup_j):
        r0 = b * rows
        c0 = j * GROUP * HEAD_DIM
        qf = _stack_group_queries(q_lo, q_hi, r0, rows, j)
        kj_t = ck_bf[j * HEAD_DIM:(j + 1) * HEAD_DIM, :]
        vj_t = cv_bf[j * HEAD_DIM:(j + 1) * HEAD_DIM, :]
        sc = jnp.concatenate(
            [_dot(qf, jnp.concatenate([kj_t, kj_t], axis=0)),
             _dot_nt(qf, k_dup_j.astype(BF))], axis=-1)
        yield
        probs = []
        for h in range(GROUP):
            probs.append(_softmax_rows(sc[h * rows:(h + 1) * rows],
                                       sinks_ref[layer, j * GROUP + h]).astype(BF))
            if h % 2 == 1:
                yield
        pm = jnp.concatenate(probs, axis=0)
        vb = v_dup_j.astype(BF)
        o_all = (_dot_nt(pm[:, :WINDOW], jnp.concatenate([vj_t] * GROUP, axis=0))
                 + _dot(pm[:, WINDOW:], jnp.concatenate([vb, vb], axis=-1)))
        a_scr[r0:r0 + rows, c0:c0 + GROUP * HEAD_DIM] = _pick_head_lanes(o_all, rows)
        yield

    def memory_chain(b):
        r0 = b * rows
        mq_blocks = jnp.concatenate([mq_heads[h][r0:r0 + rows] for h in range(MEM_HEADS)], axis=0)
        sc = _dot(mq_blocks, mk_ref[0, b].astype(BF))
        yield
        pm = _softmax_rows(sc, None).astype(BF)
        yield
        mo_b = _pick_head_lanes(_dot_nt(pm, mv_ref[0, b].astype(BF)), rows)
        mix_scr[r0:r0 + rows, _C_END:] = _rms(mo_b, g_out[:, _C_END:]).astype(BF)
        yield

    def mixer_units(b_lo, b_hi):
        for b0 in range(b_lo, b_hi, SAMPLE_LOCKSTEP):
            chains = []
            for b in range(b0, b0 + SAMPLE_LOCKSTEP):
                r0 = b * rows
                ck_t = ck_ref[0, b]
                cv_t = cv_ref[0, b]
                kn_b = kn[r0:r0 + rows]
                v_b = v[r0:r0 + rows]
                newk_ref[b] = shifted_window(ck_t, kn_b)
                newv_ref[b] = shifted_window(cv_t, v_b)
                k_dup = _dup_halves(kn_b)
                v_dup = _dup_halves(v_b)

                ccx_b = ccx[r0:r0 + rows]
                conv_scr[6:8, :] = cc_ref[0, b]
                conv_scr[8:8 + rows, :] = ccx_b
                cy_b = _conv_from_scratch(conv_scr, ccx_b, cb[r0:r0 + rows], conv_w, rows)
                mix_scr[r0:r0 + rows, _A_END:_C_END] = _rms(cy_b,
                                                            g_out[:, _A_END:_C_END]).astype(BF)
                newc_ref[b] = ccx_b[rows - (CONV_W - 1):rows, :]
                yield
                chains += [window_chain(b, j, ck_t.astype(BF), cv_t.astype(BF), k_dup[j], v_dup[j])
                           for j in range(KV_HEADS)]
                chains.append(memory_chain(b))
            yield from _round_robin(chains)
            r0, r1 = b0 * rows, (b0 + SAMPLE_LOCKSTEP) * rows
            mix_scr[r0:r1, :_A_END] = _rms(a_scr[r0:r1, :], g_out[:, :_A_END]).astype(BF)
            yield

    def output_units(r_lo, r_hi):
        h = x[r_lo:r_hi] + _dot(mix_scr[r_lo:r_hi, :], w_out_ref[0])
        hn = _rms(h, g_ffn_ref[0]).astype(BF)
        yield
        y = []
        yield from _swiglu_units(h, hn, w_gu_ref, w_down_ref, y)
        y_ref[r_lo:r_hi, :] = y[0]

    half = G_SAMPLE // 2
    for _ in mixer_units(0, half):
        pass
    _merge(output_units(0, half * rows), [4.0] + _SWIGLU_COSTS,
           mixer_units(half, G_SAMPLE), [1.0] * _N_SAMPLE_HALF_UNITS)
    for _ in output_units(half * rows, G_SAMPLE * rows):
        pass


def _sample_layer(layer, x, cache_k_t, cache_v_t, cache_conv, mem_k_t, mem_v_t, sinks, weights):
    n_b = cache_k_t.shape[1]
    g = G_SAMPLE
    tm = g * CHUNK
    in_specs = [
        _SMEM_SPEC,
        pl.BlockSpec((tm, D_MODEL), lambda i, l: (i, 0)),
        pl.BlockSpec((1, g, KV_WIDTH, WINDOW), lambda i, l: (l[0], i, 0, 0)),
        pl.BlockSpec((1, g, KV_WIDTH, WINDOW), lambda i, l: (l[0], i, 0, 0)),
        pl.BlockSpec((1, g, CONV_W - 1, CONV_DIM), lambda i, l: (l[0], i, 0, 0)),
        pl.BlockSpec((1, g, MEM_WIDTH, N_MEM), lambda i, l: (l[0], i, 0, 0)),
        pl.BlockSpec((1, g, MEM_WIDTH, N_MEM), lambda i, l: (l[0], i, 0, 0)),
    ] + _weight_specs()
    out_shape = (
        jax.ShapeDtypeStruct((n_b * CHUNK, D_MODEL), F32),
        jax.ShapeDtypeStruct((n_b, KV_WIDTH, WINDOW), F32),
        jax.ShapeDtypeStruct((n_b, KV_WIDTH, WINDOW), F32),
        jax.ShapeDtypeStruct((n_b, CONV_W - 1, CONV_DIM), F32),
    )
    out_specs = (
        pl.BlockSpec((tm, D_MODEL), lambda i, l: (i, 0)),
        pl.BlockSpec((g, KV_WIDTH, WINDOW), lambda i, l: (i, 0, 0)),
        pl.BlockSpec((g, KV_WIDTH, WINDOW), lambda i, l: (i, 0, 0)),
        pl.BlockSpec((g, CONV_W - 1, CONV_DIM), lambda i, l: (i, 0, 0)),
    )
    scratch = [
        pltpu.VMEM((8 + CHUNK, CONV_DIM), F32),
        pltpu.VMEM((tm, ATTN_WIDTH), F32),
        pltpu.VMEM((tm, MIX_WIDTH), BF),
    ]
    return pl.pallas_call(
        _sample_layer_kernel,
        grid_spec=pltpu.PrefetchScalarGridSpec(
            num_scalar_prefetch=1, grid=(n_b // g,), in_specs=in_specs, out_specs=out_specs,
            scratch_shapes=scratch),
        out_shape=out_shape,
        compiler_params=pltpu.CompilerParams(
            dimension_semantics=("arbitrary",),
            vmem_limit_bytes=VMEM_LIMIT_BYTES),
        name="sample_layer",
    )(layer, sinks, x, cache_k_t, cache_v_t, cache_conv, mem_k_t, mem_v_t,
      *[weights[n] for n in _WEIGHT_ORDER])


def _mem_kv_kernel(mem_ref, g_mem_ref, w_ref, bd_ref, gmk_ref, w0_in_ref, w0_out_ref, w0_gu_ref,
                   w0_down_ref, mk_ref, mv_ref, mk_tok_ref, w0_in_bf_ref, w0_out_bf_ref,
                   w0_gu_bf_ref, w0_down_bf_ref):
    w = w_ref[0].astype(BF)
    n_half = mk_ref.shape[1]
    first = pl.program_id(1) * n_half
    for b in range(n_half):
        xn = _rms(mem_ref[first + b], g_mem_ref[0]).astype(BF)
        kv = _dot(xn, w)
        mk = _head_rms(kv[:, :MEM_WIDTH], gmk_ref[0], bd_ref[...])
        mk_tok_ref[0, b] = mk
        mk_ref[0, b] = mk.T
        mv_ref[0, b] = kv[:, MEM_WIDTH:].T
    w0_in_bf_ref[...] = w0_in_ref[...].astype(BF)
    w0_out_bf_ref[...] = w0_out_ref[...].astype(BF)
    w0_gu_bf_ref[...] = w0_gu_ref[...].astype(BF)
    w0_down_bf_ref[...] = w0_down_ref[...].astype(BF)


def _mem_kv(mem, g_mem, w_mem_kv, bd, gmk, f32_matrices):
    batch = mem.shape[0]
    out = jax.ShapeDtypeStruct((DEPTH, batch, MEM_WIDTH, N_MEM), F32)
    out_tok = jax.ShapeDtypeStruct((DEPTH, batch, N_MEM, MEM_WIDTH), F32)
    assert batch % 2 == 0
    half = batch // 2
    casts = [_cast_specs(n, 2 * DEPTH, lambda l, i: 0, lambda l, i: 2 * l + i)
             for n in _MATRIX_ORDER]
    return pl.pallas_call(
        _mem_kv_kernel,
        grid=(DEPTH, 2),
        in_specs=[
            pl.BlockSpec((batch, N_MEM, D_MODEL), lambda l, i: (0, 0, 0),
                         pipeline_mode=pl.Buffered(1)),
            pl.BlockSpec((1, 1, D_MODEL), lambda l, i: (l, 0, 0)),
            pl.BlockSpec((1, D_MODEL, 2 * MEM_WIDTH), lambda l, i: (l, 0, 0)),
            pl.BlockSpec((256, 256), lambda l, i: (0, 0)),
            pl.BlockSpec((1, 1, MEM_WIDTH), lambda l, i: (l, 0, 0)),
        ] + [c[0] for c in casts],
        out_specs=(pl.BlockSpec((1, half, MEM_WIDTH, N_MEM), lambda l, i: (l, i, 0, 0)),
                   pl.BlockSpec((1, half, MEM_WIDTH, N_MEM), lambda l, i: (l, i, 0, 0)),
                   pl.BlockSpec((1, half, N_MEM, MEM_WIDTH), lambda l, i: (l, i, 0, 0)))
        + tuple(c[1] for c in casts),
        out_shape=(out, out, out_tok) + tuple(c[2] for c in casts),
        compiler_params=pltpu.CompilerParams(
            dimension_semantics=("arbitrary", "arbitrary"),
            vmem_limit_bytes=VMEM_LIMIT_BYTES),
        name="prompt_mem_kv",
    )(mem, g_mem, w_mem_kv, bd, gmk, *[f32_matrices[n] for n in _MATRIX_ORDER])


def _tile_heads(g, n):
    return jnp.tile(g, (1, n)).reshape(g.shape[0], 1, n * g.shape[1])


def _feature_major(c):
    lead = c.shape[:-3]
    n_tok, heads, dim = c.shape[-3:]
    perm = tuple(range(len(lead))) + (len(lead) + 1, len(lead) + 2, len(lead))
    return jnp.transpose(c, perm).reshape(lead + (heads * dim, n_tok))


def _token_major(c_t, heads):
    lead = c_t.shape[:-2]
    n_tok = c_t.shape[-1]
    c = c_t.reshape(lead + (heads, HEAD_DIM, n_tok))
    perm = tuple(range(len(lead))) + (len(lead) + 2, len(lead), len(lead) + 1)
    return jnp.transpose(c, perm)


def kernel(x_prompt, x_sample, mem_prompt, cache_win_k, cache_win_v, cache_conv, cache_mem_k,
           cache_mem_v, attn_norm_g, w_in, q_norm_g, k_norm_g, sinks, conv_w, mem_norm_g,
           w_mem_kv, mq_norm_g, mk_norm_g, out_norm_g, w_out, ffn_norm_g, w_gate_up, w_down):
    batch, seq, _ = x_prompt.shape
    dec_batch, dec_seq, _ = x_sample.shape
    assert dec_seq == CHUNK and seq % TM_PROMPT == 0 and dec_batch % G_SAMPLE == 0
    assert G_SAMPLE % (2 * SAMPLE_LOCKSTEP) == 0

    head = jnp.arange(256) // HEAD_DIM
    bd = jnp.where(head[:, None] == head[None, :], 1.0 / HEAD_DIM, 0.0).astype(BF)

    f32_matrices = dict(w_in=w_in, w_out=w_out, w_gu=w_gate_up, w_down=w_down)
    small = dict(
        g_attn=attn_norm_g.reshape(DEPTH, 1, D_MODEL), bd=bd,
        gq=_tile_heads(q_norm_g, N_HEADS), gk=_tile_heads(k_norm_g, KV_HEADS),
        gmq=_tile_heads(mq_norm_g, MEM_HEADS), conv_w=conv_w,
        g_out=out_norm_g.reshape(DEPTH, 1, MIX_WIDTH),
        g_ffn=ffn_norm_g.reshape(DEPTH, 1, D_MODEL))

    mk_t, mv_t, mk_tok, *first = _mem_kv(mem_prompt, mem_norm_g.reshape(DEPTH, 1, D_MODEL),
                                         w_mem_kv, bd, _tile_heads(mk_norm_g, MEM_HEADS),
                                         f32_matrices)
    matrices = dict(zip(_MATRIX_ORDER, first))

    ck_t = _feature_major(cache_win_k)
    cv_t = _feature_major(cache_win_v)
    cmk_t = _feature_major(cache_mem_k)
    cmv_t = _feature_major(cache_mem_v)

    yp = x_prompt
    ys = x_sample.reshape(dec_batch * dec_seq, D_MODEL)
    wk_p, wv_p, cv_p, wk_s, wv_s, cv_s = [], [], [], [], [], []
    for l in range(DEPTH):
        layer = jnp.full((1,), l, jnp.int32)
        weights = dict(small, **matrices)
        yp, k_p, v_p, c_p, *next_matrices = _prompt_layer(layer, yp, mk_tok, mv_t, sinks, weights,
                                                          f32_matrices)
        wk_p.append(k_p); wv_p.append(v_p); cv_p.append(c_p)
        ys, k_s, v_s, c_s = _sample_layer(layer, ys, ck_t, cv_t, cache_conv, cmk_t, cmv_t, sinks,
                                          weights)
        wk_s.append(k_s); wv_s.append(v_s); cv_s.append(c_s)
        matrices = dict(zip(_MATRIX_ORDER, next_matrices))

    return (yp, ys.reshape(dec_batch, dec_seq, D_MODEL),
            _token_major(jnp.stack(wk_p), KV_HEADS), _token_major(jnp.stack(wv_p), KV_HEADS),
            jnp.stack(cv_p),
            _token_major(mk_t, MEM_HEADS), _token_major(mv_t, MEM_HEADS),
            _token_major(jnp.stack(wk_s), KV_HEADS), _token_major(jnp.stack(wv_s), KV_HEADS),
            jnp.stack(cv_s))
```

```python
import functools

import jax
import jax.numpy as jnp
from jax import lax
from jax.experimental import pallas as pl
from jax.experimental.pallas import tpu as pltpu

D_MODEL = 1024
DEPTH = 4
CHUNK = 64
HEAD_DIM = 64
N_HEADS = 8
KV_HEADS = 2
GROUP = N_HEADS // KV_HEADS
WINDOW = 128
ATTN_WIDTH = N_HEADS * HEAD_DIM
KV_WIDTH = KV_HEADS * HEAD_DIM
CONV_DIM = 256
CONV_W = 3
MEM_HEADS = 4
MEM_WIDTH = MEM_HEADS * HEAD_DIM
N_MEM = 256
MIX_WIDTH = ATTN_WIDTH + CONV_DIM + MEM_WIDTH
IN_WIDTH = ATTN_WIDTH + 2 * KV_WIDTH + 3 * CONV_DIM + MEM_WIDTH
D_FF = 2816
EPS = 1e-6
ATTN_SCALE = HEAD_DIM ** -0.5
NEG = -1e30

_Q0, _K0, _V0 = 0, ATTN_WIDTH, ATTN_WIDTH + KV_WIDTH
_CB0 = ATTN_WIDTH + 2 * KV_WIDTH
_CC0 = _CB0 + CONV_DIM
_CX0 = _CC0 + CONV_DIM
_MQ0 = _CX0 + CONV_DIM
_A_END = ATTN_WIDTH
_C_END = ATTN_WIDTH + CONV_DIM

PAIR = 2 * CHUNK
TM_PROMPT = 512
G_SAMPLE = 8
SAMPLE_LOCKSTEP = 2
FF_BLOCK = 256
VMEM_LIMIT_BYTES = 56 * 1024 * 1024

BF = jnp.bfloat16
F32 = jnp.float32


def _dot(a, b):
    return jnp.dot(a, b, preferred_element_type=F32)


def _dot_nt(a, b):
    return lax.dot_general(a, b, (((1,), (1,)), ((), ())), preferred_element_type=F32)


def _rms(x, g):
    ms = jnp.mean(x * x, axis=-1, keepdims=True)
    return x * lax.rsqrt(ms + EPS) * g


def _head_rms(z, g, bd):
    w = z.shape[-1]
    ms = _dot((z * z).astype(BF), bd[:w, :w])
    return z * lax.rsqrt(ms + EPS) * g


def _lane_iota(shape):
    return lax.broadcasted_iota(jnp.int32, shape, len(shape) - 1)


def _split_heads_lo_hi(z):
    lo = (_lane_iota(z.shape) % (2 * HEAD_DIM)) < HEAD_DIM
    return jnp.where(lo, z, 0.0).astype(BF), jnp.where(lo, 0.0, z).astype(BF)


def _stack_group_queries(q_lo, q_hi, r0, rows, j):
    c0 = j * GROUP * HEAD_DIM
    return jnp.concatenate(
        [q_lo[r0:r0 + rows, c0:c0 + 128], q_hi[r0:r0 + rows, c0:c0 + 128],
         q_lo[r0:r0 + rows, c0 + 128:c0 + 256], q_hi[r0:r0 + rows, c0 + 128:c0 + 256]], axis=0)


def _dup_halves(z):
    sw = pltpu.roll(z, HEAD_DIM, axis=1)
    lo = _lane_iota(z.shape) < HEAD_DIM
    return jnp.where(lo, z, sw), jnp.where(lo, sw, z)


def _pick_head_lanes(o_all, rows):
    grp = _lane_iota((rows, GROUP * HEAD_DIM)) // HEAD_DIM
    out = o_all[3 * rows:4 * rows]
    for h in (2, 1, 0):
        out = jnp.where(grp == h, o_all[h * rows:(h + 1) * rows], out)
    return out


def _softmax_rows(s, sink):
    m = jnp.max(s, axis=-1, keepdims=True)
    if sink is not None:
        m = jnp.maximum(m, sink)
    p = jnp.exp(s - m)
    den = jnp.sum(p, axis=-1, keepdims=True)
    if sink is not None:
        den = den + jnp.exp(sink - m)
    return p * (1.0 / den)


def _project_in(x, g_attn, w_in, bd, gq, gk, gmq):
    xn = _rms(x, g_attn).astype(BF)
    u = _dot(xn, w_in)
    half = ATTN_WIDTH // 2
    qn = jnp.concatenate(
        [_head_rms(u[:, _Q0:_Q0 + half], gq[:, :half], bd),
         _head_rms(u[:, _Q0 + half:_K0], gq[:, half:], bd)], axis=-1) * ATTN_SCALE
    kn = _head_rms(u[:, _K0:_V0], gk, bd)
    v = u[:, _V0:_CB0]
    cb = u[:, _CB0:_CC0]
    ccx = u[:, _CC0:_CX0] * u[:, _CX0:_MQ0]
    mqn = _head_rms(u[:, _MQ0:], gmq, bd) * ATTN_SCALE
    return qn, kn, v, cb, ccx, mqn


def _conv_from_scratch(conv_scr, ccx, cb, conv_w, rows):
    sh2 = conv_scr[6:6 + rows, :]
    sh1 = conv_scr[7:7 + rows, :]
    return cb * (sh2 * conv_w[0:1, :] + sh1 * conv_w[1:2, :] + ccx * conv_w[2:3, :])


def _swiglu_unit_costs():
    n_blk = D_FF // FF_BLOCK
    costs = []
    for b in range(n_blk):
        costs += [1.0, 1.0]
        if b >= 2 and b % 2 == 0:
            costs.append(2.0)
    for c in range((n_blk - 1) // 2, (n_blk + 1) // 2):
        costs.append(float(min(2 * c + 2, n_blk) - 2 * c))
    return costs


_SWIGLU_COSTS = _swiglu_unit_costs()


def _swiglu_units(h, hn, w_gu_ref, w_down_ref, result):
    blk = FF_BLOCK
    n_blk = D_FF // blk
    acts = []
    y = h

    def down(c):
        lo, hi = 2 * c, min(2 * c + 2, n_blk)
        act = acts[lo] if hi - lo == 1 else jnp.concatenate(acts[lo:hi], axis=-1)
        return _dot(act, w_down_ref[0, lo * blk:hi * blk, :])

    for b in range(n_blk):
        gate = _dot(hn, w_gu_ref[0, :, b * blk:(b + 1) * blk])
        yield
        up = _dot(hn, w_gu_ref[0, :, D_FF + b * blk:D_FF + (b + 1) * blk])
        acts.append((gate * jax.nn.sigmoid(gate) * up).astype(BF))
        yield
        if b >= 2 and b % 2 == 0:
            y = y + down(b // 2 - 1)
            yield
    for c in range((n_blk - 1) // 2, (n_blk + 1) // 2):
        y = y + down(c)
        yield
    result.append(y)


def _merge(first, first_costs, second, second_costs):
    streams = [[first, list(first_costs), 0.0, sum(first_costs)],
               [second, list(second_costs), 0.0, sum(second_costs)]]
    while streams[0][1] or streams[1][1]:
        live = [st for st in streams if st[1]]
        st = min(live, key=lambda st: st[2] / st[3])
        next(st[0])
        st[2] += st[1].pop(0)
        if not st[1]:
            assert next(st[0], "end") == "end"


def _round_robin(chains):
    chains = list(chains)
    while chains:
        for chain in list(chains):
            try:
                next(chain)
            except StopIteration:
                chains.remove(chain)
            else:
                yield


_PROMPT_COSTS = ([2.0, 2.0, 1.5] + [1.0, 0.8, 0.8, 1.0] * ((TM_PROMPT // PAIR) * KV_HEADS) + [3.0]
                 + [0.8, 1.2, 0.8] * MEM_HEADS)


def _prompt_layer_kernel(layer_ref, sinks_ref, x_ref, mk_ref, mv_ref, g_attn_ref, w_in_ref, bd_ref,
                         gq_ref, gk_ref, gmq_ref, conv_w_ref, g_out_ref, w_out_ref, g_ffn_ref,
                         w_gu_ref, w_down_ref, nw_in_ref, nw_out_ref, nw_gu_ref, nw_down_ref,
                         y_ref, newk_ref, newv_ref, newc_ref,
                         nw_in_bf_ref, nw_out_bf_ref, nw_gu_bf_ref, nw_down_bf_ref,
                         kcar_scr, vcar_scr, ccar_scr, conv_scr, a_scr, mo_scr, mix_scr,
                         xres_scr, *,
                         tiles_per_seq):
    tm = TM_PROMPT
    layer = layer_ref[0]
    s = pl.program_id(0)
    n_tiles = pl.num_programs(0) - 1
    t = jnp.minimum(s, n_tiles - 1) % tiles_per_seq

    @pl.when(s == 0)
    def _():
        mix_scr[...] = jnp.zeros(mix_scr.shape, BF)
        xres_scr[...] = jnp.zeros(xres_scr.shape, F32)

    nw_in_bf_ref[...] = nw_in_ref[...].astype(BF)
    nw_out_bf_ref[...] = nw_out_ref[...].astype(BF)
    nw_gu_bf_ref[...] = nw_gu_ref[...].astype(BF)
    nw_down_bf_ref[...] = nw_down_ref[...].astype(BF)

    h_prev = xres_scr[...] + _dot(mix_scr[...], w_out_ref[0])
    hn_prev = _rms(h_prev, g_ffn_ref[0]).astype(BF)
    y_prev = []
    back = _swiglu_units(h_prev, hn_prev, w_gu_ref, w_down_ref, y_prev)

    started = t > 0
    k_prev = [jnp.where(started, kcar_scr[j], 0.0).astype(BF) for j in range(KV_HEADS)]
    v_prev = [jnp.where(started, vcar_scr[j], 0.0).astype(BF) for j in range(KV_HEADS)]
    conv_scr[0:8, :] = jnp.where(started, ccar_scr[...], 0.0)

    x = x_ref[0]
    bd = bd_ref[...]
    u = _dot(_rms(x, g_attn_ref[0]).astype(BF), w_in_ref[0])
    xres_scr[...] = x

    row_chunk = lax.broadcasted_iota(jnp.int32, (PAIR, 2 * PAIR), 0) // CHUNK
    col = _lane_iota((PAIR, 2 * PAIR))
    first_key = jnp.where(started, 0, WINDOW)
    visible = (col >= row_chunk * CHUNK) & (col < (row_chunk + 3) * CHUNK)
    bias_rest = jnp.where(visible, 0.0, NEG)
    bias_first = jnp.where(col >= first_key, bias_rest, NEG)

    mk_t = mk_ref[0, 0].astype(BF)
    mv_t = mv_ref[0, 0].astype(BF)
    g_out = g_out_ref[0]

    kept = {}

    def front_units():
        kn = _head_rms(u[:, _K0:_V0], gk_ref[0], bd)
        v = u[:, _V0:_CB0]
        k_dup = _dup_halves(kn)
        v_dup = _dup_halves(v)
        for j in range(KV_HEADS):
            kcar_scr[j] = k_dup[j][tm - WINDOW:tm, :]
            vcar_scr[j] = v_dup[j][tm - WINDOW:tm, :]
        k_rep = [k_dup[j].astype(BF) for j in range(KV_HEADS)]
        v_rep = [v_dup[j].astype(BF) for j in range(KV_HEADS)]
        kept.update(kn=kn, v=v)
        yield

        gq = gq_ref[0]
        q_half = ATTN_WIDTH // 2
        qn = jnp.concatenate(
            [_head_rms(u[:, _Q0:_Q0 + q_half], gq[:, :q_half], bd),
             _head_rms(u[:, _Q0 + q_half:_K0], gq[:, q_half:], bd)], axis=-1) * ATTN_SCALE
        q_lo, q_hi = _split_heads_lo_hi(qn)
        yield

        ccx = u[:, _CC0:_CX0] * u[:, _CX0:_MQ0]
        conv_scr[8:8 + tm, :] = ccx
        cy = _conv_from_scratch(conv_scr, ccx, u[:, _CB0:_CC0], conv_w_ref[0], tm)
        ccar_scr[...] = ccx[tm - 8:tm, :]
        mix_scr[:, _A_END:_C_END] = _rms(cy, g_out[:, _A_END:_C_END]).astype(BF)
        kept.update(ccx=ccx)
        yield

        for p in range(tm // PAIR):
            r0 = p * PAIR
            bias = bias_first if p == 0 else bias_rest
            for j in range(KV_HEADS):
                c0 = j * GROUP * HEAD_DIM
                qf = _stack_group_queries(q_lo, q_hi, r0, PAIR, j)
                if p == 0:
                    k_win = jnp.concatenate([k_prev[j], k_rep[j][0:PAIR]], axis=0)
                    v_win = jnp.concatenate([v_prev[j], v_rep[j][0:PAIR]], axis=0)
                else:
                    k_win = k_rep[j][r0 - PAIR:r0 + PAIR]
                    v_win = v_rep[j][r0 - PAIR:r0 + PAIR]
                sc = _dot_nt(qf, k_win)
                yield
                probs = []
                for h in range(GROUP):
                    sh = sc[h * PAIR:(h + 1) * PAIR] + bias
                    probs.append(_softmax_rows(sh, sinks_ref[layer, j * GROUP + h]).astype(BF))
                    if h % 2 == 1:
                        yield
                o_all = _dot(jnp.concatenate(probs, axis=0),
                             jnp.concatenate([v_win, v_win], axis=-1))
                a_scr[r0:r0 + PAIR, c0:c0 + GROUP * HEAD_DIM] = _pick_head_lanes(o_all, PAIR)
                yield
        mix_scr[:, :_A_END] = _rms(a_scr[...], g_out[:, :_A_END]).astype(BF)
        mqn = _head_rms(u[:, _MQ0:], gmq_ref[0], bd) * ATTN_SCALE
        grp = _lane_iota(mqn.shape) // HEAD_DIM
        yield
        for h in range(MEM_HEADS):
            mq_h = jnp.where(grp == h, mqn, 0.0).astype(BF)
            sc = _dot(mq_h, mk_t)
            yield
            pm = _softmax_rows(sc, None).astype(BF)
            yield
            o_h = _dot_nt(pm, mv_t)
            mo_scr[:, h * HEAD_DIM:(h + 1) * HEAD_DIM] = o_h[:, h * HEAD_DIM:(h + 1) * HEAD_DIM]
            yield

    def front_then_norm():
        yield from front_units()
        mix_scr[:, _C_END:] = _rms(mo_scr[...], g_out[:, _C_END:]).astype(BF)

    _merge(back, _SWIGLU_COSTS, front_then_norm(), _PROMPT_COSTS)
    y_ref[0] = y_prev[0]

    @pl.when(t == tiles_per_seq - 1)
    def _():
        newk_ref[0] = kept["kn"][tm - WINDOW:tm, :].T
        newv_ref[0] = kept["v"][tm - WINDOW:tm, :].T
        newc_ref[0] = kept["ccx"][tm - (CONV_W - 1):tm, :]


def _layer_spec(shape):
    zeros = (0,) * len(shape)
    return pl.BlockSpec((1,) + shape, lambda *args: (args[-1][0],) + zeros,
                        pipeline_mode=pl.Buffered(1))


def _matrix_spec(shape):
    return pl.BlockSpec((1,) + shape, lambda *args: (0, 0, 0), pipeline_mode=pl.Buffered(1))


_MATRIX_SHAPES = dict(w_in=(D_MODEL, IN_WIDTH), w_out=(MIX_WIDTH, D_MODEL),
                      w_gu=(D_MODEL, 2 * D_FF), w_down=(D_FF, D_MODEL))
_MATRIX_ORDER = ("w_in", "w_out", "w_gu", "w_down")
_CAST_BLOCKS = dict(w_in=32, w_out=32, w_gu=32, w_down=16)


def _cast_specs(name, n_blk, layer_of, step_of):
    rows, cols = _MATRIX_SHAPES[name]
    assert rows % (16 * n_blk) == 0
    blk = (1, rows // n_blk, cols)
    src = pl.BlockSpec(blk, lambda *a: (layer_of(*a), jnp.minimum(step_of(*a), n_blk - 1), 0))
    dst = pl.BlockSpec(blk, lambda *a: (0, jnp.minimum(step_of(*a), n_blk - 1), 0))
    return src, dst, jax.ShapeDtypeStruct((1, rows, cols), BF)


def _weight_specs():
    return [
        _layer_spec((1, D_MODEL)),
        _matrix_spec(_MATRIX_SHAPES["w_in"]),
        pl.BlockSpec((256, 256), lambda *args: (0, 0), pipeline_mode=pl.Buffered(1)),
        _layer_spec((1, ATTN_WIDTH)),
        _layer_spec((1, KV_WIDTH)),
        _layer_spec((1, MEM_WIDTH)),
        _layer_spec((CONV_W, CONV_DIM)),
        _layer_spec((1, MIX_WIDTH)),
        _matrix_spec(_MATRIX_SHAPES["w_out"]),
        _layer_spec((1, D_MODEL)),
        _matrix_spec(_MATRIX_SHAPES["w_gu"]),
        _matrix_spec(_MATRIX_SHAPES["w_down"]),
    ]


_WEIGHT_ORDER = ("g_attn", "w_in", "bd", "gq", "gk", "gmq", "conv_w", "g_out", "w_out", "g_ffn",
                 "w_gu", "w_down")

_SMEM_SPEC = pl.BlockSpec(memory_space=pltpu.SMEM)


def _prompt_layer(layer, x, mem_k_t, mem_v_t, sinks, weights, f32_matrices):
    batch, seq, _ = x.shape
    tm = TM_PROMPT
    tps = seq // tm
    n_tiles = batch * tps

    def front(s):
        return jnp.minimum(s, n_tiles - 1)

    def back(s):
        return jnp.maximum(s - 1, 0)

    assert max(_CAST_BLOCKS.values()) <= n_tiles
    casts = [_cast_specs(n, _CAST_BLOCKS[n], lambda s, l: jnp.minimum(l[0] + 1, DEPTH - 1),
                         lambda s, l: s) for n in _MATRIX_ORDER]
    in_specs = [
        _SMEM_SPEC,
        pl.BlockSpec((1, tm, D_MODEL), lambda s, l: (front(s) // tps, front(s) % tps, 0)),
        pl.BlockSpec((1, 1, MEM_WIDTH, N_MEM), lambda s, l: (l[0], front(s) // tps, 0, 0)),
        pl.BlockSpec((1, 1, MEM_WIDTH, N_MEM), lambda s, l: (l[0], front(s) // tps, 0, 0)),
    ] + _weight_specs() + [c[0] for c in casts]
    out_shape = (
        jax.ShapeDtypeStruct((batch, seq, D_MODEL), F32),
        jax.ShapeDtypeStruct((batch, KV_WIDTH, WINDOW), F32),
        jax.ShapeDtypeStruct((batch, KV_WIDTH, WINDOW), F32),
        jax.ShapeDtypeStruct((batch, CONV_W - 1, CONV_DIM), F32),
    ) + tuple(c[2] for c in casts)
    out_specs = (
        pl.BlockSpec((1, tm, D_MODEL), lambda s, l: (back(s) // tps, back(s) % tps, 0)),
        pl.BlockSpec((1, KV_WIDTH, WINDOW), lambda s, l: (front(s) // tps, 0, 0)),
        pl.BlockSpec((1, KV_WIDTH, WINDOW), lambda s, l: (front(s) // tps, 0, 0)),
        pl.BlockSpec((1, CONV_W - 1, CONV_DIM), lambda s, l: (front(s) // tps, 0, 0)),
    ) + tuple(c[1] for c in casts)
    scratch = [
        pltpu.VMEM((KV_HEADS, WINDOW, 2 * HEAD_DIM), F32),
        pltpu.VMEM((KV_HEADS, WINDOW, 2 * HEAD_DIM), F32),
        pltpu.VMEM((8, CONV_DIM), F32),
        pltpu.VMEM((8 + tm, CONV_DIM), F32),
        pltpu.VMEM((tm, ATTN_WIDTH), F32),
        pltpu.VMEM((tm, MEM_WIDTH), F32),
        pltpu.VMEM((tm, MIX_WIDTH), BF),
        pltpu.VMEM((tm, D_MODEL), F32),
    ]
    return pl.pallas_call(
        functools.partial(_prompt_layer_kernel, tiles_per_seq=tps),
        grid_spec=pltpu.PrefetchScalarGridSpec(
            num_scalar_prefetch=1, grid=(n_tiles + 1,), in_specs=in_specs,
            out_specs=out_specs, scratch_shapes=scratch),
        out_shape=out_shape,
        compiler_params=pltpu.CompilerParams(
            dimension_semantics=("arbitrary",),
            vmem_limit_bytes=VMEM_LIMIT_BYTES),
        name="prompt_layer",
    )(layer, sinks, x, mem_k_t, mem_v_t, *[weights[n] for n in _WEIGHT_ORDER],
      *[f32_matrices[n] for n in _MATRIX_ORDER])


_N_SAMPLE_HALF_UNITS = (G_SAMPLE // 2 // SAMPLE_LOCKSTEP) * (
    SAMPLE_LOCKSTEP * (1 + KV_HEADS * 4 + 3) + 1)


def _sample_layer_kernel(layer_ref, sinks_ref, x_ref, ck_ref, cv_ref, cc_ref, mk_ref, mv_ref,
                         g_attn_ref, w_in_ref, bd_ref, gq_ref, gk_ref, gmq_ref, conv_w_ref,
                         g_out_ref, w_out_ref, g_ffn_ref, w_gu_ref, w_down_ref, y_ref, newk_ref,
                         newv_ref, newc_ref, conv_scr, a_scr, mix_scr):
    rows = CHUNK
    layer = layer_ref[0]
    g_out = g_out_ref[0]
    x = x_ref[...]
    qn, kn, v, cb, ccx, mqn = _project_in(x, g_attn_ref[0], w_in_ref[0], bd_ref[...], gq_ref[0],
                                          gk_ref[0], gmq_ref[0])
    q_lo, q_hi = _split_heads_lo_hi(qn)
    grp = _lane_iota(mqn.shape) // HEAD_DIM
    mq_heads = [jnp.where(grp == h, mqn, 0.0).astype(BF) for h in range(MEM_HEADS)]
    conv_w = conv_w_ref[0]
    old_lane = _lane_iota((KV_WIDTH, WINDOW)) < WINDOW - rows

    def shifted_window(cache_t, new_rows):
        new_t = jnp.concatenate([new_rows, new_rows], axis=0).T
        return jnp.where(old_lane, pltpu.roll(cache_t, WINDOW - rows, axis=1), new_t)

    def window_chain(b, j, ck_bf, cv_bf, k_dup_j, v_dup_j):
        r0 = b * rows
        c0 = j * GROUP * HEAD_DIM
        qf = _stack_group_queries(q_lo, q_hi, r0, rows, j)
        kj_t = ck_bf[j * HEAD_DIM:(j + 1) * HEAD_DIM, :]
        vj_t = cv_bf[j * HEAD_DIM:(j + 1) * HEAD_DIM, :]
        sc = jnp.concatenate(
            [_dot(qf, jnp.concatenate([kj_t, kj_t], axis=0)),
             _dot_nt(qf, k_dup_j.astype(BF))], axis=-1)
        yield
        probs = []
        for h in range(GROUP):
            probs.append(_softmax_rows(sc[h * rows:(h + 1) * rows],
                                       sinks_ref[layer, j * GROUP + h]).astype(BF))
            if h % 2 == 1:
                yield
        pm = jnp.concatenate(probs, axis=0)
        vb = v_dup_j.astype(BF)
        o_all = (_dot_nt(pm[:, :WINDOW], jnp.concatenate([vj_t] * GROUP, axis=0))
                 + _dot(pm[:, WINDOW:], jnp.concatenate([vb, vb], axis=-1)))
        a_scr[r0:r0 + rows, c0:c0 + GROUP * HEAD_DIM] = _pick_head_lanes(o_all, rows)
        yield

    def memory_chain(b):
        r0 = b * rows
        mq_blocks = jnp.concatenate([mq_heads[h][r0:r0 + rows] for h in range(MEM_HEADS)], axis=0)
        sc = _dot(mq_blocks, mk_ref[0, b].astype(BF))
        yield
        pm = _softmax_rows(sc, None).astype(BF)
        yield
        mo_b = _pick_head_lanes(_dot_nt(pm, mv_ref[0, b].astype(BF)), rows)
        mix_scr[r0:r0 + rows, _C_END:] = _rms(mo_b, g_out[:, _C_END:]).astype(BF)
        yield

    def mixer_units(b_lo, b_hi):
        for b0 in range(b_lo, b_hi, SAMPLE_LOCKSTEP):
            chains = []
            for b in range(b0, b0 + SAMPLE_LOCKSTEP):
                r0 = b * rows
                ck_t = ck_ref[0, b]
                cv_t = cv_ref[0, b]
                kn_b = kn[r0:r0 + rows]
                v_b = v[r0:r0 + rows]
                newk_ref[b] = shifted_window(ck_t, kn_b)
                newv_ref[b] = shifted_window(cv_t, v_b)
                k_dup = _dup_halves(kn_b)
                v_dup = _dup_halves(v_b)

                ccx_b = ccx[r0:r0 + rows]
                conv_scr[6:8, :] = cc_ref[0, b]
                conv_scr[8:8 + rows, :] = ccx_b
                cy_b = _conv_from_scratch(conv_scr, ccx_b, cb[r0:r0 + rows], conv_w, rows)
                mix_scr[r0:r0 + rows, _A_END:_C_END] = _rms(cy_b,
                                                            g_out[:, _A_END:_C_END]).astype(BF)
                newc_ref[b] = ccx_b[rows - (CONV_W - 1):rows, :]
                yield
                chains += [window_chain(b, j, ck_t.astype(BF), cv_t.astype(BF), k_dup[j], v_dup[j])
                           for j in range(KV_HEADS)]
                chains.append(memory_chain(b))
            yield from _round_robin(chains)
            r0, r1 = b0 * rows, (b0 + SAMPLE_LOCKSTEP) * rows
            mix_scr[r0:r1, :_A_END] = _rms(a_scr[r0:r1, :], g_out[:, :_A_END]).astype(BF)
            yield

    def output_units(r_lo, r_hi):
        h = x[r_lo:r_hi] + _dot(mix_scr[r_lo:r_hi, :], w_out_ref[0])
        hn = _rms(h, g_ffn_ref[0]).astype(BF)
        yield
        y = []
        yield from _swiglu_units(h, hn, w_gu_ref, w_down_ref, y)
        y_ref[r_lo:r_hi, :] = y[0]

    half = G_SAMPLE // 2
    for _ in mixer_units(0, half):
        pass
    _merge(output_units(0, half * rows), [4.0] + _SWIGLU_COSTS,
           mixer_units(half, G_SAMPLE), [1.0] * _N_SAMPLE_HALF_UNITS)
    for _ in output_units(half * rows, G_SAMPLE * rows):
        pass


def _sample_layer(layer, x, cache_k_t, cache_v_t, cache_conv, mem_k_t, mem_v_t, sinks, weights):
    n_b = cache_k_t.shape[1]
    g = G_SAMPLE
    tm = g * CHUNK
    in_specs = [
        _SMEM_SPEC,
        pl.BlockSpec((tm, D_MODEL), lambda i, l: (i, 0)),
        pl.BlockSpec((1, g, KV_WIDTH, WINDOW), lambda i, l: (l[0], i, 0, 0)),
        pl.BlockSpec((1, g, KV_WIDTH, WINDOW), lambda i, l: (l[0], i, 0, 0)),
        pl.BlockSpec((1, g, CONV_W - 1, CONV_DIM), lambda i, l: (l[0], i, 0, 0)),
        pl.BlockSpec((1, g, MEM_WIDTH, N_MEM), lambda i, l: (l[0], i, 0, 0)),
        pl.BlockSpec((1, g, MEM_WIDTH, N_MEM), lambda i, l: (l[0], i, 0, 0)),
    ] + _weight_specs()
    out_shape = (
        jax.ShapeDtypeStruct((n_b * CHUNK, D_MODEL), F32),
        jax.ShapeDtypeStruct((n_b, KV_WIDTH, WINDOW), F32),
        jax.ShapeDtypeStruct((n_b, KV_WIDTH, WINDOW), F32),
        jax.ShapeDtypeStruct((n_b, CONV_W - 1, CONV_DIM), F32),
    )
    out_specs = (
        pl.BlockSpec((tm, D_MODEL), lambda i, l: (i, 0)),
        pl.BlockSpec((g, KV_WIDTH, WINDOW), lambda i, l: (i, 0, 0)),
        pl.BlockSpec((g, KV_WIDTH, WINDOW), lambda i, l: (i, 0, 0)),
        pl.BlockSpec((g, CONV_W - 1, CONV_DIM), lambda i, l: (i, 0, 0)),
    )
    scratch = [
        pltpu.VMEM((8 + CHUNK, CONV_DIM), F32),
        pltpu.VMEM((tm, ATTN_WIDTH), F32),
        pltpu.VMEM((tm, MIX_WIDTH), BF),
    ]
    return pl.pallas_call(
        _sample_layer_kernel,
        grid_spec=pltpu.PrefetchScalarGridSpec(
            num_scalar_prefetch=1, grid=(n_b // g,), in_specs=in_specs, out_specs=out_specs,
            scratch_shapes=scratch),
        out_shape=out_shape,
        compiler_params=pltpu.CompilerParams(
            dimension_semantics=("arbitrary",),
            vmem_limit_bytes=VMEM_LIMIT_BYTES),
        name="sample_layer",
    )(layer, sinks, x, cache_k_t, cache_v_t, cache_conv, mem_k_t, mem_v_t,
      *[weights[n] for n in _WEIGHT_ORDER])


def _mem_kv_kernel(mem_ref, g_mem_ref, w_ref, bd_ref, gmk_ref, w0_in_ref, w0_out_ref, w0_gu_ref,
                   w0_down_ref, mk_ref, mv_ref, w0_in_bf_ref, w0_out_bf_ref, w0_gu_bf_ref,
                   w0_down_bf_ref):
    w = w_ref[0].astype(BF)
    for b in range(mem_ref.shape[0]):
        xn = _rms(mem_ref[b], g_mem_ref[0]).astype(BF)
        kv = _dot(xn, w)
        mk_ref[0, b] = _head_rms(kv[:, :MEM_WIDTH], gmk_ref[0], bd_ref[...]).T
        mv_ref[0, b] = kv[:, MEM_WIDTH:].T
    w0_in_bf_ref[...] = w0_in_ref[...].astype(BF)
    w0_out_bf_ref[...] = w0_out_ref[...].astype(BF)
    w0_gu_bf_ref[...] = w0_gu_ref[...].astype(BF)
    w0_down_bf_ref[...] = w0_down_ref[...].astype(BF)


def _mem_kv(mem, g_mem, w_mem_kv, bd, gmk, f32_matrices):
    batch = mem.shape[0]
    out = jax.ShapeDtypeStruct((DEPTH, batch, MEM_WIDTH, N_MEM), F32)
    casts = [_cast_specs(n, DEPTH, lambda l: 0, lambda l: l) for n in _MATRIX_ORDER]
    return pl.pallas_call(
        _mem_kv_kernel,
        grid=(DEPTH,),
        in_specs=[
            pl.BlockSpec((batch, N_MEM, D_MODEL), lambda l: (0, 0, 0),
                         pipeline_mode=pl.Buffered(1)),
            pl.BlockSpec((1, 1, D_MODEL), lambda l: (l, 0, 0)),
            pl.BlockSpec((1, D_MODEL, 2 * MEM_WIDTH), lambda l: (l, 0, 0)),
            pl.BlockSpec((256, 256), lambda l: (0, 0)),
            pl.BlockSpec((1, 1, MEM_WIDTH), lambda l: (l, 0, 0)),
        ] + [c[0] for c in casts],
        out_specs=(pl.BlockSpec((1, batch, MEM_WIDTH, N_MEM), lambda l: (l, 0, 0, 0)),
                   pl.BlockSpec((1, batch, MEM_WIDTH, N_MEM), lambda l: (l, 0, 0, 0)))
        + tuple(c[1] for c in casts),
        out_shape=(out, out) + tuple(c[2] for c in casts),
        compiler_params=pltpu.CompilerParams(
            dimension_semantics=("arbitrary",),
            vmem_limit_bytes=VMEM_LIMIT_BYTES),
        name="prompt_mem_kv",
    )(mem, g_mem, w_mem_kv, bd, gmk, *[f32_matrices[n] for n in _MATRIX_ORDER])


def _tile_heads(g, n):
    return jnp.tile(g, (1, n)).reshape(g.shape[0], 1, n * g.shape[1])


def _feature_major(c):
    lead = c.shape[:-3]
    n_tok, heads, dim = c.shape[-3:]
    perm = tuple(range(len(lead))) + (len(lead) + 1, len(lead) + 2, len(lead))
    return jnp.transpose(c, perm).reshape(lead + (heads * dim, n_tok))


def _token_major(c_t, heads):
    lead = c_t.shape[:-2]
    n_tok = c_t.shape[-1]
    c = c_t.reshape(lead + (heads, HEAD_DIM, n_tok))
    perm = tuple(range(len(lead))) + (len(lead) + 2, len(lead), len(lead) + 1)
    return jnp.transpose(c, perm)


def kernel(x_prompt, x_sample, mem_prompt, cache_win_k, cache_win_v, cache_conv, cache_mem_k,
           cache_mem_v, attn_norm_g, w_in, q_norm_g, k_norm_g, sinks, conv_w, mem_norm_g,
           w_mem_kv, mq_norm_g, mk_norm_g, out_norm_g, w_out, ffn_norm_g, w_gate_up, w_down):
    batch, seq, _ = x_prompt.shape
    dec_batch, dec_seq, _ = x_sample.shape
    assert dec_seq == CHUNK and seq % TM_PROMPT == 0 and dec_batch % G_SAMPLE == 0
    assert G_SAMPLE % (2 * SAMPLE_LOCKSTEP) == 0

    head = jnp.arange(256) // HEAD_DIM
    bd = jnp.where(head[:, None] == head[None, :], 1.0 / HEAD_DIM, 0.0).astype(BF)

    f32_matrices = dict(w_in=w_in, w_out=w_out, w_gu=w_gate_up, w_down=w_down)
    small = dict(
        g_attn=attn_norm_g.reshape(DEPTH, 1, D_MODEL), bd=bd,
        gq=_tile_heads(q_norm_g, N_HEADS), gk=_tile_heads(k_norm_g, KV_HEADS),
        gmq=_tile_heads(mq_norm_g, MEM_HEADS), conv_w=conv_w,
        g_out=out_norm_g.reshape(DEPTH, 1, MIX_WIDTH),
        g_ffn=ffn_norm_g.reshape(DEPTH, 1, D_MODEL))

    mk_t, mv_t, *first = _mem_kv(mem_prompt, mem_norm_g.reshape(DEPTH, 1, D_MODEL), w_mem_kv, bd,
                                 _tile_heads(mk_norm_g, MEM_HEADS), f32_matrices)
    matrices = dict(zip(_MATRIX_ORDER, first))

    ck_t = _feature_major(cache_win_k)
    cv_t = _feature_major(cache_win_v)
    cmk_t = _feature_major(cache_mem_k)
    cmv_t = _feature_major(cache_mem_v)

    yp = x_prompt
    ys = x_sample.reshape(dec_batch * dec_seq, D_MODEL)
    wk_p, wv_p, cv_p, wk_s, wv_s, cv_s = [], [], [], [], [], []
    for l in range(DEPTH):
        layer = jnp.full((1,), l, jnp.int32)
        weights = dict(small, **matrices)
        yp, k_p, v_p, c_p, *next_matrices = _prompt_layer(layer, yp, mk_t, mv_t, sinks, weights,
                                                          f32_matrices)
        wk_p.append(k_p); wv_p.append(v_p); cv_p.append(c_p)
        ys, k_s, v_s, c_s = _sample_layer(layer, ys, ck_t, cv_t, cache_conv, cmk_t, cmv_t, sinks,
                                          weights)
        wk_s.append(k_s); wv_s.append(v_s); cv_s.append(c_s)
        matrices = dict(zip(_MATRIX_ORDER, next_matrices))

    return (yp, ys.reshape(dec_batch, dec_seq, D_MODEL),
            _token_major(jnp.stack(wk_p), KV_HEADS), _token_major(jnp.stack(wv_p), KV_HEADS),
            jnp.stack(cv_p),
            _token_major(mk_t, MEM_HEADS), _token_major(mv_t, MEM_HEADS),
            _token_major(jnp.stack(wk_s), KV_HEADS), _token_major(jnp.stack(wv_s), KV_HEADS),
            jnp.stack(cv_s))
```

```python
import functools

import jax
import jax.numpy as jnp
from jax import lax
from jax.experimental import pallas as pl
from jax.experimental.pallas import tpu as pltpu

D_MODEL = 1024
DEPTH = 4
CHUNK = 64
HEAD_DIM = 64
N_HEADS = 8
KV_HEADS = 2
GROUP = N_HEADS // KV_HEADS
WINDOW = 128
ATTN_WIDTH = N_HEADS * HEAD_DIM
KV_WIDTH = KV_HEADS * HEAD_DIM
CONV_DIM = 256
CONV_W = 3
MEM_HEADS = 4
MEM_WIDTH = MEM_HEADS * HEAD_DIM
N_MEM = 256
MIX_WIDTH = ATTN_WIDTH + CONV_DIM + MEM_WIDTH
IN_WIDTH = ATTN_WIDTH + 2 * KV_WIDTH + 3 * CONV_DIM + MEM_WIDTH
D_FF = 2816
EPS = 1e-6
ATTN_SCALE = HEAD_DIM ** -0.5
NEG = -1e30

_Q0, _K0, _V0 = 0, ATTN_WIDTH, ATTN_WIDTH + KV_WIDTH
_CB0 = ATTN_WIDTH + 2 * KV_WIDTH
_CC0 = _CB0 + CONV_DIM
_CX0 = _CC0 + CONV_DIM
_MQ0 = _CX0 + CONV_DIM
_A_END = ATTN_WIDTH
_C_END = ATTN_WIDTH + CONV_DIM

PAIR = 2 * CHUNK
TM_PROMPT = 512
G_SAMPLE = 8
SAMPLE_LOCKSTEP = 2
FF_BLOCK = 256
PROJ_BLOCK = 256
VMEM_LIMIT_BYTES = 56 * 1024 * 1024

BF = jnp.bfloat16
F32 = jnp.float32


def _dot(a, b):
    return jnp.dot(a, b, preferred_element_type=F32)


def _dot_nt(a, b):
    return lax.dot_general(a, b, (((1,), (1,)), ((), ())), preferred_element_type=F32)


def _rms(x, g):
    ms = jnp.mean(x * x, axis=-1, keepdims=True)
    return x * lax.rsqrt(ms + EPS) * g


def _head_rms(z, g, bd):
    w = z.shape[-1]
    ms = _dot((z * z).astype(BF), bd[:w, :w])
    return z * lax.rsqrt(ms + EPS) * g


def _lane_iota(shape):
    return lax.broadcasted_iota(jnp.int32, shape, len(shape) - 1)


def _split_heads_lo_hi(z):
    lo = (_lane_iota(z.shape) % (2 * HEAD_DIM)) < HEAD_DIM
    return jnp.where(lo, z, 0.0).astype(BF), jnp.where(lo, 0.0, z).astype(BF)


def _stack_group_queries(q_lo, q_hi, r0, rows, j):
    c0 = j * GROUP * HEAD_DIM
    return jnp.concatenate(
        [q_lo[r0:r0 + rows, c0:c0 + 128], q_hi[r0:r0 + rows, c0:c0 + 128],
         q_lo[r0:r0 + rows, c0 + 128:c0 + 256], q_hi[r0:r0 + rows, c0 + 128:c0 + 256]], axis=0)


def _dup_halves(z):
    sw = pltpu.roll(z, HEAD_DIM, axis=1)
    lo = _lane_iota(z.shape) < HEAD_DIM
    return jnp.where(lo, z, sw), jnp.where(lo, sw, z)


def _pick_head_lanes(o_all, rows):
    grp = _lane_iota((rows, GROUP * HEAD_DIM)) // HEAD_DIM
    out = o_all[3 * rows:4 * rows]
    for h in (2, 1, 0):
        out = jnp.where(grp == h, o_all[h * rows:(h + 1) * rows], out)
    return out


def _softmax_rows(s, sink):
    m = jnp.max(s, axis=-1, keepdims=True)
    if sink is not None:
        m = jnp.maximum(m, sink)
    p = jnp.exp(s - m)
    den = jnp.sum(p, axis=-1, keepdims=True)
    if sink is not None:
        den = den + jnp.exp(sink - m)
    return p * (1.0 / den)


def _finish_projection(u, bd, gq, gk, gmq):
    half = ATTN_WIDTH // 2
    qn = jnp.concatenate(
        [_head_rms(u[:, _Q0:_Q0 + half], gq[:, :half], bd),
         _head_rms(u[:, _Q0 + half:_K0], gq[:, half:], bd)], axis=-1) * ATTN_SCALE
    kn = _head_rms(u[:, _K0:_V0], gk, bd)
    v = u[:, _V0:_CB0]
    cb = u[:, _CB0:_CC0]
    ccx = u[:, _CC0:_CX0] * u[:, _CX0:_MQ0]
    mqn = _head_rms(u[:, _MQ0:], gmq, bd) * ATTN_SCALE
    return qn, kn, v, cb, ccx, mqn


def _conv_from_scratch(conv_scr, ccx, cb, conv_w, rows):
    sh2 = conv_scr[6:6 + rows, :]
    sh1 = conv_scr[7:7 + rows, :]
    return cb * (sh2 * conv_w[0:1, :] + sh1 * conv_w[1:2, :] + ccx * conv_w[2:3, :])


def _swiglu_unit_costs():
    n_blk = D_FF // FF_BLOCK
    costs = []
    for b in range(n_blk):
        costs += [1.0, 1.0]
        if b >= 2 and b % 2 == 0:
            costs.append(2.0)
    for c in range((n_blk - 1) // 2, (n_blk + 1) // 2):
        costs.append(float(min(2 * c + 2, n_blk) - 2 * c))
    return costs


_SWIGLU_COSTS = _swiglu_unit_costs()


def _swiglu_units(h, hn, w_gu_ref, w_down_ref, result):
    blk = FF_BLOCK
    n_blk = D_FF // blk
    acts = []
    y = h

    def down(c):
        lo, hi = 2 * c, min(2 * c + 2, n_blk)
        act = acts[lo] if hi - lo == 1 else jnp.concatenate(acts[lo:hi], axis=-1)
        return _dot(act, w_down_ref[0, lo * blk:hi * blk, :])

    for b in range(n_blk):
        gate = _dot(hn, w_gu_ref[0, :, b * blk:(b + 1) * blk])
        yield
        up = _dot(hn, w_gu_ref[0, :, D_FF + b * blk:D_FF + (b + 1) * blk])
        acts.append((gate * jax.nn.sigmoid(gate) * up).astype(BF))
        yield
        if b >= 2 and b % 2 == 0:
            y = y + down(b // 2 - 1)
            yield
    for c in range((n_blk - 1) // 2, (n_blk + 1) // 2):
        y = y + down(c)
        yield
    result.append(y)


def _merge(first, first_costs, second, second_costs):
    streams = [[first, list(first_costs), 0.0, sum(first_costs)],
               [second, list(second_costs), 0.0, sum(second_costs)]]
    while streams[0][1] or streams[1][1]:
        live = [st for st in streams if st[1]]
        st = min(live, key=lambda st: st[2] / st[3])
        next(st[0])
        st[2] += st[1].pop(0)
        if not st[1]:
            assert next(st[0], "end") == "end"


def _round_robin(chains):
    chains = list(chains)
    while chains:
        for chain in list(chains):
            try:
                next(chain)
            except StopIteration:
                chains.remove(chain)
            else:
                yield


_PROMPT_COSTS = ([2.0, 2.0, 1.5] + [1.0, 0.8, 0.8, 1.0] * ((TM_PROMPT // PAIR) * KV_HEADS) + [3.0]
                 + [0.8, 1.2, 0.8] * MEM_HEADS)


def _prompt_layer_kernel(layer_ref, sinks_ref, x_ref, mk_ref, mv_ref, g_attn_ref, w_in_ref, bd_ref,
                         gq_ref, gk_ref, gmq_ref, conv_w_ref, g_out_ref, w_out_ref, g_ffn_ref,
                         w_gu_ref, w_down_ref, nw_in_ref, nw_out_ref, nw_gu_ref, nw_down_ref,
                         y_ref, newk_ref, newv_ref, newc_ref,
                         nw_in_bf_ref, nw_out_bf_ref, nw_gu_bf_ref, nw_down_bf_ref,
                         kcar_scr, vcar_scr, ccar_scr, conv_scr, a_scr, mo_scr, mix_scr,
                         xres_scr, *,
                         tiles_per_seq):
    tm = TM_PROMPT
    layer = layer_ref[0]
    s = pl.program_id(0)
    n_tiles = pl.num_programs(0) - 1
    t = jnp.minimum(s, n_tiles - 1) % tiles_per_seq

    @pl.when(s == 0)
    def _():
        mix_scr[...] = jnp.zeros(mix_scr.shape, BF)
        xres_scr[...] = jnp.zeros(xres_scr.shape, F32)

    nw_in_bf_ref[...] = nw_in_ref[...].astype(BF)
    nw_out_bf_ref[...] = nw_out_ref[...].astype(BF)
    nw_gu_bf_ref[...] = nw_gu_ref[...].astype(BF)
    nw_down_bf_ref[...] = nw_down_ref[...].astype(BF)

    h_prev = xres_scr[...] + _dot(mix_scr[...], w_out_ref[0])
    hn_prev = _rms(h_prev, g_ffn_ref[0]).astype(BF)
    y_prev = []
    back = _swiglu_units(h_prev, hn_prev, w_gu_ref, w_down_ref, y_prev)

    started = t > 0
    k_prev = [jnp.where(started, kcar_scr[j], 0.0).astype(BF) for j in range(KV_HEADS)]
    v_prev = [jnp.where(started, vcar_scr[j], 0.0).astype(BF) for j in range(KV_HEADS)]
    conv_scr[0:8, :] = jnp.where(started, ccar_scr[...], 0.0)

    x = x_ref[0]
    bd = bd_ref[...]
    u = _dot(_rms(x, g_attn_ref[0]).astype(BF), w_in_ref[0])
    xres_scr[...] = x

    row_chunk = lax.broadcasted_iota(jnp.int32, (PAIR, 2 * PAIR), 0) // CHUNK
    col = _lane_iota((PAIR, 2 * PAIR))
    first_key = jnp.where(started, 0, WINDOW)
    visible = (col >= row_chunk * CHUNK) & (col < (row_chunk + 3) * CHUNK)
    bias_rest = jnp.where(visible, 0.0, NEG)
    bias_first = jnp.where(col >= first_key, bias_rest, NEG)

    mk_t = mk_ref[0, 0].astype(BF)
    mv_t = mv_ref[0, 0].astype(BF)
    g_out = g_out_ref[0]

    kept = {}

    def front_units():
        kn = _head_rms(u[:, _K0:_V0], gk_ref[0], bd)
        v = u[:, _V0:_CB0]
        k_dup = _dup_halves(kn)
        v_dup = _dup_halves(v)
        for j in range(KV_HEADS):
            kcar_scr[j] = k_dup[j][tm - WINDOW:tm, :]
            vcar_scr[j] = v_dup[j][tm - WINDOW:tm, :]
        k_rep = [k_dup[j].astype(BF) for j in range(KV_HEADS)]
        v_rep = [v_dup[j].astype(BF) for j in range(KV_HEADS)]
        kept.update(kn=kn, v=v)
        yield

        gq = gq_ref[0]
        q_half = ATTN_WIDTH // 2
        qn = jnp.concatenate(
            [_head_rms(u[:, _Q0:_Q0 + q_half], gq[:, :q_half], bd),
             _head_rms(u[:, _Q0 + q_half:_K0], gq[:, q_half:], bd)], axis=-1) * ATTN_SCALE
        q_lo, q_hi = _split_heads_lo_hi(qn)
        yield

        ccx = u[:, _CC0:_CX0] * u[:, _CX0:_MQ0]
        conv_scr[8:8 + tm, :] = ccx
        cy = _conv_from_scratch(conv_scr, ccx, u[:, _CB0:_CC0], conv_w_ref[0], tm)
        ccar_scr[...] = ccx[tm - 8:tm, :]
        mix_scr[:, _A_END:_C_END] = _rms(cy, g_out[:, _A_END:_C_END]).astype(BF)
        kept.update(ccx=ccx)
        yield

        for p in range(tm // PAIR):
            r0 = p * PAIR
            bias = bias_first if p == 0 else bias_rest
            for j in range(KV_HEADS):
                c0 = j * GROUP * HEAD_DIM
                qf = _stack_group_queries(q_lo, q_hi, r0, PAIR, j)
                if p == 0:
                    k_win = jnp.concatenate([k_prev[j], k_rep[j][0:PAIR]], axis=0)
                    v_win = jnp.concatenate([v_prev[j], v_rep[j][0:PAIR]], axis=0)
                else:
                    k_win = k_rep[j][r0 - PAIR:r0 + PAIR]
                    v_win = v_rep[j][r0 - PAIR:r0 + PAIR]
                sc = _dot_nt(qf, k_win)
                yield
                probs = []
                for h in range(GROUP):
                    sh = sc[h * PAIR:(h + 1) * PAIR] + bias
                    probs.append(_softmax_rows(sh, sinks_ref[layer, j * GROUP + h]).astype(BF))
                    if h % 2 == 1:
                        yield
                o_all = _dot(jnp.concatenate(probs, axis=0),
                             jnp.concatenate([v_win, v_win], axis=-1))
                a_scr[r0:r0 + PAIR, c0:c0 + GROUP * HEAD_DIM] = _pick_head_lanes(o_all, PAIR)
                yield
        mix_scr[:, :_A_END] = _rms(a_scr[...], g_out[:, :_A_END]).astype(BF)
        mqn = _head_rms(u[:, _MQ0:], gmq_ref[0], bd) * ATTN_SCALE
        grp = _lane_iota(mqn.shape) // HEAD_DIM
        yield
        for h in range(MEM_HEADS):
            mq_h = jnp.where(grp == h, mqn, 0.0).astype(BF)
            sc = _dot(mq_h, mk_t)
            yield
            pm = _softmax_rows(sc, None).astype(BF)
            yield
            o_h = _dot_nt(pm, mv_t)
            mo_scr[:, h * HEAD_DIM:(h + 1) * HEAD_DIM] = o_h[:, h * HEAD_DIM:(h + 1) * HEAD_DIM]
            yield

    def front_then_norm():
        yield from front_units()
        mix_scr[:, _C_END:] = _rms(mo_scr[...], g_out[:, _C_END:]).astype(BF)

    _merge(back, _SWIGLU_COSTS, front_then_norm(), _PROMPT_COSTS)
    y_ref[0] = y_prev[0]

    @pl.when(t == tiles_per_seq - 1)
    def _():
        newk_ref[0] = kept["kn"][tm - WINDOW:tm, :].T
        newv_ref[0] = kept["v"][tm - WINDOW:tm, :].T
        newc_ref[0] = kept["ccx"][tm - (CONV_W - 1):tm, :]


def _layer_spec(shape):
    zeros = (0,) * len(shape)
    return pl.BlockSpec((1,) + shape, lambda *args: (args[-1][0],) + zeros,
                        pipeline_mode=pl.Buffered(1))


def _matrix_spec(shape):
    return pl.BlockSpec((1,) + shape, lambda *args: (0, 0, 0), pipeline_mode=pl.Buffered(1))


_MATRIX_SHAPES = dict(w_in=(D_MODEL, IN_WIDTH), w_out=(MIX_WIDTH, D_MODEL),
                      w_gu=(D_MODEL, 2 * D_FF), w_down=(D_FF, D_MODEL))
_MATRIX_ORDER = ("w_in", "w_out", "w_gu", "w_down")
_CAST_BLOCKS = dict(w_in=32, w_out=32, w_gu=32, w_down=16)


def _cast_specs(name, n_blk, layer_of, step_of):
    rows, cols = _MATRIX_SHAPES[name]
    assert rows % (16 * n_blk) == 0
    blk = (1, rows // n_blk, cols)
    src = pl.BlockSpec(blk, lambda *a: (layer_of(*a), jnp.minimum(step_of(*a), n_blk - 1), 0))
    dst = pl.BlockSpec(blk, lambda *a: (0, jnp.minimum(step_of(*a), n_blk - 1), 0))
    return src, dst, jax.ShapeDtypeStruct((1, rows, cols), BF)


def _weight_specs():
    return [
        _layer_spec((1, D_MODEL)),
        _matrix_spec(_MATRIX_SHAPES["w_in"]),
        pl.BlockSpec((256, 256), lambda *args: (0, 0), pipeline_mode=pl.Buffered(1)),
        _layer_spec((1, ATTN_WIDTH)),
        _layer_spec((1, KV_WIDTH)),
        _layer_spec((1, MEM_WIDTH)),
        _layer_spec((CONV_W, CONV_DIM)),
        _layer_spec((1, MIX_WIDTH)),
        _matrix_spec(_MATRIX_SHAPES["w_out"]),
        _layer_spec((1, D_MODEL)),
        _matrix_spec(_MATRIX_SHAPES["w_gu"]),
        _matrix_spec(_MATRIX_SHAPES["w_down"]),
    ]


_WEIGHT_ORDER = ("g_attn", "w_in", "bd", "gq", "gk", "gmq", "conv_w", "g_out", "w_out", "g_ffn",
                 "w_gu", "w_down")

_SMEM_SPEC = pl.BlockSpec(memory_space=pltpu.SMEM)


def _prompt_layer(layer, x, mem_k_t, mem_v_t, sinks, weights, f32_matrices):
    batch, seq, _ = x.shape
    tm = TM_PROMPT
    tps = seq // tm
    n_tiles = batch * tps

    def front(s):
        return jnp.minimum(s, n_tiles - 1)

    def back(s):
        return jnp.maximum(s - 1, 0)

    assert max(_CAST_BLOCKS.values()) <= n_tiles
    casts = [_cast_specs(n, _CAST_BLOCKS[n], lambda s, l: jnp.minimum(l[0] + 1, DEPTH - 1),
                         lambda s, l: s) for n in _MATRIX_ORDER]
    in_specs = [
        _SMEM_SPEC,
        pl.BlockSpec((1, tm, D_MODEL), lambda s, l: (front(s) // tps, front(s) % tps, 0)),
        pl.BlockSpec((1, 1, MEM_WIDTH, N_MEM), lambda s, l: (l[0], front(s) // tps, 0, 0)),
        pl.BlockSpec((1, 1, MEM_WIDTH, N_MEM), lambda s, l: (l[0], front(s) // tps, 0, 0)),
    ] + _weight_specs() + [c[0] for c in casts]
    out_shape = (
        jax.ShapeDtypeStruct((batch, seq, D_MODEL), F32),
        jax.ShapeDtypeStruct((batch, KV_WIDTH, WINDOW), F32),
        jax.ShapeDtypeStruct((batch, KV_WIDTH, WINDOW), F32),
        jax.ShapeDtypeStruct((batch, CONV_W - 1, CONV_DIM), F32),
    ) + tuple(c[2] for c in casts)
    out_specs = (
        pl.BlockSpec((1, tm, D_MODEL), lambda s, l: (back(s) // tps, back(s) % tps, 0)),
        pl.BlockSpec((1, KV_WIDTH, WINDOW), lambda s, l: (front(s) // tps, 0, 0)),
        pl.BlockSpec((1, KV_WIDTH, WINDOW), lambda s, l: (front(s) // tps, 0, 0)),
        pl.BlockSpec((1, CONV_W - 1, CONV_DIM), lambda s, l: (front(s) // tps, 0, 0)),
    ) + tuple(c[1] for c in casts)
    scratch = [
        pltpu.VMEM((KV_HEADS, WINDOW, 2 * HEAD_DIM), F32),
        pltpu.VMEM((KV_HEADS, WINDOW, 2 * HEAD_DIM), F32),
        pltpu.VMEM((8, CONV_DIM), F32),
        pltpu.VMEM((8 + tm, CONV_DIM), F32),
        pltpu.VMEM((tm, ATTN_WIDTH), F32),
        pltpu.VMEM((tm, MEM_WIDTH), F32),
        pltpu.VMEM((tm, MIX_WIDTH), BF),
        pltpu.VMEM((tm, D_MODEL), F32),
    ]
    return pl.pallas_call(
        functools.partial(_prompt_layer_kernel, tiles_per_seq=tps),
        grid_spec=pltpu.PrefetchScalarGridSpec(
            num_scalar_prefetch=1, grid=(n_tiles + 1,), in_specs=in_specs,
            out_specs=out_specs, scratch_shapes=scratch),
        out_shape=out_shape,
        compiler_params=pltpu.CompilerParams(
            dimension_semantics=("arbitrary",),
            vmem_limit_bytes=VMEM_LIMIT_BYTES),
        name="prompt_layer",
    )(layer, sinks, x, mem_k_t, mem_v_t, *[weights[n] for n in _WEIGHT_ORDER],
      *[f32_matrices[n] for n in _MATRIX_ORDER])


_N_SAMPLE_HALF_UNITS = (G_SAMPLE // 2 // SAMPLE_LOCKSTEP) * (
    SAMPLE_LOCKSTEP * (1 + KV_HEADS * 4 + 3) + 1)


def _sample_layer_kernel(layer_ref, sinks_ref, x_ref, ck_ref, cv_ref, cc_ref, mk_ref, mv_ref,
                         g_attn_ref, w_in_ref, bd_ref, gq_ref, gk_ref, gmq_ref, conv_w_ref,
                         g_out_ref, w_out_ref, g_ffn_ref, w_gu_ref, w_down_ref, y_ref, newk_ref,
                         newv_ref, newc_ref, conv_scr, a_scr, mix_scr):
    rows = CHUNK
    half = G_SAMPLE // 2
    half_rows = half * rows
    layer = layer_ref[0]
    g_out = g_out_ref[0]
    x = x_ref[...]
    conv_w = conv_w_ref[0]
    old_lane = _lane_iota((KV_WIDTH, WINDOW)) < WINDOW - rows

    def project_units(hh, proj):
        xn = _rms(x[hh * half_rows:(hh + 1) * half_rows], g_attn_ref[0]).astype(BF)
        blocks = []
        for c0 in range(0, IN_WIDTH, PROJ_BLOCK):
            blocks.append(_dot(xn, w_in_ref[0, :, c0:c0 + PROJ_BLOCK]))
            yield
        qn, kn, v, cb, ccx, mqn = _finish_projection(
            jnp.concatenate(blocks, axis=-1), bd_ref[...], gq_ref[0], gk_ref[0], gmq_ref[0])
        q_lo, q_hi = _split_heads_lo_hi(qn)
        grp = _lane_iota(mqn.shape) // HEAD_DIM
        mq_heads = [jnp.where(grp == h, mqn, 0.0).astype(BF) for h in range(MEM_HEADS)]
        proj.update(q_lo=q_lo, q_hi=q_hi, kn=kn, v=v, cb=cb, ccx=ccx, mq_heads=mq_heads)
        yield

    def shifted_window(cache_t, new_t):
        return jnp.where(old_lane, pltpu.roll(cache_t, WINDOW - rows, axis=1), new_t)

    repeat_bias = jnp.where(_lane_iota((1, 2 * WINDOW)) < WINDOW + rows, 0.0, NEG)

    def window_chain(proj, b, j, kk_bf, vv_bf):
        r0 = b * rows
        c0 = j * GROUP * HEAD_DIM
        qf = _stack_group_queries(proj["q_lo"], proj["q_hi"], r0 % half_rows, rows, j)
        kj_t = kk_bf[j * HEAD_DIM:(j + 1) * HEAD_DIM, :]
        vj_t = vv_bf[j * HEAD_DIM:(j + 1) * HEAD_DIM, :]
        sc = _dot(qf, jnp.concatenate([kj_t, kj_t], axis=0)) + repeat_bias
        yield
        probs = []
        for h in range(GROUP):
            probs.append(_softmax_rows(sc[h * rows:(h + 1) * rows],
                                       sinks_ref[layer, j * GROUP + h]).astype(BF))
            if h % 2 == 1:
                yield
        o_all = _dot_nt(jnp.concatenate(probs, axis=0), jnp.concatenate([vj_t] * GROUP, axis=0))
        a_scr[r0:r0 + rows, c0:c0 + GROUP * HEAD_DIM] = _pick_head_lanes(o_all, rows)
        yield

    def memory_chain(proj, b):
        r0 = b * rows
        q0 = r0 % half_rows
        mq_blocks = jnp.concatenate(
            [proj["mq_heads"][h][q0:q0 + rows] for h in range(MEM_HEADS)], axis=0)
        sc = _dot(mq_blocks, mk_ref[0, b].astype(BF))
        yield
        pm = _softmax_rows(sc, None).astype(BF)
        yield
        mo_b = _pick_head_lanes(_dot_nt(pm, mv_ref[0, b].astype(BF)), rows)
        mix_scr[r0:r0 + rows, _C_END:] = _rms(mo_b, g_out[:, _C_END:]).astype(BF)
        yield

    def mixer_units(hh, proj):
        for b0 in range(hh * half, (hh + 1) * half, SAMPLE_LOCKSTEP):
            chains = []
            for b in range(b0, b0 + SAMPLE_LOCKSTEP):
                r0 = b * rows
                q0 = r0 % half_rows
                ck_t = ck_ref[0, b]
                cv_t = cv_ref[0, b]
                kn_b = proj["kn"][q0:q0 + rows]
                v_b = proj["v"][q0:q0 + rows]
                kn_t = jnp.concatenate([kn_b, kn_b], axis=0).T
                v_t = jnp.concatenate([v_b, v_b], axis=0).T
                newk_ref[b] = shifted_window(ck_t, kn_t)
                newv_ref[b] = shifted_window(cv_t, v_t)
                kk_bf = jnp.concatenate([ck_t, kn_t], axis=1).astype(BF)
                vv_bf = jnp.concatenate([cv_t, v_t], axis=1).astype(BF)

                ccx_b = proj["ccx"][q0:q0 + rows]
                conv_scr[6:8, :] = cc_ref[0, b]
                conv_scr[8:8 + rows, :] = ccx_b
                cy_b = _conv_from_scratch(conv_scr, ccx_b, proj["cb"][q0:q0 + rows], conv_w, rows)
                mix_scr[r0:r0 + rows, _A_END:_C_END] = _rms(cy_b,
                                                            g_out[:, _A_END:_C_END]).astype(BF)
                newc_ref[b] = ccx_b[rows - (CONV_W - 1):rows, :]
                yield
                chains += [window_chain(proj, b, j, kk_bf, vv_bf) for j in range(KV_HEADS)]
                chains.append(memory_chain(proj, b))
            yield from _round_robin(chains)
            r0, r1 = b0 * rows, (b0 + SAMPLE_LOCKSTEP) * rows
            mix_scr[r0:r1, :_A_END] = _rms(a_scr[r0:r1, :], g_out[:, :_A_END]).astype(BF)
            yield

    def output_units(r_lo, r_hi):
        h = x[r_lo:r_hi] + _dot(mix_scr[r_lo:r_hi, :], w_out_ref[0])
        hn = _rms(h, g_ffn_ref[0]).astype(BF)
        yield
        y = []
        yield from _swiglu_units(h, hn, w_gu_ref, w_down_ref, y)
        y_ref[r_lo:r_hi, :] = y[0]

    n_proj_units = IN_WIDTH // PROJ_BLOCK + 1
    proj_a, proj_b = {}, {}
    for _ in project_units(0, proj_a):
        pass
    _merge(project_units(1, proj_b), [1.0] * n_proj_units,
           mixer_units(0, proj_a), [1.0] * _N_SAMPLE_HALF_UNITS)
    _merge(output_units(0, half_rows), [4.0] + _SWIGLU_COSTS,
           mixer_units(1, proj_b), [1.0] * _N_SAMPLE_HALF_UNITS)
    for _ in output_units(half_rows, 2 * half_rows):
        pass


def _sample_layer(layer, x, cache_k_t, cache_v_t, cache_conv, mem_k_t, mem_v_t, sinks, weights):
    n_b = cache_k_t.shape[1]
    g = G_SAMPLE
    tm = g * CHUNK
    in_specs = [
        _SMEM_SPEC,
        pl.BlockSpec((tm, D_MODEL), lambda i, l: (i, 0)),
        pl.BlockSpec((1, g, KV_WIDTH, WINDOW), lambda i, l: (l[0], i, 0, 0)),
        pl.BlockSpec((1, g, KV_WIDTH, WINDOW), lambda i, l: (l[0], i, 0, 0)),
        pl.BlockSpec((1, g, CONV_W - 1, CONV_DIM), lambda i, l: (l[0], i, 0, 0)),
        pl.BlockSpec((1, g, MEM_WIDTH, N_MEM), lambda i, l: (l[0], i, 0, 0)),
        pl.BlockSpec((1, g, MEM_WIDTH, N_MEM), lambda i, l: (l[0], i, 0, 0)),
    ] + _weight_specs()
    out_shape = (
        jax.ShapeDtypeStruct((n_b * CHUNK, D_MODEL), F32),
        jax.ShapeDtypeStruct((n_b, KV_WIDTH, WINDOW), F32),
        jax.ShapeDtypeStruct((n_b, KV_WIDTH, WINDOW), F32),
        jax.ShapeDtypeStruct((n_b, CONV_W - 1, CONV_DIM), F32),
    )
    out_specs = (
        pl.BlockSpec((tm, D_MODEL), lambda i, l: (i, 0)),
        pl.BlockSpec((g, KV_WIDTH, WINDOW), lambda i, l: (i, 0, 0)),
        pl.BlockSpec((g, KV_WIDTH, WINDOW), lambda i, l: (i, 0, 0)),
        pl.BlockSpec((g, CONV_W - 1, CONV_DIM), lambda i, l: (i, 0, 0)),
    )
    scratch = [
        pltpu.VMEM((8 + CHUNK, CONV_DIM), F32),
        pltpu.VMEM((tm, ATTN_WIDTH), F32),
        pltpu.VMEM((tm, MIX_WIDTH), BF),
    ]
    return pl.pallas_call(
        _sample_layer_kernel,
        grid_spec=pltpu.PrefetchScalarGridSpec(
            num_scalar_prefetch=1, grid=(n_b // g,), in_specs=in_specs, out_specs=out_specs,
            scratch_shapes=scratch),
        out_shape=out_shape,
        compiler_params=pltpu.CompilerParams(
            dimension_semantics=("arbitrary",),
            vmem_limit_bytes=VMEM_LIMIT_BYTES),
        name="sample_layer",
    )(layer, sinks, x, cache_k_t, cache_v_t, cache_conv, mem_k_t, mem_v_t,
      *[weights[n] for n in _WEIGHT_ORDER])


def _mem_kv_kernel(mem_ref, g_mem_ref, w_ref, bd_ref, gmk_ref, w0_in_ref, w0_out_ref, w0_gu_ref,
                   w0_down_ref, mk_ref, mv_ref, w0_in_bf_ref, w0_out_bf_ref, w0_gu_bf_ref,
                   w0_down_bf_ref):
    w = w_ref[0].astype(BF)
    for b in range(mem_ref.shape[0]):
        xn = _rms(mem_ref[b], g_mem_ref[0]).astype(BF)
        kv = _dot(xn, w)
        mk_ref[0, b] = _head_rms(kv[:, :MEM_WIDTH], gmk_ref[0], bd_ref[...]).T
        mv_ref[0, b] = kv[:, MEM_WIDTH:].T
    w0_in_bf_ref[...] = w0_in_ref[...].astype(BF)
    w0_out_bf_ref[...] = w0_out_ref[...].astype(BF)
    w0_gu_bf_ref[...] = w0_gu_ref[...].astype(BF)
    w0_down_bf_ref[...] = w0_down_ref[...].astype(BF)


def _mem_kv(mem, g_mem, w_mem_kv, bd, gmk, f32_matrices):
    batch = mem.shape[0]
    out = jax.ShapeDtypeStruct((DEPTH, batch, MEM_WIDTH, N_MEM), F32)
    casts = [_cast_specs(n, DEPTH, lambda l: 0, lambda l: l) for n in _MATRIX_ORDER]
    return pl.pallas_call(
        _mem_kv_kernel,
        grid=(DEPTH,),
        in_specs=[
            pl.BlockSpec((batch, N_MEM, D_MODEL), lambda l: (0, 0, 0),
                         pipeline_mode=pl.Buffered(1)),
            pl.BlockSpec((1, 1, D_MODEL), lambda l: (l, 0, 0)),
            pl.BlockSpec((1, D_MODEL, 2 * MEM_WIDTH), lambda l: (l, 0, 0)),
            pl.BlockSpec((256, 256), lambda l: (0, 0)),
            pl.BlockSpec((1, 1, MEM_WIDTH), lambda l: (l, 0, 0)),
        ] + [c[0] for c in casts],
        out_specs=(pl.BlockSpec((1, batch, MEM_WIDTH, N_MEM), lambda l: (l, 0, 0, 0)),
                   pl.BlockSpec((1, batch, MEM_WIDTH, N_MEM), lambda l: (l, 0, 0, 0)))
        + tuple(c[1] for c in casts),
        out_shape=(out, out) + tuple(c[2] for c in casts),
        compiler_params=pltpu.CompilerParams(
            dimension_semantics=("arbitrary",),
            vmem_limit_bytes=VMEM_LIMIT_BYTES),
        name="prompt_mem_kv",
    )(mem, g_mem, w_mem_kv, bd, gmk, *[f32_matrices[n] for n in _MATRIX_ORDER])


def _tile_heads(g, n):
    return jnp.tile(g, (1, n)).reshape(g.shape[0], 1, n * g.shape[1])


def _feature_major(c):
    lead = c.shape[:-3]
    n_tok, heads, dim = c.shape[-3:]
    perm = tuple(range(len(lead))) + (len(lead) + 1, len(lead) + 2, len(lead))
    return jnp.transpose(c, perm).reshape(lead + (heads * dim, n_tok))


def _token_major(c_t, heads):
    lead = c_t.shape[:-2]
    n_tok = c_t.shape[-1]
    c = c_t.reshape(lead + (heads, HEAD_DIM, n_tok))
    perm = tuple(range(len(lead))) + (len(lead) + 2, len(lead), len(lead) + 1)
    return jnp.transpose(c, perm)


def kernel(x_prompt, x_sample, mem_prompt, cache_win_k, cache_win_v, cache_conv, cache_mem_k,
           cache_mem_v, attn_norm_g, w_in, q_norm_g, k_norm_g, sinks, conv_w, mem_norm_g,
           w_mem_kv, mq_norm_g, mk_norm_g, out_norm_g, w_out, ffn_norm_g, w_gate_up, w_down):
    batch, seq, _ = x_prompt.shape
    dec_batch, dec_seq, _ = x_sample.shape
    assert dec_seq == CHUNK and seq % TM_PROMPT == 0 and dec_batch % G_SAMPLE == 0
    assert G_SAMPLE % (2 * SAMPLE_LOCKSTEP) == 0

    head = jnp.arange(256) // HEAD_DIM
    bd = jnp.where(head[:, None] == head[None, :], 1.0 / HEAD_DIM, 0.0).astype(BF)

    f32_matrices = dict(w_in=w_in, w_out=w_out, w_gu=w_gate_up, w_down=w_down)
    small = dict(
        g_attn=attn_norm_g.reshape(DEPTH, 1, D_MODEL), bd=bd,
        gq=_tile_heads(q_norm_g, N_HEADS), gk=_tile_heads(k_norm_g, KV_HEADS),
        gmq=_tile_heads(mq_norm_g, MEM_HEADS), conv_w=conv_w,
        g_out=out_norm_g.reshape(DEPTH, 1, MIX_WIDTH),
        g_ffn=ffn_norm_g.reshape(DEPTH, 1, D_MODEL))

    mk_t, mv_t, *first = _mem_kv(mem_prompt, mem_norm_g.reshape(DEPTH, 1, D_MODEL), w_mem_kv, bd,
                                 _tile_heads(mk_norm_g, MEM_HEADS), f32_matrices)
    matrices = dict(zip(_MATRIX_ORDER, first))

    ck_t = _feature_major(cache_win_k)
    cv_t = _feature_major(cache_win_v)
    cmk_t = _feature_major(cache_mem_k)
    cmv_t = _feature_major(cache_mem_v)

    yp = x_prompt
    ys = x_sample.reshape(dec_batch * dec_seq, D_MODEL)
    wk_p, wv_p, cv_p, wk_s, wv_s, cv_s = [], [], [], [], [], []
    for l in range(DEPTH):
        layer = jnp.full((1,), l, jnp.int32)
        weights = dict(small, **matrices)
        yp, k_p, v_p, c_p, *next_matrices = _prompt_layer(layer, yp, mk_t, mv_t, sinks, weights,
                                                          f32_matrices)
        wk_p.append(k_p); wv_p.append(v_p); cv_p.append(c_p)
        ys, k_s, v_s, c_s = _sample_layer(layer, ys, ck_t, cv_t, cache_conv, cmk_t, cmv_t, sinks,
                                          weights)
        wk_s.append(k_s); wv_s.append(v_s); cv_s.append(c_s)
        matrices = dict(zip(_MATRIX_ORDER, next_matrices))

    return (yp, ys.reshape(dec_batch, dec_seq, D_MODEL),
            _token_major(jnp.stack(wk_p), KV_HEADS), _token_major(jnp.stack(wv_p), KV_HEADS),
            jnp.stack(cv_p),
            _token_major(mk_t, MEM_HEADS), _token_major(mv_t, MEM_HEADS),
            _token_major(jnp.stack(wk_s), KV_HEADS), _token_major(jnp.stack(wv_s), KV_HEADS),
            jnp.stack(cv_s))
```

```python
import functools

import jax
import jax.numpy as jnp
from jax import lax
from jax.experimental import pallas as pl
from jax.experimental.pallas import tpu as pltpu

D_MODEL = 1024
DEPTH = 4
CHUNK = 64
HEAD_DIM = 64
N_HEADS = 8
KV_HEADS = 2
GROUP = N_HEADS // KV_HEADS
WINDOW = 128
ATTN_WIDTH = N_HEADS * HEAD_DIM
KV_WIDTH = KV_HEADS * HEAD_DIM
CONV_DIM = 256
CONV_W = 3
MEM_HEADS = 4
MEM_WIDTH = MEM_HEADS * HEAD_DIM
N_MEM = 256
MIX_WIDTH = ATTN_WIDTH + CONV_DIM + MEM_WIDTH
IN_WIDTH = ATTN_WIDTH + 2 * KV_WIDTH + 3 * CONV_DIM + MEM_WIDTH
D_FF = 2816
EPS = 1e-6
ATTN_SCALE = HEAD_DIM ** -0.5
NEG = -1e30

_Q0, _K0, _V0 = 0, ATTN_WIDTH, ATTN_WIDTH + KV_WIDTH
_CB0 = ATTN_WIDTH + 2 * KV_WIDTH
_CC0 = _CB0 + CONV_DIM
_CX0 = _CC0 + CONV_DIM
_MQ0 = _CX0 + CONV_DIM
_A_END = ATTN_WIDTH
_C_END = ATTN_WIDTH + CONV_DIM

PAIR = 2 * CHUNK
TM_PROMPT = 512
G_SAMPLE = 8
SAMPLE_LOCKSTEP = 2
FF_BLOCK = 256
DOWN_SPLIT = 4
PROJ_BLOCK = 256
VMEM_LIMIT_BYTES = 56 * 1024 * 1024

BF = jnp.bfloat16
F32 = jnp.float32


def _dot(a, b):
    return jnp.dot(a, b, preferred_element_type=F32)


def _dot_nt(a, b):
    return lax.dot_general(a, b, (((1,), (1,)), ((), ())), preferred_element_type=F32)


def _rms(x, g):
    ms = jnp.mean(x * x, axis=-1, keepdims=True)
    return x * lax.rsqrt(ms + EPS) * g


def _head_rms(z, g, bd):
    w = z.shape[-1]
    ms = _dot((z * z).astype(BF), bd[:w, :w])
    return z * lax.rsqrt(ms + EPS) * g


def _lane_iota(shape):
    return lax.broadcasted_iota(jnp.int32, shape, len(shape) - 1)


def _split_heads_lo_hi(z):
    lo = (_lane_iota(z.shape) % (2 * HEAD_DIM)) < HEAD_DIM
    return jnp.where(lo, z, 0.0).astype(BF), jnp.where(lo, 0.0, z).astype(BF)


def _stack_group_queries(q_lo, q_hi, r0, rows, j):
    c0 = j * GROUP * HEAD_DIM
    return jnp.concatenate(
        [q_lo[r0:r0 + rows, c0:c0 + 128], q_hi[r0:r0 + rows, c0:c0 + 128],
         q_lo[r0:r0 + rows, c0 + 128:c0 + 256], q_hi[r0:r0 + rows, c0 + 128:c0 + 256]], axis=0)


def _dup_halves(z):
    sw = pltpu.roll(z, HEAD_DIM, axis=1)
    lo = _lane_iota(z.shape) < HEAD_DIM
    return jnp.where(lo, z, sw), jnp.where(lo, sw, z)


def _pick_head_lanes(o_all, rows):
    grp = _lane_iota((rows, GROUP * HEAD_DIM)) // HEAD_DIM
    out = o_all[3 * rows:4 * rows]
    for h in (2, 1, 0):
        out = jnp.where(grp == h, o_all[h * rows:(h + 1) * rows], out)
    return out


def _softmax_rows(s, sink):
    m = jnp.max(s, axis=-1, keepdims=True)
    if sink is not None:
        m = jnp.maximum(m, sink)
    p = jnp.exp(s - m)
    den = jnp.sum(p, axis=-1, keepdims=True)
    if sink is not None:
        den = den + jnp.exp(sink - m)
    return p * (1.0 / den)


def _finish_projection(u, bd, gq, gk, gmq):
    half = ATTN_WIDTH // 2
    qn = jnp.concatenate(
        [_head_rms(u[:, _Q0:_Q0 + half], gq[:, :half], bd),
         _head_rms(u[:, _Q0 + half:_K0], gq[:, half:], bd)], axis=-1) * ATTN_SCALE
    kn = _head_rms(u[:, _K0:_V0], gk, bd)
    v = u[:, _V0:_CB0]
    cb = u[:, _CB0:_CC0]
    ccx = u[:, _CC0:_CX0] * u[:, _CX0:_MQ0]
    mqn = _head_rms(u[:, _MQ0:], gmq, bd) * ATTN_SCALE
    return qn, kn, v, cb, ccx, mqn


def _conv_from_scratch(conv_scr, ccx, cb, conv_w, rows):
    sh2 = conv_scr[6:6 + rows, :]
    sh1 = conv_scr[7:7 + rows, :]
    return cb * (sh2 * conv_w[0:1, :] + sh1 * conv_w[1:2, :] + ccx * conv_w[2:3, :])


def _swiglu_unit_costs():
    n_blk = D_FF // FF_BLOCK
    costs = []
    for b in range(n_blk):
        costs += [1.0, 1.0]
        if b >= 2 and b % 2 == 0:
            costs += [2.0 / DOWN_SPLIT] * DOWN_SPLIT
    for c in range((n_blk - 1) // 2, (n_blk + 1) // 2):
        costs += [float(min(2 * c + 2, n_blk) - 2 * c) / DOWN_SPLIT] * DOWN_SPLIT
    return costs


_SWIGLU_COSTS = _swiglu_unit_costs()


def _swiglu_units(h, hn, w_gu_ref, w_down_ref, store):
    blk = FF_BLOCK
    n_blk = D_FF // blk
    n_down = (n_blk + 1) // 2
    width = D_MODEL // DOWN_SPLIT
    acts = []
    y = [h[:, k * width:(k + 1) * width] for k in range(DOWN_SPLIT)]

    def down_units(c):
        lo, hi = 2 * c, min(2 * c + 2, n_blk)
        act = acts[lo] if hi - lo == 1 else jnp.concatenate(acts[lo:hi], axis=-1)
        for k in range(DOWN_SPLIT):
            y[k] = y[k] + _dot(act, w_down_ref[0, lo * blk:hi * blk, k * width:(k + 1) * width])
            if c == n_down - 1:
                store(k * width, y[k])
            yield

    for b in range(n_blk):
        gate = _dot(hn, w_gu_ref[0, :, b * blk:(b + 1) * blk])
        yield
        up = _dot(hn, w_gu_ref[0, :, D_FF + b * blk:D_FF + (b + 1) * blk])
        acts.append((gate * jax.nn.sigmoid(gate) * up).astype(BF))
        yield
        if b >= 2 and b % 2 == 0:
            yield from down_units(b // 2 - 1)
    for c in range((n_blk - 1) // 2, n_down):
        yield from down_units(c)


def _merge(first, first_costs, second, second_costs):
    streams = [[first, list(first_costs), 0.0, sum(first_costs)],
               [second, list(second_costs), 0.0, sum(second_costs)]]
    while streams[0][1] or streams[1][1]:
        live = [st for st in streams if st[1]]
        st = min(live, key=lambda st: st[2] / st[3])
        next(st[0])
        st[2] += st[1].pop(0)
        if not st[1]:
            assert next(st[0], "end") == "end"


def _round_robin(chains):
    chains = list(chains)
    while chains:
        for chain in list(chains):
            try:
                next(chain)
            except StopIteration:
                chains.remove(chain)
            else:
                yield


_PROMPT_COSTS = ([2.0, 2.0, 1.5] + [1.0, 0.8, 0.8, 1.0] * ((TM_PROMPT // PAIR) * KV_HEADS) + [3.0]
                 + [0.8, 1.2, 0.8] * MEM_HEADS)


def _prompt_layer_kernel(layer_ref, sinks_ref, x_ref, mk_ref, mv_ref, g_attn_ref, w_in_ref, bd_ref,
                         gq_ref, gk_ref, gmq_ref, conv_w_ref, g_out_ref, w_out_ref, g_ffn_ref,
                         w_gu_ref, w_down_ref, nw_in_ref, nw_out_ref, nw_gu_ref, nw_down_ref,
                         y_ref, newk_ref, newv_ref, newc_ref,
                         nw_in_bf_ref, nw_out_bf_ref, nw_gu_bf_ref, nw_down_bf_ref,
                         kcar_scr, vcar_scr, ccar_scr, conv_scr, a_scr, mo_scr, mix_scr,
                         xres_scr, *,
                         tiles_per_seq):
    tm = TM_PROMPT
    layer = layer_ref[0]
    s = pl.program_id(0)
    n_tiles = pl.num_programs(0) - 1
    t = jnp.minimum(s, n_tiles - 1) % tiles_per_seq

    @pl.when(s == 0)
    def _():
        mix_scr[...] = jnp.zeros(mix_scr.shape, BF)
        xres_scr[...] = jnp.zeros(xres_scr.shape, F32)

    nw_in_bf_ref[...] = nw_in_ref[...].astype(BF)
    nw_out_bf_ref[...] = nw_out_ref[...].astype(BF)
    nw_gu_bf_ref[...] = nw_gu_ref[...].astype(BF)
    nw_down_bf_ref[...] = nw_down_ref[...].astype(BF)

    h_prev = xres_scr[...] + _dot(mix_scr[...], w_out_ref[0])
    hn_prev = _rms(h_prev, g_ffn_ref[0]).astype(BF)
    def store_y(c0, cols):
        y_ref[0, :, c0:c0 + cols.shape[1]] = cols

    back = _swiglu_units(h_prev, hn_prev, w_gu_ref, w_down_ref, store_y)

    started = t > 0
    k_prev = [jnp.where(started, kcar_scr[j], 0.0).astype(BF) for j in range(KV_HEADS)]
    v_prev = [jnp.where(started, vcar_scr[j], 0.0).astype(BF) for j in range(KV_HEADS)]
    conv_scr[0:8, :] = jnp.where(started, ccar_scr[...], 0.0)

    x = x_ref[0]
    bd = bd_ref[...]
    u = _dot(_rms(x, g_attn_ref[0]).astype(BF), w_in_ref[0])
    xres_scr[...] = x

    row_chunk = lax.broadcasted_iota(jnp.int32, (PAIR, 2 * PAIR), 0) // CHUNK
    col = _lane_iota((PAIR, 2 * PAIR))
    first_key = jnp.where(started, 0, WINDOW)
    visible = (col >= row_chunk * CHUNK) & (col < (row_chunk + 3) * CHUNK)
    bias_rest = jnp.where(visible, 0.0, NEG)
    bias_first = jnp.where(col >= first_key, bias_rest, NEG)

    mk_t = mk_ref[0, 0].astype(BF)
    mv_t = mv_ref[0, 0].astype(BF)
    g_out = g_out_ref[0]

    kept = {}

    def front_units():
        kn = _head_rms(u[:, _K0:_V0], gk_ref[0], bd)
        v = u[:, _V0:_CB0]
        k_dup = _dup_halves(kn)
        v_dup = _dup_halves(v)
        for j in range(KV_HEADS):
            kcar_scr[j] = k_dup[j][tm - WINDOW:tm, :]
            vcar_scr[j] = v_dup[j][tm - WINDOW:tm, :]
        k_rep = [k_dup[j].astype(BF) for j in range(KV_HEADS)]
        v_rep = [v_dup[j].astype(BF) for j in range(KV_HEADS)]
        kept.update(kn=kn, v=v)
        yield

        gq = gq_ref[0]
        q_half = ATTN_WIDTH // 2
        qn = jnp.concatenate(
            [_head_rms(u[:, _Q0:_Q0 + q_half], gq[:, :q_half], bd),
             _head_rms(u[:, _Q0 + q_half:_K0], gq[:, q_half:], bd)], axis=-1) * ATTN_SCALE
        q_lo, q_hi = _split_heads_lo_hi(qn)
        yield

        ccx = u[:, _CC0:_CX0] * u[:, _CX0:_MQ0]
        conv_scr[8:8 + tm, :] = ccx
        cy = _conv_from_scratch(conv_scr, ccx, u[:, _CB0:_CC0], conv_w_ref[0], tm)
        ccar_scr[...] = ccx[tm - 8:tm, :]
        mix_scr[:, _A_END:_C_END] = _rms(cy, g_out[:, _A_END:_C_END]).astype(BF)
        kept.update(ccx=ccx)
        yield

        for p in range(tm // PAIR):
            r0 = p * PAIR
            bias = bias_first if p == 0 else bias_rest
            for j in range(KV_HEADS):
                c0 = j * GROUP * HEAD_DIM
                qf = _stack_group_queries(q_lo, q_hi, r0, PAIR, j)
                if p == 0:
                    k_win = jnp.concatenate([k_prev[j], k_rep[j][0:PAIR]], axis=0)
                    v_win = jnp.concatenate([v_prev[j], v_rep[j][0:PAIR]], axis=0)
                else:
                    k_win = k_rep[j][r0 - PAIR:r0 + PAIR]
                    v_win = v_rep[j][r0 - PAIR:r0 + PAIR]
                sc = _dot_nt(qf, k_win)
                yield
                probs = []
                for h in range(GROUP):
                    sh = sc[h * PAIR:(h + 1) * PAIR] + bias
                    probs.append(_softmax_rows(sh, sinks_ref[layer, j * GROUP + h]).astype(BF))
                    if h % 2 == 1:
                        yield
                o_all = _dot(jnp.concatenate(probs, axis=0),
                             jnp.concatenate([v_win, v_win], axis=-1))
                a_scr[r0:r0 + PAIR, c0:c0 + GROUP * HEAD_DIM] = _pick_head_lanes(o_all, PAIR)
                yield
        mix_scr[:, :_A_END] = _rms(a_scr[...], g_out[:, :_A_END]).astype(BF)
        mqn = _head_rms(u[:, _MQ0:], gmq_ref[0], bd) * ATTN_SCALE
        grp = _lane_iota(mqn.shape) // HEAD_DIM
        yield
        for h in range(MEM_HEADS):
            mq_h = jnp.where(grp == h, mqn, 0.0).astype(BF)
            sc = _dot(mq_h, mk_t)
            yield
            pm = _softmax_rows(sc, None).astype(BF)
            yield
            o_h = _dot_nt(pm, mv_t)
            mo_scr[:, h * HEAD_DIM:(h + 1) * HEAD_DIM] = o_h[:, h * HEAD_DIM:(h + 1) * HEAD_DIM]
            yield

    def front_then_norm():
        yield from front_units()
        mix_scr[:, _C_END:] = _rms(mo_scr[...], g_out[:, _C_END:]).astype(BF)

    _merge(back, _SWIGLU_COSTS, front_then_norm(), _PROMPT_COSTS)

    @pl.when(t == tiles_per_seq - 1)
    def _():
        newk_ref[0] = kept["kn"][tm - WINDOW:tm, :].T
        newv_ref[0] = kept["v"][tm - WINDOW:tm, :].T
        newc_ref[0] = kept["ccx"][tm - (CONV_W - 1):tm, :]


def _layer_spec(shape):
    zeros = (0,) * len(shape)
    return pl.BlockSpec((1,) + shape, lambda *args: (args[-1][0],) + zeros,
                        pipeline_mode=pl.Buffered(1))


def _matrix_spec(shape):
    return pl.BlockSpec((1,) + shape, lambda *args: (0, 0, 0), pipeline_mode=pl.Buffered(1))


_MATRIX_SHAPES = dict(w_in=(D_MODEL, IN_WIDTH), w_out=(MIX_WIDTH, D_MODEL),
                      w_gu=(D_MODEL, 2 * D_FF), w_down=(D_FF, D_MODEL))
_MATRIX_ORDER = ("w_in", "w_out", "w_gu", "w_down")
_CAST_BLOCKS = dict(w_in=32, w_out=32, w_gu=32, w_down=16)


def _cast_specs(name, n_blk, layer_of, step_of):
    rows, cols = _MATRIX_SHAPES[name]
    assert rows % (16 * n_blk) == 0
    blk = (1, rows // n_blk, cols)
    src = pl.BlockSpec(blk, lambda *a: (layer_of(*a), jnp.minimum(step_of(*a), n_blk - 1), 0))
    dst = pl.BlockSpec(blk, lambda *a: (0, jnp.minimum(step_of(*a), n_blk - 1), 0))
    return src, dst, jax.ShapeDtypeStruct((1, rows, cols), BF)


def _weight_specs():
    return [
        _layer_spec((1, D_MODEL)),
        _matrix_spec(_MATRIX_SHAPES["w_in"]),
        pl.BlockSpec((256, 256), lambda *args: (0, 0), pipeline_mode=pl.Buffered(1)),
        _layer_spec((1, ATTN_WIDTH)),
        _layer_spec((1, KV_WIDTH)),
        _layer_spec((1, MEM_WIDTH)),
        _layer_spec((CONV_W, CONV_DIM)),
        _layer_spec((1, MIX_WIDTH)),
        _matrix_spec(_MATRIX_SHAPES["w_out"]),
        _layer_spec((1, D_MODEL)),
        _matrix_spec(_MATRIX_SHAPES["w_gu"]),
        _matrix_spec(_MATRIX_SHAPES["w_down"]),
    ]


_WEIGHT_ORDER = ("g_attn", "w_in", "bd", "gq", "gk", "gmq", "conv_w", "g_out", "w_out", "g_ffn",
                 "w_gu", "w_down")

_SMEM_SPEC = pl.BlockSpec(memory_space=pltpu.SMEM)


def _prompt_layer(layer, x, mem_k_t, mem_v_t, sinks, weights, f32_matrices):
    batch, seq, _ = x.shape
    tm = TM_PROMPT
    tps = seq // tm
    n_tiles = batch * tps

    def front(s):
        return jnp.minimum(s, n_tiles - 1)

    def back(s):
        return jnp.maximum(s - 1, 0)

    assert max(_CAST_BLOCKS.values()) <= n_tiles
    casts = [_cast_specs(n, _CAST_BLOCKS[n], lambda s, l: jnp.minimum(l[0] + 1, DEPTH - 1),
                         lambda s, l: s) for n in _MATRIX_ORDER]
    in_specs = [
        _SMEM_SPEC,
        pl.BlockSpec((1, tm, D_MODEL), lambda s, l: (front(s) // tps, front(s) % tps, 0)),
        pl.BlockSpec((1, 1, MEM_WIDTH, N_MEM), lambda s, l: (l[0], front(s) // tps, 0, 0)),
        pl.BlockSpec((1, 1, MEM_WIDTH, N_MEM), lambda s, l: (l[0], front(s) // tps, 0, 0)),
    ] + _weight_specs() + [c[0] for c in casts]
    out_shape = (
        jax.ShapeDtypeStruct((batch, seq, D_MODEL), F32),
        jax.ShapeDtypeStruct((batch, KV_WIDTH, WINDOW), F32),
        jax.ShapeDtypeStruct((batch, KV_WIDTH, WINDOW), F32),
        jax.ShapeDtypeStruct((batch, CONV_W - 1, CONV_DIM), F32),
    ) + tuple(c[2] for c in casts)
    out_specs = (
        pl.BlockSpec((1, tm, D_MODEL), lambda s, l: (back(s) // tps, back(s) % tps, 0)),
        pl.BlockSpec((1, KV_WIDTH, WINDOW), lambda s, l: (front(s) // tps, 0, 0)),
        pl.BlockSpec((1, KV_WIDTH, WINDOW), lambda s, l: (front(s) // tps, 0, 0)),
        pl.BlockSpec((1, CONV_W - 1, CONV_DIM), lambda s, l: (front(s) // tps, 0, 0)),
    ) + tuple(c[1] for c in casts)
    scratch = [
        pltpu.VMEM((KV_HEADS, WINDOW, 2 * HEAD_DIM), F32),
        pltpu.VMEM((KV_HEADS, WINDOW, 2 * HEAD_DIM), F32),
        pltpu.VMEM((8, CONV_DIM), F32),
        pltpu.VMEM((8 + tm, CONV_DIM), F32),
        pltpu.VMEM((tm, ATTN_WIDTH), F32),
        pltpu.VMEM((tm, MEM_WIDTH), F32),
        pltpu.VMEM((tm, MIX_WIDTH), BF),
        pltpu.VMEM((tm, D_MODEL), F32),
    ]
    return pl.pallas_call(
        functools.partial(_prompt_layer_kernel, tiles_per_seq=tps),
        grid_spec=pltpu.PrefetchScalarGridSpec(
            num_scalar_prefetch=1, grid=(n_tiles + 1,), in_specs=in_specs,
            out_specs=out_specs, scratch_shapes=scratch),
        out_shape=out_shape,
        compiler_params=pltpu.CompilerParams(
            dimension_semantics=("arbitrary",),
            vmem_limit_bytes=VMEM_LIMIT_BYTES),
        name="prompt_layer",
    )(layer, sinks, x, mem_k_t, mem_v_t, *[weights[n] for n in _WEIGHT_ORDER],
      *[f32_matrices[n] for n in _MATRIX_ORDER])


_N_SAMPLE_HALF_UNITS = (G_SAMPLE // 2 // SAMPLE_LOCKSTEP) * (
    SAMPLE_LOCKSTEP * (1 + KV_HEADS * 4 + 3) + 1)


def _sample_layer_kernel(layer_ref, sinks_ref, x_ref, ck_ref, cv_ref, cc_ref, mk_ref, mv_ref,
                         g_attn_ref, w_in_ref, bd_ref, gq_ref, gk_ref, gmq_ref, conv_w_ref,
                         g_out_ref, w_out_ref, g_ffn_ref, w_gu_ref, w_down_ref, y_ref, newk_ref,
                         newv_ref, newc_ref, conv_scr, a_scr, mix_scr):
    rows = CHUNK
    half = G_SAMPLE // 2
    half_rows = half * rows
    layer = layer_ref[0]
    g_out = g_out_ref[0]
    x = x_ref[...]
    conv_w = conv_w_ref[0]
    old_lane = _lane_iota((KV_WIDTH, WINDOW)) < WINDOW - rows

    def project_units(hh, proj):
        xn = _rms(x[hh * half_rows:(hh + 1) * half_rows], g_attn_ref[0]).astype(BF)
        blocks = []
        for c0 in range(0, IN_WIDTH, PROJ_BLOCK):
            blocks.append(_dot(xn, w_in_ref[0, :, c0:c0 + PROJ_BLOCK]))
            yield
        qn, kn, v, cb, ccx, mqn = _finish_projection(
            jnp.concatenate(blocks, axis=-1), bd_ref[...], gq_ref[0], gk_ref[0], gmq_ref[0])
        q_lo, q_hi = _split_heads_lo_hi(qn)
        grp = _lane_iota(mqn.shape) // HEAD_DIM
        mq_heads = [jnp.where(grp == h, mqn, 0.0).astype(BF) for h in range(MEM_HEADS)]
        proj.update(q_lo=q_lo, q_hi=q_hi, kn=kn, v=v, cb=cb, ccx=ccx, mq_heads=mq_heads)
        yield

    def shifted_window(cache_t, new_t):
        return jnp.where(old_lane, pltpu.roll(cache_t, WINDOW - rows, axis=1), new_t)

    repeat_bias = jnp.where(_lane_iota((1, 2 * WINDOW)) < WINDOW + rows, 0.0, NEG)

    def window_chain(proj, b, j, kk_bf, vv_bf):
        r0 = b * rows
        c0 = j * GROUP * HEAD_DIM
        qf = _stack_group_queries(proj["q_lo"], proj["q_hi"], r0 % half_rows, rows, j)
        kj_t = kk_bf[j * HEAD_DIM:(j + 1) * HEAD_DIM, :]
        vj_t = vv_bf[j * HEAD_DIM:(j + 1) * HEAD_DIM, :]
        sc = _dot(qf, jnp.concatenate([kj_t, kj_t], axis=0)) + repeat_bias
        yield
        probs = []
        for h in range(GROUP):
            probs.append(_softmax_rows(sc[h * rows:(h + 1) * rows],
                                       sinks_ref[layer, j * GROUP + h]).astype(BF))
            if h % 2 == 1:
                yield
        o_all = _dot_nt(jnp.concatenate(probs, axis=0), jnp.concatenate([vj_t] * GROUP, axis=0))
        a_scr[r0:r0 + rows, c0:c0 + GROUP * HEAD_DIM] = _pick_head_lanes(o_all, rows)
        yield

    def memory_chain(proj, b):
        r0 = b * rows
        q0 = r0 % half_rows
        mq_blocks = jnp.concatenate(
            [proj["mq_heads"][h][q0:q0 + rows] for h in range(MEM_HEADS)], axis=0)
        sc = _dot(mq_blocks, mk_ref[0, b].astype(BF))
        yield
        pm = _softmax_rows(sc, None).astype(BF)
        yield
        mo_b = _pick_head_lanes(_dot_nt(pm, mv_ref[0, b].astype(BF)), rows)
        mix_scr[r0:r0 + rows, _C_END:] = _rms(mo_b, g_out[:, _C_END:]).astype(BF)
        yield

    def mixer_units(hh, proj):
        for b0 in range(hh * half, (hh + 1) * half, SAMPLE_LOCKSTEP):
            chains = []
            for b in range(b0, b0 + SAMPLE_LOCKSTEP):
                r0 = b * rows
                q0 = r0 % half_rows
                ck_t = ck_ref[0, b]
                cv_t = cv_ref[0, b]
                kn_b = proj["kn"][q0:q0 + rows]
                v_b = proj["v"][q0:q0 + rows]
                kn_t = jnp.concatenate([kn_b, kn_b], axis=0).T
                v_t = jnp.concatenate([v_b, v_b], axis=0).T
                newk_ref[b] = shifted_window(ck_t, kn_t)
                newv_ref[b] = shifted_window(cv_t, v_t)
                kk_bf = jnp.concatenate([ck_t, kn_t], axis=1).astype(BF)
                vv_bf = jnp.concatenate([cv_t, v_t], axis=1).astype(BF)

                ccx_b = proj["ccx"][q0:q0 + rows]
                conv_scr[6:8, :] = cc_ref[0, b]
                conv_scr[8:8 + rows, :] = ccx_b
                cy_b = _conv_from_scratch(conv_scr, ccx_b, proj["cb"][q0:q0 + rows], conv_w, rows)
                mix_scr[r0:r0 + rows, _A_END:_C_END] = _rms(cy_b,
                                                            g_out[:, _A_END:_C_END]).astype(BF)
                newc_ref[b] = ccx_b[rows - (CONV_W - 1):rows, :]
                yield
                chains += [window_chain(proj, b, j, kk_bf, vv_bf) for j in range(KV_HEADS)]
                chains.append(memory_chain(proj, b))
            yield from _round_robin(chains)
            r0, r1 = b0 * rows, (b0 + SAMPLE_LOCKSTEP) * rows
            mix_scr[r0:r1, :_A_END] = _rms(a_scr[r0:r1, :], g_out[:, :_A_END]).astype(BF)
            yield

    def output_units(r_lo, r_hi):
        h = x[r_lo:r_hi] + _dot(mix_scr[r_lo:r_hi, :], w_out_ref[0])
        hn = _rms(h, g_ffn_ref[0]).astype(BF)
        yield

        def store_y(c0, cols):
            y_ref[r_lo:r_hi, c0:c0 + cols.shape[1]] = cols

        yield from _swiglu_units(h, hn, w_gu_ref, w_down_ref, store_y)

    n_proj_units = IN_WIDTH // PROJ_BLOCK + 1
    proj_a, proj_b = {}, {}
    for _ in project_units(0, proj_a):
        pass
    _merge(project_units(1, proj_b), [1.0] * n_proj_units,
           mixer_units(0, proj_a), [1.0] * _N_SAMPLE_HALF_UNITS)
    _merge(output_units(0, half_rows), [4.0] + _SWIGLU_COSTS,
           mixer_units(1, proj_b), [1.0] * _N_SAMPLE_HALF_UNITS)
    for _ in output_units(half_rows, 2 * half_rows):
        pass


def _sample_layer(layer, x, cache_k_t, cache_v_t, cache_conv, mem_k_t, mem_v_t, sinks, weights):
    n_b = cache_k_t.shape[1]
    g = G_SAMPLE
    tm = g * CHUNK
    in_specs = [
        _SMEM_SPEC,
        pl.BlockSpec((tm, D_MODEL), lambda i, l: (i, 0)),
        pl.BlockSpec((1, g, KV_WIDTH, WINDOW), lambda i, l: (l[0], i, 0, 0)),
        pl.BlockSpec((1, g, KV_WIDTH, WINDOW), lambda i, l: (l[0], i, 0, 0)),
        pl.BlockSpec((1, g, CONV_W - 1, CONV_DIM), lambda i, l: (l[0], i, 0, 0)),
        pl.BlockSpec((1, g, MEM_WIDTH, N_MEM), lambda i, l: (l[0], i, 0, 0)),
        pl.BlockSpec((1, g, MEM_WIDTH, N_MEM), lambda i, l: (l[0], i, 0, 0)),
    ] + _weight_specs()
    out_shape = (
        jax.ShapeDtypeStruct((n_b * CHUNK, D_MODEL), F32),
        jax.ShapeDtypeStruct((n_b, KV_WIDTH, WINDOW), F32),
        jax.ShapeDtypeStruct((n_b, KV_WIDTH, WINDOW), F32),
        jax.ShapeDtypeStruct((n_b, CONV_W - 1, CONV_DIM), F32),
    )
    out_specs = (
        pl.BlockSpec((tm, D_MODEL), lambda i, l: (i, 0)),
        pl.BlockSpec((g, KV_WIDTH, WINDOW), lambda i, l: (i, 0, 0)),
        pl.BlockSpec((g, KV_WIDTH, WINDOW), lambda i, l: (i, 0, 0)),
        pl.BlockSpec((g, CONV_W - 1, CONV_DIM), lambda i, l: (i, 0, 0)),
    )
    scratch = [
        pltpu.VMEM((8 + CHUNK, CONV_DIM), F32),
        pltpu.VMEM((tm, ATTN_WIDTH), F32),
        pltpu.VMEM((tm, MIX_WIDTH), BF),
    ]
    return pl.pallas_call(
        _sample_layer_kernel,
        grid_spec=pltpu.PrefetchScalarGridSpec(
            num_scalar_prefetch=1, grid=(n_b // g,), in_specs=in_specs, out_specs=out_specs,
            scratch_shapes=scratch),
        out_shape=out_shape,
        compiler_params=pltpu.CompilerParams(
            dimension_semantics=("arbitrary",),
            vmem_limit_bytes=VMEM_LIMIT_BYTES),
        name="sample_layer",
    )(layer, sinks, x, cache_k_t, cache_v_t, cache_conv, mem_k_t, mem_v_t,
      *[weights[n] for n in _WEIGHT_ORDER])


def _mem_kv_kernel(mem_ref, g_mem_ref, w_ref, bd_ref, gmk_ref, w0_in_ref, w0_out_ref, w0_gu_ref,
                   w0_down_ref, mk_ref, mv_ref, w0_in_bf_ref, w0_out_bf_ref, w0_gu_bf_ref,
                   w0_down_bf_ref):
    w = w_ref[0].astype(BF)
    for b in range(mem_ref.shape[0]):
        xn = _rms(mem_ref[b], g_mem_ref[0]).astype(BF)
        kv = _dot(xn, w)
        mk_ref[0, b] = _head_rms(kv[:, :MEM_WIDTH], gmk_ref[0], bd_ref[...]).T
        mv_ref[0, b] = kv[:, MEM_WIDTH:].T
    w0_in_bf_ref[...] = w0_in_ref[...].astype(BF)
    w0_out_bf_ref[...] = w0_out_ref[...].astype(BF)
    w0_gu_bf_ref[...] = w0_gu_ref[...].astype(BF)
    w0_down_bf_ref[...] = w0_down_ref[...].astype(BF)


def _mem_kv(mem, g_mem, w_mem_kv, bd, gmk, f32_matrices):
    batch = mem.shape[0]
    out = jax.ShapeDtypeStruct((DEPTH, batch, MEM_WIDTH, N_MEM), F32)
    casts = [_cast_specs(n, DEPTH, lambda l: 0, lambda l: l) for n in _MATRIX_ORDER]
    return pl.pallas_call(
        _mem_kv_kernel,
        grid=(DEPTH,),
        in_specs=[
            pl.BlockSpec((batch, N_MEM, D_MODEL), lambda l: (0, 0, 0),
                         pipeline_mode=pl.Buffered(1)),
            pl.BlockSpec((1, 1, D_MODEL), lambda l: (l, 0, 0)),
            pl.BlockSpec((1, D_MODEL, 2 * MEM_WIDTH), lambda l: (l, 0, 0)),
            pl.BlockSpec((256, 256), lambda l: (0, 0)),
            pl.BlockSpec((1, 1, MEM_WIDTH), lambda l: (l, 0, 0)),
        ] + [c[0] for c in casts],
        out_specs=(pl.BlockSpec((1, batch, MEM_WIDTH, N_MEM), lambda l: (l, 0, 0, 0)),
                   pl.BlockSpec((1, batch, MEM_WIDTH, N_MEM), lambda l: (l, 0, 0, 0)))
        + tuple(c[1] for c in casts),
        out_shape=(out, out) + tuple(c[2] for c in casts),
        compiler_params=pltpu.CompilerParams(
            dimension_semantics=("arbitrary",),
            vmem_limit_bytes=VMEM_LIMIT_BYTES),
        name="prompt_mem_kv",
    )(mem, g_mem, w_mem_kv, bd, gmk, *[f32_matrices[n] for n in _MATRIX_ORDER])


def _tile_heads(g, n):
    return jnp.tile(g, (1, n)).reshape(g.shape[0], 1, n * g.shape[1])


def _feature_major(c):
    lead = c.shape[:-3]
    n_tok, heads, dim = c.shape[-3:]
    perm = tuple(range(len(lead))) + (len(lead) + 1, len(lead) + 2, len(lead))
    return jnp.transpose(c, perm).reshape(lead + (heads * dim, n_tok))


def _token_major(c_t, heads):
    lead = c_t.shape[:-2]
    n_tok = c_t.shape[-1]
    c = c_t.reshape(lead + (heads, HEAD_DIM, n_tok))
    perm = tuple(range(len(lead))) + (len(lead) + 2, len(lead), len(lead) + 1)
    return jnp.transpose(c, perm)


def kernel(x_prompt, x_sample, mem_prompt, cache_win_k, cache_win_v, cache_conv, cache_mem_k,
           cache_mem_v, attn_norm_g, w_in, q_norm_g, k_norm_g, sinks, conv_w, mem_norm_g,
           w_mem_kv, mq_norm_g, mk_norm_g, out_norm_g, w_out, ffn_norm_g, w_gate_up, w_down):
    batch, seq, _ = x_prompt.shape
    dec_batch, dec_seq, _ = x_sample.shape
    assert dec_seq == CHUNK and seq % TM_PROMPT == 0 and dec_batch % G_SAMPLE == 0
    assert G_SAMPLE % (2 * SAMPLE_LOCKSTEP) == 0

    head = jnp.arange(256) // HEAD_DIM
    bd = jnp.where(head[:, None] == head[None, :], 1.0 / HEAD_DIM, 0.0).astype(BF)

    f32_matrices = dict(w_in=w_in, w_out=w_out, w_gu=w_gate_up, w_down=w_down)
    small = dict(
        g_attn=attn_norm_g.reshape(DEPTH, 1, D_MODEL), bd=bd,
        gq=_tile_heads(q_norm_g, N_HEADS), gk=_tile_heads(k_norm_g, KV_HEADS),
        gmq=_tile_heads(mq_norm_g, MEM_HEADS), conv_w=conv_w,
        g_out=out_norm_g.reshape(DEPTH, 1, MIX_WIDTH),
        g_ffn=ffn_norm_g.reshape(DEPTH, 1, D_MODEL))

    mk_t, mv_t, *first = _mem_kv(mem_prompt, mem_norm_g.reshape(DEPTH, 1, D_MODEL), w_mem_kv, bd,
                                 _tile_heads(mk_norm_g, MEM_HEADS), f32_matrices)
    matrices = dict(zip(_MATRIX_ORDER, first))

    ck_t = _feature_major(cache_win_k)
    cv_t = _feature_major(cache_win_v)
    cmk_t = _feature_major(cache_mem_k)
    cmv_t = _feature_major(cache_mem_v)

    yp = x_prompt
    ys = x_sample.reshape(dec_batch * dec_seq, D_MODEL)
    wk_p, wv_p, cv_p, wk_s, wv_s, cv_s = [], [], [], [], [], []
    for l in range(DEPTH):
        layer = jnp.full((1,), l, jnp.int32)
        weights = dict(small, **matrices)
        yp, k_p, v_p, c_p, *next_matrices = _prompt_layer(layer, yp, mk_t, mv_t, sinks, weights,
                                                          f32_matrices)
        wk_p.append(k_p); wv_p.append(v_p); cv_p.append(c_p)
        ys, k_s, v_s, c_s = _sample_layer(layer, ys, ck_t, cv_t, cache_conv, cmk_t, cmv_t, sinks,
                                          weights)
        wk_s.append(k_s); wv_s.append(v_s); cv_s.append(c_s)
        matrices = dict(zip(_MATRIX_ORDER, next_matrices))

    return (yp, ys.reshape(dec_batch, dec_seq, D_MODEL),
            _token_major(jnp.stack(wk_p), KV_HEADS), _token_major(jnp.stack(wv_p), KV_HEADS),
            jnp.stack(cv_p),
            _token_major(mk_t, MEM_HEADS), _token_major(mv_t, MEM_HEADS),
            _token_major(jnp.stack(wk_s), KV_HEADS), _token_major(jnp.stack(wv_s), KV_HEADS),
            jnp.stack(cv_s))
```

```python
import functools

import jax
import jax.numpy as jnp
from jax import lax
from jax.experimental import pallas as pl
from jax.experimental.pallas import tpu as pltpu

D_MODEL = 1024
DEPTH = 4
CHUNK = 64
HEAD_DIM = 64
N_HEADS = 8
KV_HEADS = 2
GROUP = N_HEADS // KV_HEADS
WINDOW = 128
ATTN_WIDTH = N_HEADS * HEAD_DIM
KV_WIDTH = KV_HEADS * HEAD_DIM
CONV_DIM = 256
CONV_W = 3
MEM_HEADS = 4
MEM_WIDTH = MEM_HEADS * HEAD_DIM
N_MEM = 256
MIX_WIDTH = ATTN_WIDTH + CONV_DIM + MEM_WIDTH
IN_WIDTH = ATTN_WIDTH + 2 * KV_WIDTH + 3 * CONV_DIM + MEM_WIDTH
D_FF = 2816
EPS = 1e-6
ATTN_SCALE = HEAD_DIM ** -0.5
NEG = -1e30

_Q0, _K0, _V0 = 0, ATTN_WIDTH, ATTN_WIDTH + KV_WIDTH
_CB0 = ATTN_WIDTH + 2 * KV_WIDTH
_CC0 = _CB0 + CONV_DIM
_CX0 = _CC0 + CONV_DIM
_MQ0 = _CX0 + CONV_DIM
_A_END = ATTN_WIDTH
_C_END = ATTN_WIDTH + CONV_DIM

PAIR = 2 * CHUNK
TM_PROMPT = 512
G_SAMPLE = 8
SAMPLE_LOCKSTEP = 2
FF_BLOCK = 256
DOWN_SPLIT = 4
PROJ_BLOCK = 256
STAT_WIDTH = 256
VMEM_LIMIT_BYTES = 56 * 1024 * 1024

BF = jnp.bfloat16
F32 = jnp.float32


def _dot(a, b):
    return jnp.dot(a, b, preferred_element_type=F32)


def _dot_nt(a, b):
    return lax.dot_general(a, b, (((1,), (1,)), ((), ())), preferred_element_type=F32)


def _rms(x, g):
    ms = jnp.mean(x * x, axis=-1, keepdims=True)
    return x * lax.rsqrt(ms + EPS) * g


def _head_rms(z, g, bd):
    w = z.shape[-1]
    ms = _dot((z * z).astype(BF), bd[:w, :w])
    return z * lax.rsqrt(ms + EPS) * g


def _lane_iota(shape):
    return lax.broadcasted_iota(jnp.int32, shape, len(shape) - 1)


def _split_heads_lo_hi(z):
    lo = (_lane_iota(z.shape) % (2 * HEAD_DIM)) < HEAD_DIM
    return jnp.where(lo, z, 0.0).astype(BF), jnp.where(lo, 0.0, z).astype(BF)


def _stack_group_queries(q_lo, q_hi, r0, rows, j):
    c0 = j * GROUP * HEAD_DIM
    pair = 2 * HEAD_DIM
    return jnp.concatenate(
        [half[r0:r0 + rows, c0 + k * pair:c0 + (k + 1) * pair]
         for k in range(GROUP // 2) for half in (q_lo, q_hi)], axis=0)


def _dup_halves(z):
    sw = pltpu.roll(z, HEAD_DIM, axis=1)
    lo = _lane_iota(z.shape) < HEAD_DIM
    return jnp.where(lo, z, sw), jnp.where(lo, sw, z)


def _pick_head_lanes(o_all, rows):
    grp = _lane_iota((rows, GROUP * HEAD_DIM)) // HEAD_DIM
    out = o_all[3 * rows:4 * rows]
    for h in (2, 1, 0):
        out = jnp.where(grp == h, o_all[h * rows:(h + 1) * rows], out)
    return out


def _softmax_rows(s, sink):
    m = jnp.max(s, axis=-1, keepdims=True)
    if sink is not None:
        m = jnp.maximum(m, sink)
    p = jnp.exp(s - m)
    den = jnp.sum(p, axis=-1, keepdims=True)
    if sink is not None:
        den = den + jnp.exp(sink - m)
    return p * (1.0 / den)


def _finish_projection(u, bd, gq, gk, gmq):
    half = ATTN_WIDTH // 2
    qn = jnp.concatenate(
        [_head_rms(u[:, _Q0:_Q0 + half], gq[:, :half], bd),
         _head_rms(u[:, _Q0 + half:_K0], gq[:, half:], bd)], axis=-1) * ATTN_SCALE
    kn = _head_rms(u[:, _K0:_V0], gk, bd)
    v = u[:, _V0:_CB0]
    cb = u[:, _CB0:_CC0]
    ccx = u[:, _CC0:_CX0] * u[:, _CX0:_MQ0]
    mqn = _head_rms(u[:, _MQ0:], gmq, bd) * ATTN_SCALE
    return qn, kn, v, cb, ccx, mqn


def _conv_from_scratch(conv_scr, ccx, cb, conv_w, rows):
    sh2 = conv_scr[6:6 + rows, :]
    sh1 = conv_scr[7:7 + rows, :]
    return cb * (sh2 * conv_w[0:1, :] + sh1 * conv_w[1:2, :] + ccx * conv_w[2:3, :])


def _swiglu_unit_costs():
    n_blk = D_FF // FF_BLOCK
    costs = []
    for b in range(n_blk):
        costs += [1.0, 1.0]
        if b >= 2 and b % 2 == 0:
            costs += [2.0 / DOWN_SPLIT] * DOWN_SPLIT
    for c in range((n_blk - 1) // 2, (n_blk + 1) // 2):
        costs += [float(min(2 * c + 2, n_blk) - 2 * c) / DOWN_SPLIT] * DOWN_SPLIT
    return costs


_SWIGLU_COSTS = _swiglu_unit_costs()


def _swiglu_units(h, hn, w_gu_ref, w_down_ref, store):
    blk = FF_BLOCK
    n_blk = D_FF // blk
    n_down = (n_blk + 1) // 2
    width = D_MODEL // DOWN_SPLIT
    acts = []
    y = [h[:, k * width:(k + 1) * width] for k in range(DOWN_SPLIT)]

    def down_units(c):
        lo, hi = 2 * c, min(2 * c + 2, n_blk)
        act = acts[lo] if hi - lo == 1 else jnp.concatenate(acts[lo:hi], axis=-1)
        for k in range(DOWN_SPLIT):
            y[k] = y[k] + _dot(act, w_down_ref[0, lo * blk:hi * blk, k * width:(k + 1) * width])
            if c == n_down - 1:
                store(k * width, y[k])
            yield

    for b in range(n_blk):
        gate = _dot(hn, w_gu_ref[0, :, b * blk:(b + 1) * blk])
        yield
        up = _dot(hn, w_gu_ref[0, :, D_FF + b * blk:D_FF + (b + 1) * blk])
        acts.append((gate * jax.nn.sigmoid(gate) * up).astype(BF))
        yield
        if b >= 2 and b % 2 == 0:
            yield from down_units(b // 2 - 1)
    for c in range((n_blk - 1) // 2, n_down):
        yield from down_units(c)


def _merge(first, first_costs, second, second_costs):
    streams = [[first, list(first_costs), 0.0, sum(first_costs)],
               [second, list(second_costs), 0.0, sum(second_costs)]]
    while streams[0][1] or streams[1][1]:
        live = [st for st in streams if st[1]]
        st = min(live, key=lambda st: st[2] / st[3])
        next(st[0])
        st[2] += st[1].pop(0)
        if not st[1]:
            assert next(st[0], "end") == "end"


def _round_robin(chains):
    chains = list(chains)
    while chains:
        for chain in list(chains):
            try:
                next(chain)
            except StopIteration:
                chains.remove(chain)
            else:
                yield


_PROMPT_COSTS = ([2.0, 2.0, 1.5] + [1.0, 0.8, 0.8, 1.0] * ((TM_PROMPT // PAIR) * KV_HEADS) + [3.0]
                 + [0.8, 1.2, 0.8] * MEM_HEADS)


def _prompt_layer_kernel(layer_ref, sinks_ref, x_ref, mk_ref, mv_ref, g_attn_ref, w_in_ref, bd_ref,
                         gq_ref, gk_ref, gmq_ref, conv_w_ref, g_out_ref, w_out_ref, g_ffn_ref,
                         w_gu_ref, w_down_ref, nw_in_ref, nw_out_ref, nw_gu_ref, nw_down_ref,
                         y_ref, newk_ref, newv_ref, newc_ref,
                         nw_in_bf_ref, nw_out_bf_ref, nw_gu_bf_ref, nw_down_bf_ref,
                         kcar_scr, vcar_scr, ccar_scr, conv_scr, a_scr, mo_scr, mix_scr,
                         xres_scr, *,
                         tiles_per_seq):
    tm = TM_PROMPT
    layer = layer_ref[0]
    s = pl.program_id(0)
    n_tiles = pl.num_programs(0) - 1
    t = jnp.minimum(s, n_tiles - 1) % tiles_per_seq

    @pl.when(s == 0)
    def _():
        mix_scr[...] = jnp.zeros(mix_scr.shape, BF)
        xres_scr[...] = jnp.zeros(xres_scr.shape, F32)

    nw_in_bf_ref[...] = nw_in_ref[...].astype(BF)
    nw_out_bf_ref[...] = nw_out_ref[...].astype(BF)
    nw_gu_bf_ref[...] = nw_gu_ref[...].astype(BF)
    nw_down_bf_ref[...] = nw_down_ref[...].astype(BF)

    h_prev = xres_scr[...] + _dot(mix_scr[...], w_out_ref[0])
    hn_prev = _rms(h_prev, g_ffn_ref[0]).astype(BF)
    def store_y(c0, cols):
        y_ref[0, :, c0:c0 + cols.shape[1]] = cols

    back = _swiglu_units(h_prev, hn_prev, w_gu_ref, w_down_ref, store_y)

    started = t > 0
    k_prev = [jnp.where(started, kcar_scr[j], 0.0).astype(BF) for j in range(KV_HEADS)]
    v_prev = [jnp.where(started, vcar_scr[j], 0.0).astype(BF) for j in range(KV_HEADS)]
    conv_scr[0:8, :] = jnp.where(started, ccar_scr[...], 0.0)

    x = x_ref[0]
    bd = bd_ref[...]
    u = _dot(_rms(x, g_attn_ref[0]).astype(BF), w_in_ref[0])
    xres_scr[...] = x

    row_chunk = lax.broadcasted_iota(jnp.int32, (PAIR, 2 * PAIR), 0) // CHUNK
    col = _lane_iota((PAIR, 2 * PAIR))
    first_key = jnp.where(started, 0, WINDOW)
    visible = (col >= row_chunk * CHUNK) & (col < (row_chunk + 3) * CHUNK)
    bias_rest = jnp.where(visible, 0.0, NEG)
    bias_first = jnp.where(col >= first_key, bias_rest, NEG)

    mk_t = mk_ref[0, 0].astype(BF)
    mv_t = mv_ref[0, 0].astype(BF)
    g_out = g_out_ref[0]

    kept = {}

    def front_units():
        kn = _head_rms(u[:, _K0:_V0], gk_ref[0], bd)
        v = u[:, _V0:_CB0]
        k_dup = _dup_halves(kn)
        v_dup = _dup_halves(v)
        for j in range(KV_HEADS):
            kcar_scr[j] = k_dup[j][tm - WINDOW:tm, :]
            vcar_scr[j] = v_dup[j][tm - WINDOW:tm, :]
        k_rep = [k_dup[j].astype(BF) for j in range(KV_HEADS)]
        v_rep = [v_dup[j].astype(BF) for j in range(KV_HEADS)]
        kept.update(kn=kn, v=v)
        yield

        gq = gq_ref[0]
        q_half = ATTN_WIDTH // 2
        qn = jnp.concatenate(
            [_head_rms(u[:, _Q0:_Q0 + q_half], gq[:, :q_half], bd),
             _head_rms(u[:, _Q0 + q_half:_K0], gq[:, q_half:], bd)], axis=-1) * ATTN_SCALE
        q_lo, q_hi = _split_heads_lo_hi(qn)
        yield

        ccx = u[:, _CC0:_CX0] * u[:, _CX0:_MQ0]
        conv_scr[8:8 + tm, :] = ccx
        cy = _conv_from_scratch(conv_scr, ccx, u[:, _CB0:_CC0], conv_w_ref[0], tm)
        ccar_scr[...] = ccx[tm - 8:tm, :]
        mix_scr[:, _A_END:_C_END] = _rms(cy, g_out[:, _A_END:_C_END]).astype(BF)
        kept.update(ccx=ccx)
        yield

        for p in range(tm // PAIR):
            r0 = p * PAIR
            bias = bias_first if p == 0 else bias_rest
            for j in range(KV_HEADS):
                c0 = j * GROUP * HEAD_DIM
                qf = _stack_group_queries(q_lo, q_hi, r0, PAIR, j)
                if p == 0:
                    k_win = jnp.concatenate([k_prev[j], k_rep[j][0:PAIR]], axis=0)
                    v_win = jnp.concatenate([v_prev[j], v_rep[j][0:PAIR]], axis=0)
                else:
                    k_win = k_rep[j][r0 - PAIR:r0 + PAIR]
                    v_win = v_rep[j][r0 - PAIR:r0 + PAIR]
                sc = _dot_nt(qf, k_win)
                yield
                probs = []
                for h in range(GROUP):
                    sh = sc[h * PAIR:(h + 1) * PAIR] + bias
                    probs.append(_softmax_rows(sh, sinks_ref[layer, j * GROUP + h]).astype(BF))
                    if h % 2 == 1:
                        yield
                o_all = _dot(jnp.concatenate(probs, axis=0),
                             jnp.concatenate([v_win, v_win], axis=-1))
                a_scr[r0:r0 + PAIR, c0:c0 + GROUP * HEAD_DIM] = _pick_head_lanes(o_all, PAIR)
                yield
        mix_scr[:, :_A_END] = _rms(a_scr[...], g_out[:, :_A_END]).astype(BF)
        mqn = _head_rms(u[:, _MQ0:], gmq_ref[0], bd) * ATTN_SCALE
        grp = _lane_iota(mqn.shape) // HEAD_DIM
        yield
        for h in range(MEM_HEADS):
            mq_h = jnp.where(grp == h, mqn, 0.0).astype(BF)
            sc = _dot(mq_h, mk_t)
            yield
            pm = _softmax_rows(sc, None).astype(BF)
            yield
            o_h = _dot_nt(pm, mv_t)
            mo_scr[:, h * HEAD_DIM:(h + 1) * HEAD_DIM] = o_h[:, h * HEAD_DIM:(h + 1) * HEAD_DIM]
            yield

    def front_then_norm():
        yield from front_units()
        mix_scr[:, _C_END:] = _rms(mo_scr[...], g_out[:, _C_END:]).astype(BF)

    _merge(back, _SWIGLU_COSTS, front_then_norm(), _PROMPT_COSTS)

    @pl.when(t == tiles_per_seq - 1)
    def _():
        newk_ref[0] = kept["kn"][tm - WINDOW:tm, :].T
        newv_ref[0] = kept["v"][tm - WINDOW:tm, :].T
        newc_ref[0] = kept["ccx"][tm - (CONV_W - 1):tm, :]


def _layer_spec(shape):
    zeros = (0,) * len(shape)
    return pl.BlockSpec((1,) + shape, lambda *args: (args[-1][0],) + zeros,
                        pipeline_mode=pl.Buffered(1))


def _matrix_spec(shape):
    return pl.BlockSpec((1,) + shape, lambda *args: (0, 0, 0), pipeline_mode=pl.Buffered(1))


_MATRIX_SHAPES = dict(w_in=(D_MODEL, IN_WIDTH), w_out=(MIX_WIDTH, D_MODEL),
                      w_gu=(D_MODEL, 2 * D_FF), w_down=(D_FF, D_MODEL))
_MATRIX_ORDER = ("w_in", "w_out", "w_gu", "w_down")
_CAST_BLOCKS = dict(w_in=32, w_out=32, w_gu=32, w_down=16)


def _cast_specs(name, n_blk, layer_of, step_of):
    rows, cols = _MATRIX_SHAPES[name]
    assert rows % (16 * n_blk) == 0
    blk = (1, rows // n_blk, cols)
    src = pl.BlockSpec(blk, lambda *a: (layer_of(*a), jnp.minimum(step_of(*a), n_blk - 1), 0))
    dst = pl.BlockSpec(blk, lambda *a: (0, jnp.minimum(step_of(*a), n_blk - 1), 0))
    return src, dst, jax.ShapeDtypeStruct((1, rows, cols), BF)


def _weight_specs():
    return [
        _layer_spec((1, D_MODEL)),
        _matrix_spec(_MATRIX_SHAPES["w_in"]),
        pl.BlockSpec((STAT_WIDTH, STAT_WIDTH), lambda *args: (0, 0),
                     pipeline_mode=pl.Buffered(1)),
        _layer_spec((1, ATTN_WIDTH)),
        _layer_spec((1, KV_WIDTH)),
        _layer_spec((1, MEM_WIDTH)),
        _layer_spec((CONV_W, CONV_DIM)),
        _layer_spec((1, MIX_WIDTH)),
        _matrix_spec(_MATRIX_SHAPES["w_out"]),
        _layer_spec((1, D_MODEL)),
        _matrix_spec(_MATRIX_SHAPES["w_gu"]),
        _matrix_spec(_MATRIX_SHAPES["w_down"]),
    ]


_WEIGHT_ORDER = ("g_attn", "w_in", "bd", "gq", "gk", "gmq", "conv_w", "g_out", "w_out", "g_ffn",
                 "w_gu", "w_down")

_SMEM_SPEC = pl.BlockSpec(memory_space=pltpu.SMEM)


def _prompt_layer(layer, x, mem_k_t, mem_v_t, sinks, weights, f32_matrices):
    batch, seq, _ = x.shape
    tm = TM_PROMPT
    tps = seq // tm
    n_tiles = batch * tps

    def front(s):
        return jnp.minimum(s, n_tiles - 1)

    def back(s):
        return jnp.maximum(s - 1, 0)

    assert max(_CAST_BLOCKS.values()) <= n_tiles
    casts = [_cast_specs(n, _CAST_BLOCKS[n], lambda s, l: jnp.minimum(l[0] + 1, DEPTH - 1),
                         lambda s, l: s) for n in _MATRIX_ORDER]
    in_specs = [
        _SMEM_SPEC,
        pl.BlockSpec((1, tm, D_MODEL), lambda s, l: (front(s) // tps, front(s) % tps, 0)),
        pl.BlockSpec((1, 1, MEM_WIDTH, N_MEM), lambda s, l: (l[0], front(s) // tps, 0, 0)),
        pl.BlockSpec((1, 1, MEM_WIDTH, N_MEM), lambda s, l: (l[0], front(s) // tps, 0, 0)),
    ] + _weight_specs() + [c[0] for c in casts]
    out_shape = (
        jax.ShapeDtypeStruct((batch, seq, D_MODEL), F32),
        jax.ShapeDtypeStruct((batch, KV_WIDTH, WINDOW), F32),
        jax.ShapeDtypeStruct((batch, KV_WIDTH, WINDOW), F32),
        jax.ShapeDtypeStruct((batch, CONV_W - 1, CONV_DIM), F32),
    ) + tuple(c[2] for c in casts)
    out_specs = (
        pl.BlockSpec((1, tm, D_MODEL), lambda s, l: (back(s) // tps, back(s) % tps, 0)),
        pl.BlockSpec((1, KV_WIDTH, WINDOW), lambda s, l: (front(s) // tps, 0, 0)),
        pl.BlockSpec((1, KV_WIDTH, WINDOW), lambda s, l: (front(s) // tps, 0, 0)),
        pl.BlockSpec((1, CONV_W - 1, CONV_DIM), lambda s, l: (front(s) // tps, 0, 0)),
    ) + tuple(c[1] for c in casts)
    scratch = [
        pltpu.VMEM((KV_HEADS, WINDOW, 2 * HEAD_DIM), F32),
        pltpu.VMEM((KV_HEADS, WINDOW, 2 * HEAD_DIM), F32),
        pltpu.VMEM((8, CONV_DIM), F32),
        pltpu.VMEM((8 + tm, CONV_DIM), F32),
        pltpu.VMEM((tm, ATTN_WIDTH), F32),
        pltpu.VMEM((tm, MEM_WIDTH), F32),
        pltpu.VMEM((tm, MIX_WIDTH), BF),
        pltpu.VMEM((tm, D_MODEL), F32),
    ]
    return pl.pallas_call(
        functools.partial(_prompt_layer_kernel, tiles_per_seq=tps),
        grid_spec=pltpu.PrefetchScalarGridSpec(
            num_scalar_prefetch=1, grid=(n_tiles + 1,), in_specs=in_specs,
            out_specs=out_specs, scratch_shapes=scratch),
        out_shape=out_shape,
        compiler_params=pltpu.CompilerParams(
            dimension_semantics=("arbitrary",),
            vmem_limit_bytes=VMEM_LIMIT_BYTES),
        name="prompt_layer",
    )(layer, sinks, x, mem_k_t, mem_v_t, *[weights[n] for n in _WEIGHT_ORDER],
      *[f32_matrices[n] for n in _MATRIX_ORDER])


_N_SAMPLE_HALF_UNITS = (G_SAMPLE // 2 // SAMPLE_LOCKSTEP) * (
    SAMPLE_LOCKSTEP * (1 + KV_HEADS * 4 + 3) + 1)


def _sample_layer_kernel(layer_ref, sinks_ref, x_ref, ck_ref, cv_ref, cc_ref, mk_ref, mv_ref,
                         g_attn_ref, w_in_ref, bd_ref, gq_ref, gk_ref, gmq_ref, conv_w_ref,
                         g_out_ref, w_out_ref, g_ffn_ref, w_gu_ref, w_down_ref, y_ref, newk_ref,
                         newv_ref, newc_ref, conv_scr, a_scr, mix_scr):
    rows = CHUNK
    half = G_SAMPLE // 2
    half_rows = half * rows
    layer = layer_ref[0]
    g_out = g_out_ref[0]
    x = x_ref[...]
    conv_w = conv_w_ref[0]
    old_lane = _lane_iota((KV_WIDTH, WINDOW)) < WINDOW - rows

    def project_units(hh, proj):
        xn = _rms(x[hh * half_rows:(hh + 1) * half_rows], g_attn_ref[0]).astype(BF)
        blocks = []
        for c0 in range(0, IN_WIDTH, PROJ_BLOCK):
            blocks.append(_dot(xn, w_in_ref[0, :, c0:c0 + PROJ_BLOCK]))
            yield
        qn, kn, v, cb, ccx, mqn = _finish_projection(
            jnp.concatenate(blocks, axis=-1), bd_ref[...], gq_ref[0], gk_ref[0], gmq_ref[0])
        q_lo, q_hi = _split_heads_lo_hi(qn)
        grp = _lane_iota(mqn.shape) // HEAD_DIM
        mq_heads = [jnp.where(grp == h, mqn, 0.0).astype(BF) for h in range(MEM_HEADS)]
        proj.update(q_lo=q_lo, q_hi=q_hi, kn=kn, v=v, cb=cb, ccx=ccx, mq_heads=mq_heads)
        yield

    def shifted_window(cache_t, new_t):
        return jnp.where(old_lane, pltpu.roll(cache_t, WINDOW - rows, axis=1), new_t)

    repeat_bias = jnp.where(_lane_iota((1, 2 * WINDOW)) < WINDOW + rows, 0.0, NEG)

    def window_chain(proj, b, j, kk_bf, vv_bf):
        r0 = b * rows
        c0 = j * GROUP * HEAD_DIM
        qf = _stack_group_queries(proj["q_lo"], proj["q_hi"], r0 % half_rows, rows, j)
        kj_t = kk_bf[j * HEAD_DIM:(j + 1) * HEAD_DIM, :]
        vj_t = vv_bf[j * HEAD_DIM:(j + 1) * HEAD_DIM, :]
        sc = _dot(qf, jnp.concatenate([kj_t, kj_t], axis=0)) + repeat_bias
        yield
        probs = []
        for h in range(GROUP):
            probs.append(_softmax_rows(sc[h * rows:(h + 1) * rows],
                                       sinks_ref[layer, j * GROUP + h]).astype(BF))
            if h % 2 == 1:
                yield
        o_all = _dot_nt(jnp.concatenate(probs, axis=0), jnp.concatenate([vj_t] * GROUP, axis=0))
        a_scr[r0:r0 + rows, c0:c0 + GROUP * HEAD_DIM] = _pick_head_lanes(o_all, rows)
        yield

    def memory_chain(proj, b):
        r0 = b * rows
        q0 = r0 % half_rows
        mq_blocks = jnp.concatenate(
            [proj["mq_heads"][h][q0:q0 + rows] for h in range(MEM_HEADS)], axis=0)
        sc = _dot(mq_blocks, mk_ref[0, b].astype(BF))
        yield
        pm = _softmax_rows(sc, None).astype(BF)
        yield
        mo_b = _pick_head_lanes(_dot_nt(pm, mv_ref[0, b].astype(BF)), rows)
        mix_scr[r0:r0 + rows, _C_END:] = _rms(mo_b, g_out[:, _C_END:]).astype(BF)
        yield

    def mixer_units(hh, proj):
        for b0 in range(hh * half, (hh + 1) * half, SAMPLE_LOCKSTEP):
            chains = []
            for b in range(b0, b0 + SAMPLE_LOCKSTEP):
                r0 = b * rows
                q0 = r0 % half_rows
                ck_t = ck_ref[0, b]
                cv_t = cv_ref[0, b]
                kn_b = proj["kn"][q0:q0 + rows]
                v_b = proj["v"][q0:q0 + rows]
                kn_t = jnp.concatenate([kn_b, kn_b], axis=0).T
                v_t = jnp.concatenate([v_b, v_b], axis=0).T
                newk_ref[b] = shifted_window(ck_t, kn_t)
                newv_ref[b] = shifted_window(cv_t, v_t)
                kk_bf = jnp.concatenate([ck_t, kn_t], axis=1).astype(BF)
                vv_bf = jnp.concatenate([cv_t, v_t], axis=1).astype(BF)

                ccx_b = proj["ccx"][q0:q0 + rows]
                conv_scr[6:8, :] = cc_ref[0, b]
                conv_scr[8:8 + rows, :] = ccx_b
                cy_b = _conv_from_scratch(conv_scr, ccx_b, proj["cb"][q0:q0 + rows], conv_w, rows)
                mix_scr[r0:r0 + rows, _A_END:_C_END] = _rms(cy_b,
                                                            g_out[:, _A_END:_C_END]).astype(BF)
                newc_ref[b] = ccx_b[rows - (CONV_W - 1):rows, :]
                yield
                chains += [window_chain(proj, b, j, kk_bf, vv_bf) for j in range(KV_HEADS)]
                chains.append(memory_chain(proj, b))
            yield from _round_robin(chains)
            r0, r1 = b0 * rows, (b0 + SAMPLE_LOCKSTEP) * rows
            mix_scr[r0:r1, :_A_END] = _rms(a_scr[r0:r1, :], g_out[:, :_A_END]).astype(BF)
            yield

    def output_units(r_lo, r_hi):
        h = x[r_lo:r_hi] + _dot(mix_scr[r_lo:r_hi, :], w_out_ref[0])
        hn = _rms(h, g_ffn_ref[0]).astype(BF)
        yield

        def store_y(c0, cols):
            y_ref[r_lo:r_hi, c0:c0 + cols.shape[1]] = cols

        yield from _swiglu_units(h, hn, w_gu_ref, w_down_ref, store_y)

    n_proj_units = IN_WIDTH // PROJ_BLOCK + 1
    proj_a, proj_b = {}, {}
    for _ in project_units(0, proj_a):
        pass
    _merge(project_units(1, proj_b), [1.0] * n_proj_units,
           mixer_units(0, proj_a), [1.0] * _N_SAMPLE_HALF_UNITS)
    _merge(output_units(0, half_rows), [4.0] + _SWIGLU_COSTS,
           mixer_units(1, proj_b), [1.0] * _N_SAMPLE_HALF_UNITS)
    for _ in output_units(half_rows, 2 * half_rows):
        pass


def _sample_layer(layer, x, cache_k_t, cache_v_t, cache_conv, mem_k_t, mem_v_t, sinks, weights):
    n_b = cache_k_t.shape[1]
    g = G_SAMPLE
    tm = g * CHUNK
    in_specs = [
        _SMEM_SPEC,
        pl.BlockSpec((tm, D_MODEL), lambda i, l: (i, 0)),
        pl.BlockSpec((1, g, KV_WIDTH, WINDOW), lambda i, l: (l[0], i, 0, 0)),
        pl.BlockSpec((1, g, KV_WIDTH, WINDOW), lambda i, l: (l[0], i, 0, 0)),
        pl.BlockSpec((1, g, CONV_W - 1, CONV_DIM), lambda i, l: (l[0], i, 0, 0)),
        pl.BlockSpec((1, g, MEM_WIDTH, N_MEM), lambda i, l: (l[0], i, 0, 0)),
        pl.BlockSpec((1, g, MEM_WIDTH, N_MEM), lambda i, l: (l[0], i, 0, 0)),
    ] + _weight_specs()
    out_shape = (
        jax.ShapeDtypeStruct((n_b * CHUNK, D_MODEL), F32),
        jax.ShapeDtypeStruct((n_b, KV_WIDTH, WINDOW), F32),
        jax.ShapeDtypeStruct((n_b, KV_WIDTH, WINDOW), F32),
        jax.ShapeDtypeStruct((n_b, CONV_W - 1, CONV_DIM), F32),
    )
    out_specs = (
        pl.BlockSpec((tm, D_MODEL), lambda i, l: (i, 0)),
        pl.BlockSpec((g, KV_WIDTH, WINDOW), lambda i, l: (i, 0, 0)),
        pl.BlockSpec((g, KV_WIDTH, WINDOW), lambda i, l: (i, 0, 0)),
        pl.BlockSpec((g, CONV_W - 1, CONV_DIM), lambda i, l: (i, 0, 0)),
    )
    scratch = [
        pltpu.VMEM((8 + CHUNK, CONV_DIM), F32),
        pltpu.VMEM((tm, ATTN_WIDTH), F32),
        pltpu.VMEM((tm, MIX_WIDTH), BF),
    ]
    return pl.pallas_call(
        _sample_layer_kernel,
        grid_spec=pltpu.PrefetchScalarGridSpec(
            num_scalar_prefetch=1, grid=(n_b // g,), in_specs=in_specs, out_specs=out_specs,
            scratch_shapes=scratch),
        out_shape=out_shape,
        compiler_params=pltpu.CompilerParams(
            dimension_semantics=("arbitrary",),
            vmem_limit_bytes=VMEM_LIMIT_BYTES),
        name="sample_layer",
    )(layer, sinks, x, cache_k_t, cache_v_t, cache_conv, mem_k_t, mem_v_t,
      *[weights[n] for n in _WEIGHT_ORDER])


def _mem_kv_kernel(mem_ref, g_mem_ref, w_ref, bd_ref, gmk_ref, w0_in_ref, w0_out_ref, w0_gu_ref,
                   w0_down_ref, mk_ref, mv_ref, w0_in_bf_ref, w0_out_bf_ref, w0_gu_bf_ref,
                   w0_down_bf_ref):
    w = w_ref[0].astype(BF)
    for b in range(mem_ref.shape[0]):
        xn = _rms(mem_ref[b], g_mem_ref[0]).astype(BF)
        kv = _dot(xn, w)
        mk_ref[0, b] = _head_rms(kv[:, :MEM_WIDTH], gmk_ref[0], bd_ref[...]).T
        mv_ref[0, b] = kv[:, MEM_WIDTH:].T
    w0_in_bf_ref[...] = w0_in_ref[...].astype(BF)
    w0_out_bf_ref[...] = w0_out_ref[...].astype(BF)
    w0_gu_bf_ref[...] = w0_gu_ref[...].astype(BF)
    w0_down_bf_ref[...] = w0_down_ref[...].astype(BF)


def _mem_kv(mem, g_mem, w_mem_kv, bd, gmk, f32_matrices):
    batch = mem.shape[0]
    out = jax.ShapeDtypeStruct((DEPTH, batch, MEM_WIDTH, N_MEM), F32)
    casts = [_cast_specs(n, DEPTH, lambda l: 0, lambda l: l) for n in _MATRIX_ORDER]
    return pl.pallas_call(
        _mem_kv_kernel,
        grid=(DEPTH,),
        in_specs=[
            pl.BlockSpec((batch, N_MEM, D_MODEL), lambda l: (0, 0, 0),
                         pipeline_mode=pl.Buffered(1)),
            pl.BlockSpec((1, 1, D_MODEL), lambda l: (l, 0, 0)),
            pl.BlockSpec((1, D_MODEL, 2 * MEM_WIDTH), lambda l: (l, 0, 0)),
            pl.BlockSpec((STAT_WIDTH, STAT_WIDTH), lambda l: (0, 0)),
            pl.BlockSpec((1, 1, MEM_WIDTH), lambda l: (l, 0, 0)),
        ] + [c[0] for c in casts],
        out_specs=(pl.BlockSpec((1, batch, MEM_WIDTH, N_MEM), lambda l: (l, 0, 0, 0)),
                   pl.BlockSpec((1, batch, MEM_WIDTH, N_MEM), lambda l: (l, 0, 0, 0)))
        + tuple(c[1] for c in casts),
        out_shape=(out, out) + tuple(c[2] for c in casts),
        compiler_params=pltpu.CompilerParams(
            dimension_semantics=("arbitrary",),
            vmem_limit_bytes=VMEM_LIMIT_BYTES),
        name="prompt_mem_kv",
    )(mem, g_mem, w_mem_kv, bd, gmk, *[f32_matrices[n] for n in _MATRIX_ORDER])


def _tile_heads(g, n):
    return jnp.tile(g, (1, n)).reshape(g.shape[0], 1, n * g.shape[1])


def _feature_major(c):
    lead = c.shape[:-3]
    n_tok, heads, dim = c.shape[-3:]
    perm = tuple(range(len(lead))) + (len(lead) + 1, len(lead) + 2, len(lead))
    return jnp.transpose(c, perm).reshape(lead + (heads * dim, n_tok))


def _token_major(c_t, heads):
    lead = c_t.shape[:-2]
    n_tok = c_t.shape[-1]
    c = c_t.reshape(lead + (heads, HEAD_DIM, n_tok))
    perm = tuple(range(len(lead))) + (len(lead) + 2, len(lead), len(lead) + 1)
    return jnp.transpose(c, perm)


def kernel(x_prompt, x_sample, mem_prompt, cache_win_k, cache_win_v, cache_conv, cache_mem_k,
           cache_mem_v, attn_norm_g, w_in, q_norm_g, k_norm_g, sinks, conv_w, mem_norm_g,
           w_mem_kv, mq_norm_g, mk_norm_g, out_norm_g, w_out, ffn_norm_g, w_gate_up, w_down):
    batch, seq, _ = x_prompt.shape
    dec_batch, dec_seq, _ = x_sample.shape
    assert dec_seq == CHUNK and seq % TM_PROMPT == 0 and dec_batch % G_SAMPLE == 0
    assert G_SAMPLE % (2 * SAMPLE_LOCKSTEP) == 0

    head = jnp.arange(STAT_WIDTH) // HEAD_DIM
    bd = jnp.where(head[:, None] == head[None, :], 1.0 / HEAD_DIM, 0.0).astype(BF)

    f32_matrices = dict(w_in=w_in, w_out=w_out, w_gu=w_gate_up, w_down=w_down)
    small = dict(
        g_attn=attn_norm_g.reshape(DEPTH, 1, D_MODEL), bd=bd,
        gq=_tile_heads(q_norm_g, N_HEADS), gk=_tile_heads(k_norm_g, KV_HEADS),
        gmq=_tile_heads(mq_norm_g, MEM_HEADS), conv_w=conv_w,
        g_out=out_norm_g.reshape(DEPTH, 1, MIX_WIDTH),
        g_ffn=ffn_norm_g.reshape(DEPTH, 1, D_MODEL))

    mk_t, mv_t, *first = _mem_kv(mem_prompt, mem_norm_g.reshape(DEPTH, 1, D_MODEL), w_mem_kv, bd,
                                 _tile_heads(mk_norm_g, MEM_HEADS), f32_matrices)
    matrices = dict(zip(_MATRIX_ORDER, first))

    ck_t = _feature_major(cache_win_k)
    cv_t = _feature_major(cache_win_v)
    cmk_t = _feature_major(cache_mem_k)
    cmv_t = _feature_major(cache_mem_v)

    yp = x_prompt
    ys = x_sample.reshape(dec_batch * dec_seq, D_MODEL)
    wk_p, wv_p, cv_p, wk_s, wv_s, cv_s = [], [], [], [], [], []
    for l in range(DEPTH):
        layer = jnp.full((1,), l, jnp.int32)
        weights = dict(small, **matrices)
        yp, k_p, v_p, c_p, *next_matrices = _prompt_layer(layer, yp, mk_t, mv_t, sinks, weights,
                                                          f32_matrices)
        wk_p.append(k_p); wv_p.append(v_p); cv_p.append(c_p)
        ys, k_s, v_s, c_s = _sample_layer(layer, ys, ck_t, cv_t, cache_conv, cmk_t, cmv_t, sinks,
                                          weights)
        wk_s.append(k_s); wv_s.append(v_s); cv_s.append(c_s)
        matrices = dict(zip(_MATRIX_ORDER, next_matrices))

    return (yp, ys.reshape(dec_batch, dec_seq, D_MODEL),
            _token_major(jnp.stack(wk_p), KV_HEADS), _token_major(jnp.stack(wv_p), KV_HEADS),
            jnp.stack(cv_p),
            _token_major(mk_t, MEM_HEADS), _token_major(mv_t, MEM_HEADS),
            _token_major(jnp.stack(wk_s), KV_HEADS), _token_major(jnp.stack(wv_s), KV_HEADS),
            jnp.stack(cv_s))
```

```python
import functools

import jax
import jax.numpy as jnp
from jax import lax
from jax.experimental import pallas as pl
from jax.experimental.pallas import tpu as pltpu

D_MODEL = 1024
DEPTH = 4
CHUNK = 64
HEAD_DIM = 64
N_HEADS = 8
KV_HEADS = 2
GROUP = N_HEADS // KV_HEADS
WINDOW = 128
ATTN_WIDTH = N_HEADS * HEAD_DIM
KV_WIDTH = KV_HEADS * HEAD_DIM
CONV_DIM = 256
CONV_W = 3
MEM_HEADS = 4
MEM_WIDTH = MEM_HEADS * HEAD_DIM
N_MEM = 256
MIX_WIDTH = ATTN_WIDTH + CONV_DIM + MEM_WIDTH
IN_WIDTH = ATTN_WIDTH + 2 * KV_WIDTH + 3 * CONV_DIM + MEM_WIDTH
D_FF = 2816
EPS = 1e-6
ATTN_SCALE = HEAD_DIM ** -0.5
NEG = -1e30

_Q0, _K0, _V0 = 0, ATTN_WIDTH, ATTN_WIDTH + KV_WIDTH
_CB0 = ATTN_WIDTH + 2 * KV_WIDTH
_CC0 = _CB0 + CONV_DIM
_CX0 = _CC0 + CONV_DIM
_MQ0 = _CX0 + CONV_DIM
_A_END = ATTN_WIDTH
_C_END = ATTN_WIDTH + CONV_DIM

PAIR = 2 * CHUNK
TM_PROMPT = 256
G_SAMPLE = 8
SAMPLE_LOCKSTEP = 2
FF_BLOCK = 256
DOWN_SPLIT = 4
PROJ_BLOCK = 256
VMEM_LIMIT_BYTES = 56 * 1024 * 1024

BF = jnp.bfloat16
F32 = jnp.float32


def _dot(a, b):
    return jnp.dot(a, b, preferred_element_type=F32)


def _dot_nt(a, b):
    return lax.dot_general(a, b, (((1,), (1,)), ((), ())), preferred_element_type=F32)


def _rms(x, g):
    ms = jnp.mean(x * x, axis=-1, keepdims=True)
    return x * lax.rsqrt(ms + EPS) * g


def _head_rms(z, g, bd):
    w = z.shape[-1]
    ms = _dot((z * z).astype(BF), bd[:w, :w])
    return z * lax.rsqrt(ms + EPS) * g


def _lane_iota(shape):
    return lax.broadcasted_iota(jnp.int32, shape, len(shape) - 1)


def _split_heads_lo_hi(z):
    lo = (_lane_iota(z.shape) % (2 * HEAD_DIM)) < HEAD_DIM
    return jnp.where(lo, z, 0.0).astype(BF), jnp.where(lo, 0.0, z).astype(BF)


def _stack_group_queries(q_lo, q_hi, r0, rows, j):
    c0 = j * GROUP * HEAD_DIM
    return jnp.concatenate(
        [q_lo[r0:r0 + rows, c0:c0 + 128], q_hi[r0:r0 + rows, c0:c0 + 128],
         q_lo[r0:r0 + rows, c0 + 128:c0 + 256], q_hi[r0:r0 + rows, c0 + 128:c0 + 256]], axis=0)


def _dup_halves(z):
    sw = pltpu.roll(z, HEAD_DIM, axis=1)
    lo = _lane_iota(z.shape) < HEAD_DIM
    return jnp.where(lo, z, sw), jnp.where(lo, sw, z)


def _pick_head_lanes(o_all, rows):
    grp = _lane_iota((rows, GROUP * HEAD_DIM)) // HEAD_DIM
    out = o_all[3 * rows:4 * rows]
    for h in (2, 1, 0):
        out = jnp.where(grp == h, o_all[h * rows:(h + 1) * rows], out)
    return out


def _softmax_rows(s, sink):
    m = jnp.max(s, axis=-1, keepdims=True)
    if sink is not None:
        m = jnp.maximum(m, sink)
    p = jnp.exp(s - m)
    den = jnp.sum(p, axis=-1, keepdims=True)
    if sink is not None:
        den = den + jnp.exp(sink - m)
    return p * (1.0 / den)


def _finish_projection(u, bd, gq, gk, gmq):
    half = ATTN_WIDTH // 2
    qn = jnp.concatenate(
        [_head_rms(u[:, _Q0:_Q0 + half], gq[:, :half], bd),
         _head_rms(u[:, _Q0 + half:_K0], gq[:, half:], bd)], axis=-1) * ATTN_SCALE
    kn = _head_rms(u[:, _K0:_V0], gk, bd)
    v = u[:, _V0:_CB0]
    cb = u[:, _CB0:_CC0]
    ccx = u[:, _CC0:_CX0] * u[:, _CX0:_MQ0]
    mqn = _head_rms(u[:, _MQ0:], gmq, bd) * ATTN_SCALE
    return qn, kn, v, cb, ccx, mqn


def _conv_from_scratch(conv_scr, ccx, cb, conv_w, rows):
    sh2 = conv_scr[6:6 + rows, :]
    sh1 = conv_scr[7:7 + rows, :]
    return cb * (sh2 * conv_w[0:1, :] + sh1 * conv_w[1:2, :] + ccx * conv_w[2:3, :])


def _swiglu_unit_costs():
    n_blk = D_FF // FF_BLOCK
    costs = []
    for b in range(n_blk):
        costs += [1.0, 1.0]
        if b >= 2 and b % 2 == 0:
            costs += [2.0 / DOWN_SPLIT] * DOWN_SPLIT
    for c in range((n_blk - 1) // 2, (n_blk + 1) // 2):
        costs += [float(min(2 * c + 2, n_blk) - 2 * c) / DOWN_SPLIT] * DOWN_SPLIT
    return costs


_SWIGLU_COSTS = _swiglu_unit_costs()


def _swiglu_units(h, hn, w_gu_ref, w_down_ref, store):
    blk = FF_BLOCK
    n_blk = D_FF // blk
    n_down = (n_blk + 1) // 2
    width = D_MODEL // DOWN_SPLIT
    acts = []
    y = [h[:, k * width:(k + 1) * width] for k in range(DOWN_SPLIT)]

    def down_units(c):
        lo, hi = 2 * c, min(2 * c + 2, n_blk)
        act = acts[lo] if hi - lo == 1 else jnp.concatenate(acts[lo:hi], axis=-1)
        for k in range(DOWN_SPLIT):
            y[k] = y[k] + _dot(act, w_down_ref[0, lo * blk:hi * blk, k * width:(k + 1) * width])
            if c == n_down - 1:
                store(k * width, y[k])
            yield

    for b in range(n_blk):
        gate = _dot(hn, w_gu_ref[0, :, b * blk:(b + 1) * blk])
        yield
        up = _dot(hn, w_gu_ref[0, :, D_FF + b * blk:D_FF + (b + 1) * blk])
        acts.append((gate * jax.nn.sigmoid(gate) * up).astype(BF))
        yield
        if b >= 2 and b % 2 == 0:
            yield from down_units(b // 2 - 1)
    for c in range((n_blk - 1) // 2, n_down):
        yield from down_units(c)


def _merge(first, first_costs, second, second_costs):
    streams = [[first, list(first_costs), 0.0, sum(first_costs)],
               [second, list(second_costs), 0.0, sum(second_costs)]]
    while streams[0][1] or streams[1][1]:
        live = [st for st in streams if st[1]]
        st = min(live, key=lambda st: st[2] / st[3])
        next(st[0])
        st[2] += st[1].pop(0)
        if not st[1]:
            assert next(st[0], "end") == "end"


def _round_robin(chains):
    chains = list(chains)
    while chains:
        for chain in list(chains):
            try:
                next(chain)
            except StopIteration:
                chains.remove(chain)
            else:
                yield


_PROMPT_COSTS = ([2.0, 2.0, 1.5] + [1.0, 0.8, 0.8, 1.0] * ((TM_PROMPT // PAIR) * KV_HEADS) + [3.0]
                 + [0.8, 1.2, 0.8] * MEM_HEADS)


def _prompt_layer_kernel(layer_ref, sinks_ref, x_ref, mk_ref, mv_ref, g_attn_ref, w_in_ref, bd_ref,
                         gq_ref, gk_ref, gmq_ref, conv_w_ref, g_out_ref, w_out_ref, g_ffn_ref,
                         w_gu_ref, w_down_ref, nw_in_ref, nw_out_ref, nw_gu_ref, nw_down_ref,
                         y_ref, newk_ref, newv_ref, newc_ref,
                         nw_in_bf_ref, nw_out_bf_ref, nw_gu_bf_ref, nw_down_bf_ref,
                         kcar_scr, vcar_scr, ccar_scr, conv_scr, a_scr, mo_scr, mix_scr,
                         xres_scr, *,
                         tiles_per_seq):
    tm = TM_PROMPT
    layer = layer_ref[0]
    s = pl.program_id(0)
    n_tiles = pl.num_programs(0) - 1
    t = jnp.minimum(s, n_tiles - 1) % tiles_per_seq

    @pl.when(s == 0)
    def _():
        mix_scr[...] = jnp.zeros(mix_scr.shape, BF)
        xres_scr[...] = jnp.zeros(xres_scr.shape, F32)

    nw_in_bf_ref[...] = nw_in_ref[...].astype(BF)
    nw_out_bf_ref[...] = nw_out_ref[...].astype(BF)
    nw_gu_bf_ref[...] = nw_gu_ref[...].astype(BF)
    nw_down_bf_ref[...] = nw_down_ref[...].astype(BF)

    h_prev = xres_scr[...] + _dot(mix_scr[...], w_out_ref[0])
    hn_prev = _rms(h_prev, g_ffn_ref[0]).astype(BF)
    def store_y(c0, cols):
        y_ref[0, :, c0:c0 + cols.shape[1]] = cols

    back = _swiglu_units(h_prev, hn_prev, w_gu_ref, w_down_ref, store_y)

    started = t > 0
    k_prev = [jnp.where(started, kcar_scr[j], 0.0).astype(BF) for j in range(KV_HEADS)]
    v_prev = [jnp.where(started, vcar_scr[j], 0.0).astype(BF) for j in range(KV_HEADS)]
    conv_scr[0:8, :] = jnp.where(started, ccar_scr[...], 0.0)

    x = x_ref[0]
    bd = bd_ref[...]
    u = _dot(_rms(x, g_attn_ref[0]).astype(BF), w_in_ref[0])
    xres_scr[...] = x

    row_chunk = lax.broadcasted_iota(jnp.int32, (PAIR, 2 * PAIR), 0) // CHUNK
    col = _lane_iota((PAIR, 2 * PAIR))
    first_key = jnp.where(started, 0, WINDOW)
    visible = (col >= row_chunk * CHUNK) & (col < (row_chunk + 3) * CHUNK)
    bias_rest = jnp.where(visible, 0.0, NEG)
    bias_first = jnp.where(col >= first_key, bias_rest, NEG)

    mk_t = mk_ref[0, 0].astype(BF)
    mv_t = mv_ref[0, 0].astype(BF)
    g_out = g_out_ref[0]

    kept = {}

    def front_units():
        kn = _head_rms(u[:, _K0:_V0], gk_ref[0], bd)
        v = u[:, _V0:_CB0]
        k_dup = _dup_halves(kn)
        v_dup = _dup_halves(v)
        for j in range(KV_HEADS):
            kcar_scr[j] = k_dup[j][tm - WINDOW:tm, :]
            vcar_scr[j] = v_dup[j][tm - WINDOW:tm, :]
        k_rep = [k_dup[j].astype(BF) for j in range(KV_HEADS)]
        v_rep = [v_dup[j].astype(BF) for j in range(KV_HEADS)]
        kept.update(kn=kn, v=v)
        yield

        gq = gq_ref[0]
        q_half = ATTN_WIDTH // 2
        qn = jnp.concatenate(
            [_head_rms(u[:, _Q0:_Q0 + q_half], gq[:, :q_half], bd),
             _head_rms(u[:, _Q0 + q_half:_K0], gq[:, q_half:], bd)], axis=-1) * ATTN_SCALE
        q_lo, q_hi = _split_heads_lo_hi(qn)
        yield

        ccx = u[:, _CC0:_CX0] * u[:, _CX0:_MQ0]
        conv_scr[8:8 + tm, :] = ccx
        cy = _conv_from_scratch(conv_scr, ccx, u[:, _CB0:_CC0], conv_w_ref[0], tm)
        ccar_scr[...] = ccx[tm - 8:tm, :]
        mix_scr[:, _A_END:_C_END] = _rms(cy, g_out[:, _A_END:_C_END]).astype(BF)
        kept.update(ccx=ccx)
        yield

        for p in range(tm // PAIR):
            r0 = p * PAIR
            bias = bias_first if p == 0 else bias_rest
            for j in range(KV_HEADS):
                c0 = j * GROUP * HEAD_DIM
                qf = _stack_group_queries(q_lo, q_hi, r0, PAIR, j)
                if p == 0:
                    k_win = jnp.concatenate([k_prev[j], k_rep[j][0:PAIR]], axis=0)
                    v_win = jnp.concatenate([v_prev[j], v_rep[j][0:PAIR]], axis=0)
                else:
                    k_win = k_rep[j][r0 - PAIR:r0 + PAIR]
                    v_win = v_rep[j][r0 - PAIR:r0 + PAIR]
                sc = _dot_nt(qf, k_win)
                yield
                probs = []
                for h in range(GROUP):
                    sh = sc[h * PAIR:(h + 1) * PAIR] + bias
                    probs.append(_softmax_rows(sh, sinks_ref[layer, j * GROUP + h]).astype(BF))
                    if h % 2 == 1:
                        yield
                o_all = _dot(jnp.concatenate(probs, axis=0),
                             jnp.concatenate([v_win, v_win], axis=-1))
                a_scr[r0:r0 + PAIR, c0:c0 + GROUP * HEAD_DIM] = _pick_head_lanes(o_all, PAIR)
                yield
        mix_scr[:, :_A_END] = _rms(a_scr[...], g_out[:, :_A_END]).astype(BF)
        mqn = _head_rms(u[:, _MQ0:], gmq_ref[0], bd) * ATTN_SCALE
        grp = _lane_iota(mqn.shape) // HEAD_DIM
        yield
        for h in range(MEM_HEADS):
            mq_h = jnp.where(grp == h, mqn, 0.0).astype(BF)
            sc = _dot(mq_h, mk_t)
            yield
            pm = _softmax_rows(sc, None).astype(BF)
            yield
            o_h = _dot_nt(pm, mv_t)
            mo_scr[:, h * HEAD_DIM:(h + 1) * HEAD_DIM] = o_h[:, h * HEAD_DIM:(h + 1) * HEAD_DIM]
            yield

    def front_then_norm():
        yield from front_units()
        mix_scr[:, _C_END:] = _rms(mo_scr[...], g_out[:, _C_END:]).astype(BF)

    _merge(back, _SWIGLU_COSTS, front_then_norm(), _PROMPT_COSTS)

    @pl.when(t == tiles_per_seq - 1)
    def _():
        newk_ref[0] = kept["kn"][tm - WINDOW:tm, :].T
        newv_ref[0] = kept["v"][tm - WINDOW:tm, :].T
        newc_ref[0] = kept["ccx"][tm - (CONV_W - 1):tm, :]


def _layer_spec(shape):
    zeros = (0,) * len(shape)
    return pl.BlockSpec((1,) + shape, lambda *args: (args[-1][0],) + zeros,
                        pipeline_mode=pl.Buffered(1))


def _matrix_spec(shape):
    return pl.BlockSpec((1,) + shape, lambda *args: (0, 0, 0), pipeline_mode=pl.Buffered(1))


_MATRIX_SHAPES = dict(w_in=(D_MODEL, IN_WIDTH), w_out=(MIX_WIDTH, D_MODEL),
                      w_gu=(D_MODEL, 2 * D_FF), w_down=(D_FF, D_MODEL))
_MATRIX_ORDER = ("w_in", "w_out", "w_gu", "w_down")
_CAST_BLOCKS = dict(w_in=32, w_out=32, w_gu=32, w_down=16)


def _cast_specs(name, n_blk, layer_of, step_of):
    rows, cols = _MATRIX_SHAPES[name]
    assert rows % (16 * n_blk) == 0
    blk = (1, rows // n_blk, cols)
    src = pl.BlockSpec(blk, lambda *a: (layer_of(*a), jnp.minimum(step_of(*a), n_blk - 1), 0))
    dst = pl.BlockSpec(blk, lambda *a: (0, jnp.minimum(step_of(*a), n_blk - 1), 0))
    return src, dst, jax.ShapeDtypeStruct((1, rows, cols), BF)


def _weight_specs():
    return [
        _layer_spec((1, D_MODEL)),
        _matrix_spec(_MATRIX_SHAPES["w_in"]),
        pl.BlockSpec((256, 256), lambda *args: (0, 0), pipeline_mode=pl.Buffered(1)),
        _layer_spec((1, ATTN_WIDTH)),
        _layer_spec((1, KV_WIDTH)),
        _layer_spec((1, MEM_WIDTH)),
        _layer_spec((CONV_W, CONV_DIM)),
        _layer_spec((1, MIX_WIDTH)),
        _matrix_spec(_MATRIX_SHAPES["w_out"]),
        _layer_spec((1, D_MODEL)),
        _matrix_spec(_MATRIX_SHAPES["w_gu"]),
        _matrix_spec(_MATRIX_SHAPES["w_down"]),
    ]


_WEIGHT_ORDER = ("g_attn", "w_in", "bd", "gq", "gk", "gmq", "conv_w", "g_out", "w_out", "g_ffn",
                 "w_gu", "w_down")

_SMEM_SPEC = pl.BlockSpec(memory_space=pltpu.SMEM)


def _prompt_layer(layer, x, mem_k_t, mem_v_t, sinks, weights, f32_matrices):
    batch, seq, _ = x.shape
    tm = TM_PROMPT
    tps = seq // tm
    n_tiles = batch * tps

    def front(s):
        return jnp.minimum(s, n_tiles - 1)

    def back(s):
        return jnp.maximum(s - 1, 0)

    assert max(_CAST_BLOCKS.values()) <= n_tiles
    casts = [_cast_specs(n, _CAST_BLOCKS[n], lambda s, l: jnp.minimum(l[0] + 1, DEPTH - 1),
                         lambda s, l: s) for n in _MATRIX_ORDER]
    in_specs = [
        _SMEM_SPEC,
        pl.BlockSpec((1, tm, D_MODEL), lambda s, l: (front(s) // tps, front(s) % tps, 0)),
        pl.BlockSpec((1, 1, MEM_WIDTH, N_MEM), lambda s, l: (l[0], front(s) // tps, 0, 0)),
        pl.BlockSpec((1, 1, MEM_WIDTH, N_MEM), lambda s, l: (l[0], front(s) // tps, 0, 0)),
    ] + _weight_specs() + [c[0] for c in casts]
    out_shape = (
        jax.ShapeDtypeStruct((batch, seq, D_MODEL), F32),
        jax.ShapeDtypeStruct((batch, KV_WIDTH, WINDOW), F32),
        jax.ShapeDtypeStruct((batch, KV_WIDTH, WINDOW), F32),
        jax.ShapeDtypeStruct((batch, CONV_W - 1, CONV_DIM), F32),
    ) + tuple(c[2] for c in casts)
    out_specs = (
        pl.BlockSpec((1, tm, D_MODEL), lambda s, l: (back(s) // tps, back(s) % tps, 0)),
        pl.BlockSpec((1, KV_WIDTH, WINDOW), lambda s, l: (front(s) // tps, 0, 0)),
        pl.BlockSpec((1, KV_WIDTH, WINDOW), lambda s, l: (front(s) // tps, 0, 0)),
        pl.BlockSpec((1, CONV_W - 1, CONV_DIM), lambda s, l: (front(s) // tps, 0, 0)),
    ) + tuple(c[1] for c in casts)
    scratch = [
        pltpu.VMEM((KV_HEADS, WINDOW, 2 * HEAD_DIM), F32),
        pltpu.VMEM((KV_HEADS, WINDOW, 2 * HEAD_DIM), F32),
        pltpu.VMEM((8, CONV_DIM), F32),
        pltpu.VMEM((8 + tm, CONV_DIM), F32),
        pltpu.VMEM((tm, ATTN_WIDTH), F32),
        pltpu.VMEM((tm, MEM_WIDTH), F32),
        pltpu.VMEM((tm, MIX_WIDTH), BF),
        pltpu.VMEM((tm, D_MODEL), F32),
    ]
    return pl.pallas_call(
        functools.partial(_prompt_layer_kernel, tiles_per_seq=tps),
        grid_spec=pltpu.PrefetchScalarGridSpec(
            num_scalar_prefetch=1, grid=(n_tiles + 1,), in_specs=in_specs,
            out_specs=out_specs, scratch_shapes=scratch),
        out_shape=out_shape,
        compiler_params=pltpu.CompilerParams(
            dimension_semantics=("arbitrary",),
            vmem_limit_bytes=VMEM_LIMIT_BYTES),
        name="prompt_layer",
    )(layer, sinks, x, mem_k_t, mem_v_t, *[weights[n] for n in _WEIGHT_ORDER],
      *[f32_matrices[n] for n in _MATRIX_ORDER])


_N_SAMPLE_HALF_UNITS = (G_SAMPLE // 2 // SAMPLE_LOCKSTEP) * (
    SAMPLE_LOCKSTEP * (1 + KV_HEADS * 4 + 3) + 1)


def _sample_layer_kernel(layer_ref, sinks_ref, x_ref, ck_ref, cv_ref, cc_ref, mk_ref, mv_ref,
                         g_attn_ref, w_in_ref, bd_ref, gq_ref, gk_ref, gmq_ref, conv_w_ref,
                         g_out_ref, w_out_ref, g_ffn_ref, w_gu_ref, w_down_ref, y_ref, newk_ref,
                         newv_ref, newc_ref, conv_scr, a_scr, mix_scr):
    rows = CHUNK
    half = G_SAMPLE // 2
    half_rows = half * rows
    layer = layer_ref[0]
    g_out = g_out_ref[0]
    x = x_ref[...]
    conv_w = conv_w_ref[0]
    old_lane = _lane_iota((KV_WIDTH, WINDOW)) < WINDOW - rows

    def project_units(hh, proj):
        xn = _rms(x[hh * half_rows:(hh + 1) * half_rows], g_attn_ref[0]).astype(BF)
        blocks = []
        for c0 in range(0, IN_WIDTH, PROJ_BLOCK):
            blocks.append(_dot(xn, w_in_ref[0, :, c0:c0 + PROJ_BLOCK]))
            yield
        qn, kn, v, cb, ccx, mqn = _finish_projection(
            jnp.concatenate(blocks, axis=-1), bd_ref[...], gq_ref[0], gk_ref[0], gmq_ref[0])
        q_lo, q_hi = _split_heads_lo_hi(qn)
        grp = _lane_iota(mqn.shape) // HEAD_DIM
        mq_heads = [jnp.where(grp == h, mqn, 0.0).astype(BF) for h in range(MEM_HEADS)]
        proj.update(q_lo=q_lo, q_hi=q_hi, kn=kn, v=v, cb=cb, ccx=ccx, mq_heads=mq_heads)
        yield

    def shifted_window(cache_t, new_t):
        return jnp.where(old_lane, pltpu.roll(cache_t, WINDOW - rows, axis=1), new_t)

    repeat_bias = jnp.where(_lane_iota((1, 2 * WINDOW)) < WINDOW + rows, 0.0, NEG)

    def window_chain(proj, b, j, kk_bf, vv_bf):
        r0 = b * rows
        c0 = j * GROUP * HEAD_DIM
        qf = _stack_group_queries(proj["q_lo"], proj["q_hi"], r0 % half_rows, rows, j)
        kj_t = kk_bf[j * HEAD_DIM:(j + 1) * HEAD_DIM, :]
        vj_t = vv_bf[j * HEAD_DIM:(j + 1) * HEAD_DIM, :]
        sc = _dot(qf, jnp.concatenate([kj_t, kj_t], axis=0)) + repeat_bias
        yield
        probs = []
        for h in range(GROUP):
            probs.append(_softmax_rows(sc[h * rows:(h + 1) * rows],
                                       sinks_ref[layer, j * GROUP + h]).astype(BF))
            if h % 2 == 1:
                yield
        o_all = _dot_nt(jnp.concatenate(probs, axis=0), jnp.concatenate([vj_t] * GROUP, axis=0))
        a_scr[r0:r0 + rows, c0:c0 + GROUP * HEAD_DIM] = _pick_head_lanes(o_all, rows)
        yield

    def memory_chain(proj, b):
        r0 = b * rows
        q0 = r0 % half_rows
        mq_blocks = jnp.concatenate(
            [proj["mq_heads"][h][q0:q0 + rows] for h in range(MEM_HEADS)], axis=0)
        sc = _dot(mq_blocks, mk_ref[0, b].astype(BF))
        yield
        pm = _softmax_rows(sc, None).astype(BF)
        yield
        mo_b = _pick_head_lanes(_dot_nt(pm, mv_ref[0, b].astype(BF)), rows)
        mix_scr[r0:r0 + rows, _C_END:] = _rms(mo_b, g_out[:, _C_END:]).astype(BF)
        yield

    def mixer_units(hh, proj):
        for b0 in range(hh * half, (hh + 1) * half, SAMPLE_LOCKSTEP):
            chains = []
            for b in range(b0, b0 + SAMPLE_LOCKSTEP):
                r0 = b * rows
                q0 = r0 % half_rows
                ck_t = ck_ref[0, b]
                cv_t = cv_ref[0, b]
                kn_b = proj["kn"][q0:q0 + rows]
                v_b = proj["v"][q0:q0 + rows]
                kn_t = jnp.concatenate([kn_b, kn_b], axis=0).T
                v_t = jnp.concatenate([v_b, v_b], axis=0).T
                newk_ref[b] = shifted_window(ck_t, kn_t)
                newv_ref[b] = shifted_window(cv_t, v_t)
                kk_bf = jnp.concatenate([ck_t, kn_t], axis=1).astype(BF)
                vv_bf = jnp.concatenate([cv_t, v_t], axis=1).astype(BF)

                ccx_b = proj["ccx"][q0:q0 + rows]
                conv_scr[6:8, :] = cc_ref[0, b]
                conv_scr[8:8 + rows, :] = ccx_b
                cy_b = _conv_from_scratch(conv_scr, ccx_b, proj["cb"][q0:q0 + rows], conv_w, rows)
                mix_scr[r0:r0 + rows, _A_END:_C_END] = _rms(cy_b,
                                                            g_out[:, _A_END:_C_END]).astype(BF)
                newc_ref[b] = ccx_b[rows - (CONV_W - 1):rows, :]
                yield
                chains += [window_chain(proj, b, j, kk_bf, vv_bf) for j in range(KV_HEADS)]
                chains.append(memory_chain(proj, b))
            yield from _round_robin(chains)
            r0, r1 = b0 * rows, (b0 + SAMPLE_LOCKSTEP) * rows
            mix_scr[r0:r1, :_A_END] = _rms(a_scr[r0:r1, :], g_out[:, :_A_END]).astype(BF)
            yield

    def output_units(r_lo, r_hi):
        h = x[r_lo:r_hi] + _dot(mix_scr[r_lo:r_hi, :], w_out_ref[0])
        hn = _rms(h, g_ffn_ref[0]).astype(BF)
        yield

        def store_y(c0, cols):
            y_ref[r_lo:r_hi, c0:c0 + cols.shape[1]] = cols

        yield from _swiglu_units(h, hn, w_gu_ref, w_down_ref, store_y)

    n_proj_units = IN_WIDTH // PROJ_BLOCK + 1
    proj_a, proj_b = {}, {}
    for _ in project_units(0, proj_a):
        pass
    _merge(project_units(1, proj_b), [1.0] * n_proj_units,
           mixer_units(0, proj_a), [1.0] * _N_SAMPLE_HALF_UNITS)
    _merge(output_units(0, half_rows), [4.0] + _SWIGLU_COSTS,
           mixer_units(1, proj_b), [1.0] * _N_SAMPLE_HALF_UNITS)
    for _ in output_units(half_rows, 2 * half_rows):
        pass


def _sample_layer(layer, x, cache_k_t, cache_v_t, cache_conv, mem_k_t, mem_v_t, sinks, weights):
    n_b = cache_k_t.shape[1]
    g = G_SAMPLE
    tm = g * CHUNK
    in_specs = [
        _SMEM_SPEC,
        pl.BlockSpec((tm, D_MODEL), lambda i, l: (i, 0)),
        pl.BlockSpec((1, g, KV_WIDTH, WINDOW), lambda i, l: (l[0], i, 0, 0)),
        pl.BlockSpec((1, g, KV_WIDTH, WINDOW), lambda i, l: (l[0], i, 0, 0)),
        pl.BlockSpec((1, g, CONV_W - 1, CONV_DIM), lambda i, l: (l[0], i, 0, 0)),
        pl.BlockSpec((1, g, MEM_WIDTH, N_MEM), lambda i, l: (l[0], i, 0, 0)),
        pl.BlockSpec((1, g, MEM_WIDTH, N_MEM), lambda i, l: (l[0], i, 0, 0)),
    ] + _weight_specs()
    out_shape = (
        jax.ShapeDtypeStruct((n_b * CHUNK, D_MODEL), F32),
        jax.ShapeDtypeStruct((n_b, KV_WIDTH, WINDOW), F32),
        jax.ShapeDtypeStruct((n_b, KV_WIDTH, WINDOW), F32),
        jax.ShapeDtypeStruct((n_b, CONV_W - 1, CONV_DIM), F32),
    )
    out_specs = (
        pl.BlockSpec((tm, D_MODEL), lambda i, l: (i, 0)),
        pl.BlockSpec((g, KV_WIDTH, WINDOW), lambda i, l: (i, 0, 0)),
        pl.BlockSpec((g, KV_WIDTH, WINDOW), lambda i, l: (i, 0, 0)),
        pl.BlockSpec((g, CONV_W - 1, CONV_DIM), lambda i, l: (i, 0, 0)),
    )
    scratch = [
        pltpu.VMEM((8 + CHUNK, CONV_DIM), F32),
        pltpu.VMEM((tm, ATTN_WIDTH), F32),
        pltpu.VMEM((tm, MIX_WIDTH), BF),
    ]
    return pl.pallas_call(
        _sample_layer_kernel,
        grid_spec=pltpu.PrefetchScalarGridSpec(
            num_scalar_prefetch=1, grid=(n_b // g,), in_specs=in_specs, out_specs=out_specs,
            scratch_shapes=scratch),
        out_shape=out_shape,
        compiler_params=pltpu.CompilerParams(
            dimension_semantics=("arbitrary",),
            vmem_limit_bytes=VMEM_LIMIT_BYTES),
        name="sample_layer",
    )(layer, sinks, x, cache_k_t, cache_v_t, cache_conv, mem_k_t, mem_v_t,
      *[weights[n] for n in _WEIGHT_ORDER])


def _mem_kv_kernel(mem_ref, g_mem_ref, w_ref, bd_ref, gmk_ref, w0_in_ref, w0_out_ref, w0_gu_ref,
                   w0_down_ref, mk_ref, mv_ref, w0_in_bf_ref, w0_out_bf_ref, w0_gu_bf_ref,
                   w0_down_bf_ref):
    w = w_ref[0].astype(BF)
    for b in range(mem_ref.shape[0]):
        xn = _rms(mem_ref[b], g_mem_ref[0]).astype(BF)
        kv = _dot(xn, w)
        mk_ref[0, b] = _head_rms(kv[:, :MEM_WIDTH], gmk_ref[0], bd_ref[...]).T
        mv_ref[0, b] = kv[:, MEM_WIDTH:].T
    w0_in_bf_ref[...] = w0_in_ref[...].astype(BF)
    w0_out_bf_ref[...] = w0_out_ref[...].astype(BF)
    w0_gu_bf_ref[...] = w0_gu_ref[...].astype(BF)
    w0_down_bf_ref[...] = w0_down_ref[...].astype(BF)


def _mem_kv(mem, g_mem, w_mem_kv, bd, gmk, f32_matrices):
    batch = mem.shape[0]
    out = jax.ShapeDtypeStruct((DEPTH, batch, MEM_WIDTH, N_MEM), F32)
    casts = [_cast_specs(n, DEPTH, lambda l: 0, lambda l: l) for n in _MATRIX_ORDER]
    return pl.pallas_call(
        _mem_kv_kernel,
        grid=(DEPTH,),
        in_specs=[
            pl.BlockSpec((batch, N_MEM, D_MODEL), lambda l: (0, 0, 0),
                         pipeline_mode=pl.Buffered(1)),
            pl.BlockSpec((1, 1, D_MODEL), lambda l: (l, 0, 0)),
            pl.BlockSpec((1, D_MODEL, 2 * MEM_WIDTH), lambda l: (l, 0, 0)),
            pl.BlockSpec((256, 256), lambda l: (0, 0)),
            pl.BlockSpec((1, 1, MEM_WIDTH), lambda l: (l, 0, 0)),
        ] + [c[0] for c in casts],
        out_specs=(pl.BlockSpec((1, batch, MEM_WIDTH, N_MEM), lambda l: (l, 0, 0, 0)),
                   pl.BlockSpec((1, batch, MEM_WIDTH, N_MEM), lambda l: (l, 0, 0, 0)))
        + tuple(c[1] for c in casts),
        out_shape=(out, out) + tuple(c[2] for c in casts),
        compiler_params=pltpu.CompilerParams(
            dimension_semantics=("arbitrary",),
            vmem_limit_bytes=VMEM_LIMIT_BYTES),
        name="prompt_mem_kv",
    )(mem, g_mem, w_mem_kv, bd, gmk, *[f32_matrices[n] for n in _MATRIX_ORDER])


def _tile_heads(g, n):
    return jnp.tile(g, (1, n)).reshape(g.shape[0], 1, n * g.shape[1])


def _feature_major(c):
    lead = c.shape[:-3]
    n_tok, heads, dim = c.shape[-3:]
    perm = tuple(range(len(lead))) + (len(lead) + 1, len(lead) + 2, len(lead))
    return jnp.transpose(c, perm).reshape(lead + (heads * dim, n_tok))


def _token_major(c_t, heads):
    lead = c_t.shape[:-2]
    n_tok = c_t.shape[-1]
    c = c_t.reshape(lead + (heads, HEAD_DIM, n_tok))
    perm = tuple(range(len(lead))) + (len(lead) + 2, len(lead), len(lead) + 1)
    return jnp.transpose(c, perm)


def kernel(x_prompt, x_sample, mem_prompt, cache_win_k, cache_win_v, cache_conv, cache_mem_k,
           cache_mem_v, attn_norm_g, w_in, q_norm_g, k_norm_g, sinks, conv_w, mem_norm_g,
           w_mem_kv, mq_norm_g, mk_norm_g, out_norm_g, w_out, ffn_norm_g, w_gate_up, w_down):
    batch, seq, _ = x_prompt.shape
    dec_batch, dec_seq, _ = x_sample.shape
    assert dec_seq == CHUNK and seq % TM_PROMPT == 0 and dec_batch % G_SAMPLE == 0
    assert G_SAMPLE % (2 * SAMPLE_LOCKSTEP) == 0

    head = jnp.arange(256) // HEAD_DIM
    bd = jnp.where(head[:, None] == head[None, :], 1.0 / HEAD_DIM, 0.0).astype(BF)

    f32_matrices = dict(w_in=w_in, w_out=w_out, w_gu=w_gate_up, w_down=w_down)
    small = dict(
        g_attn=attn_norm_g.reshape(DEPTH, 1, D_MODEL), bd=bd,
        gq=_tile_heads(q_norm_g, N_HEADS), gk=_tile_heads(k_norm_g, KV_HEADS),
        gmq=_tile_heads(mq_norm_g, MEM_HEADS), conv_w=conv_w,
        g_out=out_norm_g.reshape(DEPTH, 1, MIX_WIDTH),
        g_ffn=ffn_norm_g.reshape(DEPTH, 1, D_MODEL))

    mk_t, mv_t, *first = _mem_kv(mem_prompt, mem_norm_g.reshape(DEPTH, 1, D_MODEL), w_mem_kv, bd,
                                 _tile_heads(mk_norm_g, MEM_HEADS), f32_matrices)
    matrices = dict(zip(_MATRIX_ORDER, first))

    ck_t = _feature_major(cache_win_k)
    cv_t = _feature_major(cache_win_v)
    cmk_t = _feature_major(cache_mem_k)
    cmv_t = _feature_major(cache_mem_v)

    yp = x_prompt
    ys = x_sample.reshape(dec_batch * dec_seq, D_MODEL)
    wk_p, wv_p, cv_p, wk_s, wv_s, cv_s = [], [], [], [], [], []
    for l in range(DEPTH):
        layer = jnp.full((1,), l, jnp.int32)
        weights = dict(small, **matrices)
        yp, k_p, v_p, c_p, *next_matrices = _prompt_layer(layer, yp, mk_t, mv_t, sinks, weights,
                                                          f32_matrices)
        wk_p.append(k_p); wv_p.append(v_p); cv_p.append(c_p)
        ys, k_s, v_s, c_s = _sample_layer(layer, ys, ck_t, cv_t, cache_conv, cmk_t, cmv_t, sinks,
                                          weights)
        wk_s.append(k_s); wv_s.append(v_s); cv_s.append(c_s)
        matrices = dict(zip(_MATRIX_ORDER, next_matrices))

    return (yp, ys.reshape(dec_batch, dec_seq, D_MODEL),
            _token_major(jnp.stack(wk_p), KV_HEADS), _token_major(jnp.stack(wv_p), KV_HEADS),
            jnp.stack(cv_p),
            _token_major(mk_t, MEM_HEADS), _token_major(mv_t, MEM_HEADS),
            _token_major(jnp.stack(wk_s), KV_HEADS), _token_major(jnp.stack(wv_s), KV_HEADS),
            jnp.stack(cv_s))
```
